```python
import jax
import jax.numpy as jnp
from jax import lax
import numpy as np

D_MODEL = 2048
BATCH = 1
SEQ = 16384
DEPTH = 1

D_MIX = D_MODEL
RWKV_WIDTH = D_MIX // 2
RWKV_HEAD_DIM = 64
RWKV_HEADS = RWKV_WIDTH // RWKV_HEAD_DIM
LORA_RANK = 64
MOBA_WIDTH = D_MIX - RWKV_WIDTH
MOBA_HEAD_DIM = 128
MOBA_HEADS = MOBA_WIDTH // MOBA_HEAD_DIM
ROT_DIM = MOBA_HEAD_DIM // 4
ROPE_THETA = 500000.0
MOBA_BLOCK = 256
MOBA_TOPK = 3
Q_CHUNK = 64
RMS_EPS = 1e-6
LNX_EPS = 64e-5
NEG_INF = -1e30
IN_SPLITS = (RWKV_WIDTH, RWKV_WIDTH, RWKV_WIDTH, LORA_RANK, LORA_RANK, RWKV_WIDTH,
             MOBA_WIDTH, MOBA_WIDTH, MOBA_WIDTH, MOBA_WIDTH)
D_IN = sum(IN_SPLITS)

kernel_name = "hymba_rwkv7_moba_layer"


def rms_norm(x, w):
    xf = x.astype(jnp.float32)
    y = xf * lax.rsqrt(jnp.mean(xf * xf, axis=-1, keepdims=True) + RMS_EPS)
    return (y * w.astype(jnp.float32)).astype(x.dtype)


def token_shift(z, mu):
    z_prev = jnp.pad(z, ((0, 0), (1, 0), (0, 0)))[:, :-1]
    return z + mu * (z_prev - z)


def wkv7_scan(r, w, k, v, a, b):
    B, T, H, N = r.shape

    def step(S, inp):
        r_t, w_t, k_t, v_t, a_t, b_t = inp
        sa = jnp.einsum("bhij,bhj->bhi", S, a_t)
        S = (S * w_t[:, :, None, :] + sa[..., None] * b_t[:, :, None, :]
             + v_t[..., None] * k_t[:, :, None, :])
        y = jnp.einsum("bhij,bhj->bhi", S, r_t)
        return S, y

    seq = tuple(jnp.moveaxis(t, 1, 0) for t in (r, w, k, v, a, b))
    S0 = jnp.zeros((B, H, N, N), jnp.float32)
    _, ys = lax.scan(step, S0, seq)
    return jnp.moveaxis(ys, 0, 1)


def rwkv7_branch(r, k, v, wd, ad, mu_rkv, mu_lora, w_decay_up, w_decay_bias,
                 w_iclr_up, w_iclr_bias, k_k, k_a, r_k, lnx_w, lnx_b):
    B, T, _ = r.shape
    H, N = RWKV_HEADS, RWKV_HEAD_DIM
    r = token_shift(r, mu_rkv[0])
    k = token_shift(k, mu_rkv[1])
    v = token_shift(v, mu_rkv[2])
    wd = token_shift(wd, mu_lora[0])
    ad = token_shift(ad, mu_lora[1])
    w_log = -jax.nn.softplus(-(w_decay_bias + jnp.tanh(wd) @ w_decay_up)) - 0.5
    a = jax.nn.sigmoid(w_iclr_bias + ad @ w_iclr_up)
    kk = (k * k_k).reshape(B, T, H, N).astype(jnp.float32)
    kk = kk / jnp.maximum(jnp.sqrt(jnp.sum(kk * kk, axis=-1, keepdims=True)), 1e-12)
    k = k * (1.0 + (a - 1.0) * k_a)

    def heads(t):
        return t.reshape(B, T, H, N).astype(jnp.float32)

    rh, kh, vh, ah = heads(r), heads(k), heads(v), heads(a)
    decay = jnp.exp(-jnp.exp(heads(w_log)))
    y = wkv7_scan(rh, decay, kh, vh, -kk, kk * ah)
    mean = jnp.mean(y, axis=-1, keepdims=True)
    var = jnp.mean(jnp.square(y - mean), axis=-1, keepdims=True)
    y = ((y - mean) * lax.rsqrt(var + LNX_EPS)).reshape(B, T, H * N) * lnx_w + lnx_b
    bonus = jnp.sum(rh * kh * r_k, axis=-1, keepdims=True) * vh
    return (y + bonus.reshape(B, T, H * N)).astype(r.dtype)


def partial_rope(x, pos):
    half = ROT_DIM // 2
    inv_freq = ROPE_THETA ** (-jnp.arange(half, dtype=jnp.float32) / half)
    ang = pos[:, None] * inv_freq[None, :]
    cos = jnp.cos(ang)[None, :, None, :]
    sin = jnp.sin(ang)[None, :, None, :]
    xf = x.astype(jnp.float32)
    x1 = xf[..., :half]
    x2 = xf[..., half:ROT_DIM]
    out = jnp.concatenate([x1 * cos - x2 * sin, x2 * cos + x1 * sin, xf[..., ROT_DIM:]], axis=-1)
    return out.astype(x.dtype)


def moba_attention(q, k, v):
    B, T, H, Dh = q.shape
    nb = -(-T // MOBA_BLOCK)
    pad = nb * MOBA_BLOCK - T
    topk = min(MOBA_TOPK, nb)

    def to_blocks(t):
        t = jnp.pad(t, ((0, 0), (0, pad), (0, 0), (0, 0)))
        return t.reshape(B, nb, MOBA_BLOCK, H, Dh).transpose(0, 3, 1, 2, 4)

    kb, vb = to_blocks(k), to_blocks(v)
    k_mean = jnp.mean(kb.astype(jnp.float32), axis=3)
    b_idx = jnp.arange(B)[:, None, None, None]
    h_idx = jnp.arange(H)[None, None, :, None]
    scale = Dh ** -0.5

    def attend_chunk(ci):
        start = ci * Q_CHUNK
        qc = lax.dynamic_slice_in_dim(q, start, Q_CHUNK, axis=1).astype(jnp.float32)
        cur = start // MOBA_BLOCK
        q_pos = start + jnp.arange(Q_CHUNK)
        gate = jnp.einsum("bqhd,bhnd->bqhn", qc, k_mean)
        gate = jnp.where(jnp.arange(nb) < cur, gate, NEG_INF)
        _, sel = lax.top_k(gate, topk)
        valid = jnp.arange(topk) < cur
        k_sel = kb[b_idx, h_idx, sel].astype(jnp.float32)
        v_sel = vb[b_idx, h_idx, sel].astype(jnp.float32)
        s_sel = jnp.einsum("bqhd,bqhksd->bqhks", qc, k_sel) * scale
        s_sel = jnp.where(valid[:, None], s_sel, NEG_INF)
        k_own = lax.dynamic_index_in_dim(kb, cur, axis=2, keepdims=False).astype(jnp.float32)
        v_own = lax.dynamic_index_in_dim(vb, cur, axis=2, keepdims=False).astype(jnp.float32)
        s_own = jnp.einsum("bqhd,bhsd->bqhs", qc, k_own) * scale
        k_pos = cur * MOBA_BLOCK + jnp.arange(MOBA_BLOCK)
        causal = (k_pos[None, :] <= q_pos[:, None])[:, None, :]
        s_own = jnp.where(causal, s_own, NEG_INF)
        scores = jnp.concatenate([s_sel.reshape(B, Q_CHUNK, H, topk * MOBA_BLOCK), s_own], axis=-1)
        p = jax.nn.softmax(scores, axis=-1)
        p_sel = p[..., :topk * MOBA_BLOCK].reshape(B, Q_CHUNK, H, topk, MOBA_BLOCK)
        p_own = p[..., topk * MOBA_BLOCK:]
        out = (jnp.einsum("bqhks,bqhksd->bqhd", p_sel, v_sel)
               + jnp.einsum("bqhs,bhsd->bqhd", p_own, v_own))
        return out.astype(q.dtype)

    outs = lax.map(attend_chunk, jnp.arange(T // Q_CHUNK))
    return jnp.moveaxis(outs, 0, 1).reshape(B, T, H, Dh)


def setup_inputs(seed: int = 0) -> dict:
    key = jax.random.key(seed)
    ks = jax.random.split(key, 16)
    f32 = jnp.float32
    x = jax.random.normal(ks[0], (BATCH, SEQ, D_MODEL), f32)
    norm_w = 1.0 + 0.02 * jax.random.normal(ks[1], (DEPTH, D_MODEL), f32)
    w_in = jax.random.normal(ks[2], (DEPTH, D_MODEL, D_IN), f32) * D_MODEL ** -0.5
    mu_rkv = jax.random.uniform(ks[3], (DEPTH, 3, RWKV_WIDTH), f32)
    mu_lora = jax.random.uniform(ks[4], (DEPTH, 2, LORA_RANK), f32)
    w_decay_up = jax.random.normal(ks[5], (DEPTH, LORA_RANK, RWKV_WIDTH), f32) * (0.1 * LORA_RANK ** -0.5)
    w_decay_bias = jax.random.uniform(ks[6], (DEPTH, RWKV_WIDTH), f32, minval=-6.5, maxval=-1.5)
    w_iclr_up = jax.random.normal(ks[7], (DEPTH, LORA_RANK, RWKV_WIDTH), f32) * (0.1 * LORA_RANK ** -0.5)
    w_iclr_bias = 0.1 * jax.random.normal(ks[8], (DEPTH, RWKV_WIDTH), f32)
    k_k = 0.85 + 0.02 * jax.random.normal(ks[9], (DEPTH, RWKV_WIDTH), f32)
    k_a = 1.0 + 0.02 * jax.random.normal(ks[10], (DEPTH, RWKV_WIDTH), f32)
    r_k = 0.1 * jax.random.normal(ks[11], (DEPTH, RWKV_HEADS, RWKV_HEAD_DIM), f32)
    lnx_w = 1.0 + 0.02 * jax.random.normal(ks[12], (DEPTH, RWKV_WIDTH), f32)
    lnx_b = 0.02 * jax.random.normal(ks[13], (DEPTH, RWKV_WIDTH), f32)
    w_out = jax.random.normal(ks[14], (DEPTH, D_MIX, D_MODEL), f32) * D_MIX ** -0.5
    final_norm_w = 1.0 + 0.02 * jax.random.normal(ks[15], (D_MODEL,), f32)
    return {"x": x, "norm_w": norm_w, "w_in": w_in, "mu_rkv": mu_rkv, "mu_lora": mu_lora,
            "w_decay_up": w_decay_up, "w_decay_bias": w_decay_bias, "w_iclr_up": w_iclr_up,
            "w_iclr_bias": w_iclr_bias, "k_k": k_k, "k_a": k_a, "r_k": r_k, "lnx_w": lnx_w,
            "lnx_b": lnx_b, "w_out": w_out, "final_norm_w": final_norm_w}


def reference(x, norm_w, w_in, mu_rkv, mu_lora, w_decay_up, w_decay_bias, w_iclr_up,
              w_iclr_bias, k_k, k_a, r_k, lnx_w, lnx_b, w_out, final_norm_w):
    B, T, _ = x.shape
    pos = jnp.arange(T, dtype=jnp.float32)
    offsets = [int(o) for o in np.cumsum(IN_SPLITS)[:-1]]
    h = x
    for layer in range(DEPTH):
        u = rms_norm(h, norm_w[layer])
        proj = u @ w_in[layer]
        r, k_r, v_r, wd, ad, g_r, q_m, k_m, v_m, g_m = jnp.split(proj, offsets, axis=-1)
        y_r = rwkv7_branch(r, k_r, v_r, wd, ad, mu_rkv[layer], mu_lora[layer],
                           w_decay_up[layer], w_decay_bias[layer], w_iclr_up[layer],
                           w_iclr_bias[layer], k_k[layer], k_a[layer], r_k[layer],
                           lnx_w[layer], lnx_b[layer])
        q_m = partial_rope(q_m.reshape(B, T, MOBA_HEADS, MOBA_HEAD_DIM), pos)
        k_m = partial_rope(k_m.reshape(B, T, MOBA_HEADS, MOBA_HEAD_DIM), pos)
        v_m = v_m.reshape(B, T, MOBA_HEADS, MOBA_HEAD_DIM)
        y_m = moba_attention(q_m, k_m, v_m).reshape(B, T, MOBA_WIDTH)
        mixed = jnp.concatenate([y_r * jax.nn.silu(g_r), y_m * jax.nn.silu(g_m)], axis=-1)
        h = h + mixed @ w_out[layer]
    return rms_norm(h, final_norm_w)
```

```python
import functools

import jax
import jax.numpy as jnp
from jax import lax
from jax.experimental import pallas as pl
from jax.experimental.pallas import tpu as pltpu

F32 = jnp.float32
BF16 = jnp.bfloat16

D_MODEL = 2048
RWKV_WIDTH = 1024
RWKV_HEAD_DIM = 64
LORA_RANK = 64
MOBA_WIDTH = 1024
MOBA_HEAD_DIM = 128
MOBA_HEADS = 8
ROT_HALF = 16
ROPE_THETA = 500000.0
MOBA_BLOCK = 256
MOBA_TOPK = 3
RMS_EPS = 1e-6
LNX_EPS = 64e-5
NEG_INF = -1e30

LANES = 128
N_PAIRS = RWKV_WIDTH // LANES
CHUNK = 64
N_MAIN_BLOCKS = 8
VMEM_LIMIT = 56 * 1024 * 1024


def _dot(a, b, precision=None):
    return jnp.dot(a, b, preferred_element_type=F32, precision=precision)


def _dot_nt(a, b, precision=None):
    return lax.dot_general(a, b, (((1,), (1,)), ((), ())), preferred_element_type=F32,
                           precision=precision)


def _silu(g):
    return g / (1.0 + jnp.exp(-g))


def _proj_kernel(x_ref, nw_ref, wm_ref, wl_ref, om_ref, ol_ref, xn_ref):
    n = pl.program_id(1)

    @pl.when(n == 0)
    def _():
        x = x_ref[...]
        ms = jnp.mean(x * x, axis=-1, keepdims=True)
        xn_ref[...] = (x * lax.rsqrt(ms + RMS_EPS) * nw_ref[...]).astype(BF16)

    @pl.when(n < N_MAIN_BLOCKS)
    def _():
        om_ref[...] = _dot(xn_ref[...], wm_ref[...])

    @pl.when(n == N_MAIN_BLOCKS)
    def _():
        ol_ref[...] = _dot(xn_ref[...], wl_ref[...])


def _project(x2, norm_w, w_main, w_lora, tm):
    T = x2.shape[0]
    last = N_MAIN_BLOCKS - 1
    return pl.pallas_call(
        _proj_kernel,
        grid=(T // tm, N_MAIN_BLOCKS + 1),
        in_specs=[
            pl.BlockSpec((tm, D_MODEL), lambda i, n: (i, 0)),
            pl.BlockSpec((1, D_MODEL), lambda i, n: (0, 0)),
            pl.BlockSpec((D_MODEL, 1024), lambda i, n: (0, jnp.minimum(n, last))),
            pl.BlockSpec((D_MODEL, LANES), lambda i, n: (0, 0)),
        ],
        out_specs=[
            pl.BlockSpec((tm, 1024), lambda i, n: (i, jnp.minimum(n, last))),
            pl.BlockSpec((tm, LANES), lambda i, n: (i, 0)),
        ],
        out_shape=[
            jax.ShapeDtypeStruct((T, N_MAIN_BLOCKS * 1024), F32),
            jax.ShapeDtypeStruct((T, LANES), F32),
        ],
        scratch_shapes=[pltpu.VMEM((tm, D_MODEL), BF16)],
        compiler_params=pltpu.CompilerParams(
            dimension_semantics=("arbitrary", "arbitrary"), vmem_limit_bytes=VMEM_LIMIT),
        name="proj",
    )(x2, norm_w, w_main, w_lora)


_V_MU_R, _V_MU_K, _V_MU_V, _V_DEC_B, _V_ICL_B, _V_KK, _V_KA, _V_RK, _V_LNW, _V_LNB = range(10)
N_VEC = 16


def _to_pairs(x):
    return jnp.stack([x[:, LANES * p:LANES * (p + 1)] for p in range(N_PAIRS)], axis=0)


def _shift_rows(z, prev_row):
    rolled = pltpu.roll(z, 1, axis=0)
    row = lax.broadcasted_iota(jnp.int32, z.shape, 0)
    return jnp.where(row == 0, prev_row, rolled)


def _rwkv_kernel(r_ref, k_ref, v_ref, g_ref, lo_ref, vec_ref, mul_ref, wdec_ref, wicl_ref,
                 o_ref,
                 s_ref, prev_ref, prevl_ref, xa_ref, yb_ref, yk_ref, vb_ref, vf_ref, wb_ref,
                 pc_ref, bon_ref, yo_ref):
    C = CHUNK
    c_idx = pl.program_id(0)

    @pl.when(c_idx == 0)
    def _():
        s_ref[...] = jnp.zeros_like(s_ref)
        prev_ref[...] = jnp.zeros_like(prev_ref)
        prevl_ref[...] = jnp.zeros_like(prevl_ref)

    def vec(i):
        return vec_ref[i:i + 1, :]

    def shifted(ref, slot, mu):
        z = ref[...]
        zp = _shift_rows(z, prev_ref[slot:slot + 1, :])
        prev_ref[slot:slot + 1, :] = z[C - 1:C, :]
        return z + mu * (zp - z)

    r = shifted(r_ref, 0, vec(_V_MU_R))
    k = shifted(k_ref, 1, vec(_V_MU_K))
    v = shifted(v_ref, 2, vec(_V_MU_V))
    lo = lo_ref[...]
    lop = _shift_rows(lo, prevl_ref[0:1, :])
    prevl_ref[0:1, :] = lo[C - 1:C, :]
    lo = lo + mul_ref[...] * (lop - lo)

    lane = lax.broadcasted_iota(jnp.int32, (C, LANES), 1)
    lmix = jnp.where(lane < LORA_RANK, jnp.tanh(lo), lo)
    dec_in = _dot(lmix, wdec_ref[...], lax.Precision.HIGHEST)
    icl_in = _dot(lmix, wicl_ref[...], lax.Precision.HIGHEST)
    z = -(vec(_V_DEC_B) + dec_in)
    softplus = jnp.maximum(z, 0.0) + jnp.log(1.0 + jnp.exp(-jnp.abs(z)))
    w_log = -softplus - 0.5
    lw = -jnp.exp(w_log)
    a = 1.0 / (1.0 + jnp.exp(-(vec(_V_ICL_B) + icl_in)))

    ti = lax.broadcasted_iota(jnp.int32, (C, C), 0)
    si = lax.broadcasted_iota(jnp.int32, (C, C), 1)
    tril = jnp.where(si <= ti, 1.0, 0.0).astype(BF16)
    lw_hi = lw.astype(BF16)
    rem = lw - lw_hi.astype(F32)
    lw_mid = rem.astype(BF16)
    lw_lo = (rem - lw_mid.astype(F32)).astype(BF16)
    cl = _dot(tril, lw_hi) + _dot(tril, lw_mid) + _dot(tril, lw_lo)

    gi = lax.broadcasted_iota(jnp.int32, (LANES, LANES), 0)
    gj = lax.broadcasted_iota(jnp.int32, (LANES, LANES), 1)
    same_head = (gi < RWKV_HEAD_DIM) == (gj < RWKV_HEAD_DIM)
    ones_bd = jnp.where(same_head, 1.0, 0.0).astype(BF16)

    def head_sum(x3):
        flat = x3.reshape(N_PAIRS * C, LANES).astype(BF16)
        return _dot(flat, ones_bd).reshape(N_PAIRS, C, LANES)

    kk3 = _to_pairs(k * vec(_V_KK))
    kk3 = kk3 / jnp.maximum(jnp.sqrt(head_sum(kk3 * kk3)), 1e-12)
    k2 = k * (1.0 + (a - 1.0) * vec(_V_KA))

    r3, k3, v3, a3, cl3, lw3 = (_to_pairs(t) for t in (r, k2, v, a, cl, lw))
    p_in = jnp.exp(cl3)
    p_ex = jnp.exp(cl3 - lw3)
    p_inv = jnp.exp(-cl3)
    cl_end = cl3[:, C - 1:C, :]
    p_tail = jnp.exp(cl_end - cl3)
    b3 = kk3 * a3

    xa_ref[:, 0:C, :] = (-kk3 * p_ex).astype(BF16)
    xa_ref[:, C:2 * C, :] = (r3 * p_in).astype(BF16)
    yb_ref[...] = (b3 * p_inv).astype(BF16)
    yk_ref[...] = (k3 * p_inv).astype(BF16)
    vb_ref[...] = v3.astype(BF16)
    vf_ref[...] = v3
    wb_ref[:, 0:C, :] = (b3 * p_tail).astype(BF16)
    wb_ref[:, C:2 * C, :] = (k3 * p_tail).astype(BF16)
    pc_ref[...] = jnp.broadcast_to(jnp.exp(cl_end), pc_ref.shape)
    rk3 = _to_pairs(r * k2 * vec(_V_RK))
    bon_ref[...] = head_sum(rk3) * v3

    lo_half = lane < RWKV_HEAD_DIM
    t_idx = lax.broadcasted_iota(jnp.int32, (C, LANES), 0)
    s_idx = lane & (RWKV_HEAD_DIM - 1)
    strict = s_idx < t_idx
    incl = s_idx <= t_idx

    def bd(m):
        zero = jnp.zeros_like(m)
        return jnp.concatenate([jnp.where(lo_half, m, zero), jnp.where(lo_half, zero, m)], axis=0)

    def pair_body(p, carry):
        x = xa_ref[p]
        ycat = jnp.concatenate([bd(yb_ref[p]), bd(yk_ref[p])], axis=0)
        sc = _dot_nt(x, ycat)
        s_bd = s_ref[p]
        xs = _dot_nt(x, s_bd.astype(BF16))
        zero = jnp.zeros((C, LANES), F32)
        a_ab = jnp.where(strict, sc[0:C, 0:LANES], zero)
        a_ak = jnp.where(strict, sc[0:C, LANES:2 * LANES], zero)
        b_rb = jnp.where(incl, sc[C:2 * C, 0:LANES], zero)
        b_rk = jnp.where(incl, sc[C:2 * C, LANES:2 * LANES], zero)
        av = _dot(jnp.concatenate([a_ak, b_rk], axis=0).astype(BF16), bd(vb_ref[p]))
        u = xs[0:C] + av[0:C]
        ap = a_ab.astype(BF16)
        n_steps = 6
        for step in range(n_steps):
            u = u + _dot(ap, bd(u.astype(BF16)))
            if step < n_steps - 1:
                ap = _dot(ap, bd(ap)).astype(BF16)
        y = xs[C:2 * C] + av[C:2 * C] + _dot(b_rb.astype(BF16), bd(u.astype(BF16)))
        uv = jnp.concatenate([u, vf_ref[p]], axis=0)
        zmat = _dot(uv.T.astype(BF16), wb_ref[p])
        s_ref[p] = s_bd * pc_ref[p][0:1, :] + jnp.where(same_head, zmat, jnp.zeros_like(zmat))
        mean = _dot(y.astype(BF16), ones_bd) * (1.0 / RWKV_HEAD_DIM)
        d = y - mean
        var = _dot((d * d).astype(BF16), ones_bd) * (1.0 / RWKV_HEAD_DIM)
        yo_ref[p] = d * lax.rsqrt(var + LNX_EPS)
        return carry

    lax.fori_loop(0, N_PAIRS, pair_body, 0)

    gate = _silu(g_ref[...])
    lnw = vec(_V_LNW)
    lnb = vec(_V_LNB)
    for p in range(N_PAIRS):
        sl = slice(LANES * p, LANES * (p + 1))
        o_ref[:, sl] = (yo_ref[p] * lnw[:, sl] + lnb[:, sl] + bon_ref[p]) * gate[:, sl]


def _rwkv(proj_main, proj_lora, vec_tab, mu_l, wdec, wicl):
    T = proj_main.shape[0]
    C = CHUNK
    pair_bf = lambda rows: pltpu.VMEM((N_PAIRS, rows, LANES), BF16)
    pair_f32 = lambda rows: pltpu.VMEM((N_PAIRS, rows, LANES), F32)
    return pl.pallas_call(
        _rwkv_kernel,
        grid=(T // C,),
        in_specs=[
            pl.BlockSpec((C, 1024), lambda c: (c, 0)),
            pl.BlockSpec((C, 1024), lambda c: (c, 1)),
            pl.BlockSpec((C, 1024), lambda c: (c, 2)),
            pl.BlockSpec((C, 1024), lambda c: (c, 3)),
            pl.BlockSpec((C, LANES), lambda c: (c, 0)),
            pl.BlockSpec((N_VEC, 1024), lambda c: (0, 0)),
            pl.BlockSpec((1, LANES), lambda c: (0, 0)),
            pl.BlockSpec((LANES, 1024), lambda c: (0, 0)),
            pl.BlockSpec((LANES, 1024), lambda c: (0, 0)),
        ],
        out_specs=pl.BlockSpec((C, 1024), lambda c: (c, 0)),
        out_shape=jax.ShapeDtypeStruct((T, RWKV_WIDTH), F32),
        scratch_shapes=[
            pltpu.VMEM((N_PAIRS, LANES, LANES), F32),
            pltpu.VMEM((8, 1024), F32),
            pltpu.VMEM((8, LANES), F32),
            pair_bf(2 * C), pair_bf(C), pair_bf(C), pair_bf(C), pair_f32(C), pair_bf(2 * C),
            pair_f32(8), pair_f32(C), pair_f32(C),
        ],
        compiler_params=pltpu.CompilerParams(
            dimension_semantics=("arbitrary",), vmem_limit_bytes=VMEM_LIMIT),
        name="rwkv",
    )(proj_main, proj_main, proj_main, proj_main, proj_lora, vec_tab, mu_l, wdec, wicl)


def _moba_prep_kernel(q_ref, k_ref, v_ref, invf_ref, qs_ref, kr_ref, vt_ref, bias_ref, km_ref, *, nb):
    b = pl.program_id(0)
    BS = MOBA_BLOCK

    @pl.when(b == 0)
    def _():
        km_ref[...] = jnp.zeros_like(km_ref)

    lane = lax.broadcasted_iota(jnp.int32, (BS, LANES), 1)
    pos = (b * BS + lax.broadcasted_iota(jnp.int32, (BS, LANES), 0)).astype(F32)
    ang = pos * invf_ref[...]
    cos = jnp.cos(ang)
    sin = jnp.sin(ang)
    cos_f = jnp.where(lane < 2 * ROT_HALF, cos, 1.0)
    sin_a = jnp.where(lane < ROT_HALF, -sin, 0.0)
    sin_b = jnp.where((lane >= ROT_HALF) & (lane < 2 * ROT_HALF), sin, 0.0)

    def rope(x):
        return (x * cos_f + pltpu.roll(x, LANES - ROT_HALF, axis=1) * sin_a
                + pltpu.roll(x, ROT_HALF, axis=1) * sin_b)

    jidx = lax.broadcasted_iota(jnp.int32, (nb, BS), 0)
    past = jidx < b
    scale = MOBA_HEAD_DIM ** -0.5
    for h in range(MOBA_HEADS):
        sl = slice(LANES * h, LANES * (h + 1))
        qh = rope(q_ref[:, sl])
        kh = rope(k_ref[:, sl])
        qs_ref[h] = (qh * scale).astype(BF16)
        kr_ref[h, 0] = kh.astype(BF16)
        vt_ref[h, 0] = v_ref[:, sl].T.astype(BF16)
        gate = _dot_nt(km_ref[h], qh, lax.Precision.HIGHEST)
        gate = jnp.where(past, gate, NEG_INF)
        sel = jnp.zeros((nb, BS), jnp.bool_)
        for _ in range(MOBA_TOPK):
            m = jnp.max(gate, axis=0, keepdims=True)
            first = jnp.min(jnp.where(gate == m, jidx, nb), axis=0, keepdims=True)
            hit = jidx == first
            sel = sel | (hit & past)
            gate = jnp.where(hit, -3e38, gate)
        bias_ref[h, 0] = jnp.where(sel, 0.0, NEG_INF)
        km_ref[h, pl.ds(b, 1), :] = jnp.mean(kh, axis=0, keepdims=True)


def _moba_prep(proj_main, invf):
    T = proj_main.shape[0]
    nb = T // MOBA_BLOCK
    BS = MOBA_BLOCK
    H = MOBA_HEADS
    return pl.pallas_call(
        functools.partial(_moba_prep_kernel, nb=nb),
        grid=(nb,),
        in_specs=[
            pl.BlockSpec((BS, 1024), lambda b: (b, 4)),
            pl.BlockSpec((BS, 1024), lambda b: (b, 5)),
            pl.BlockSpec((BS, 1024), lambda b: (b, 6)),
            pl.BlockSpec((1, LANES), lambda b: (0, 0)),
        ],
        out_specs=[
            pl.BlockSpec((H, BS, LANES), lambda b: (0, b, 0)),
            pl.BlockSpec((H, 1, BS, LANES), lambda b: (0, b, 0, 0)),
            pl.BlockSpec((H, 1, LANES, BS), lambda b: (0, b, 0, 0)),
            pl.BlockSpec((H, 1, nb, BS), lambda b: (0, b, 0, 0)),
        ],
        out_shape=[
            jax.ShapeDtypeStruct((H, T, LANES), BF16),
            jax.ShapeDtypeStruct((H, nb, BS, LANES), BF16),
            jax.ShapeDtypeStruct((H, nb, LANES, BS), BF16),
            jax.ShapeDtypeStruct((H, nb, nb, BS), F32),
        ],
        scratch_shapes=[pltpu.VMEM((H, nb, LANES), F32)],
        compiler_params=pltpu.CompilerParams(
            dimension_semantics=("arbitrary",), vmem_limit_bytes=VMEM_LIMIT),
        name="moba_prep",
    )(proj_main, proj_main, proj_main, invf)


def _moba_attn_kernel(q_ref, k_ref, vt_ref, bias_ref, o_ref):
    i = pl.program_id(1)
    BS = MOBA_BLOCK
    q = q_ref[0]

    s = _dot_nt(k_ref[0, i], q)
    kidx = lax.broadcasted_iota(jnp.int32, (BS, BS), 0)
    qidx = lax.broadcasted_iota(jnp.int32, (BS, BS), 1)
    s = jnp.where(kidx <= qidx, s, NEG_INF)
    m = jnp.max(s, axis=0, keepdims=True)
    p = jnp.exp(s - m)
    l = jnp.sum(p, axis=0, keepdims=True)
    acc = _dot(vt_ref[0, i], p.astype(BF16))

    def body(j, carry):
        m, l, acc = carry
        s = _dot_nt(k_ref[0, j], q) + bias_ref[0, 0, pl.ds(j, 1), :]
        m_new = jnp.maximum(m, jnp.max(s, axis=0, keepdims=True))
        alpha = jnp.exp(m - m_new)
        p = jnp.exp(s - m_new)
        l = alpha * l + jnp.sum(p, axis=0, keepdims=True)
        acc = alpha * acc + _dot(vt_ref[0, j], p.astype(BF16))
        return m_new, l, acc

    m, l, acc = lax.fori_loop(0, i, body, (m, l, acc))
    o_ref[...] = (acc / l).T


def _moba_attn(qs, kr, vt, bias):
    H, T, _ = qs.shape
    nb = T // MOBA_BLOCK
    BS = MOBA_BLOCK
    return pl.pallas_call(
        _moba_attn_kernel,
        grid=(H, nb),
        in_specs=[
            pl.BlockSpec((1, BS, LANES), lambda h, i: (h, i, 0)),
            pl.BlockSpec((1, nb, BS, LANES), lambda h, i: (h, 0, 0, 0)),
            pl.BlockSpec((1, nb, LANES, BS), lambda h, i: (h, 0, 0, 0)),
            pl.BlockSpec((1, 1, nb, BS), lambda h, i: (h, i, 0, 0)),
        ],
        out_specs=pl.BlockSpec((BS, LANES), lambda h, i: (i, h)),
        out_shape=jax.ShapeDtypeStruct((T, MOBA_WIDTH), F32),
        compiler_params=pltpu.CompilerParams(
            dimension_semantics=("arbitrary", "arbitrary"), vmem_limit_bytes=VMEM_LIMIT),
        name="moba_attn",
    )(qs, kr, vt, bias)


def _out_kernel(mr_ref, ym_ref, gm_ref, x_ref, wt_ref, wb_ref, fw_ref, o_ref):
    mm = (ym_ref[...] * _silu(gm_ref[...])).astype(BF16)
    h = x_ref[...] + _dot(mr_ref[...].astype(BF16), wt_ref[...]) + _dot(mm, wb_ref[...])
    ms = jnp.mean(h * h, axis=-1, keepdims=True)
    o_ref[...] = h * lax.rsqrt(ms + RMS_EPS) * fw_ref[...]


def _out_proj(mixed_r, y_m, proj_main, x2, w_top, w_bot, fw, tm):
    T = x2.shape[0]
    return pl.pallas_call(
        _out_kernel,
        grid=(T // tm,),
        in_specs=[
            pl.BlockSpec((tm, 1024), lambda i: (i, 0)),
            pl.BlockSpec((tm, 1024), lambda i: (i, 0)),
            pl.BlockSpec((tm, 1024), lambda i: (i, 7)),
            pl.BlockSpec((tm, D_MODEL), lambda i: (i, 0)),
            pl.BlockSpec((1024, D_MODEL), lambda i: (0, 0)),
            pl.BlockSpec((1024, D_MODEL), lambda i: (0, 0)),
            pl.BlockSpec((1, D_MODEL), lambda i: (0, 0)),
        ],
        out_specs=pl.BlockSpec((tm, D_MODEL), lambda i: (i, 0)),
        out_shape=jax.ShapeDtypeStruct((T, D_MODEL), F32),
        compiler_params=pltpu.CompilerParams(
            dimension_semantics=("arbitrary",), vmem_limit_bytes=VMEM_LIMIT),
        name="out_proj",
    )(mixed_r, y_m, proj_main, x2, w_top, w_bot, fw)


def kernel(x, norm_w, w_in, mu_rkv, mu_lora, w_decay_up, w_decay_bias, w_iclr_up, w_iclr_bias,
           k_k, k_a, r_k, lnx_w, lnx_b, w_out, final_norm_w):
    B, T, _ = x.shape
    assert B == 1 and T % MOBA_BLOCK == 0
    x2 = x.reshape(T, D_MODEL)

    w = w_in[0]
    lora0 = 3 * RWKV_WIDTH
    lora1 = lora0 + 2 * LORA_RANK
    w_main = jnp.concatenate([w[:, :lora0], w[:, lora1:]], axis=1).astype(BF16)
    w_lora = w[:, lora0:lora1].astype(BF16)
    tm_proj = min(1024, T)
    proj_main, proj_lora = _project(x2, norm_w[0:1], w_main, w_lora, tm_proj)

    rows = [mu_rkv[0, 0], mu_rkv[0, 1], mu_rkv[0, 2], w_decay_bias[0], w_iclr_bias[0], k_k[0], k_a[0],
            r_k[0].reshape(-1), lnx_w[0], lnx_b[0]]
    vec_tab = jnp.stack(rows + [jnp.zeros_like(rows[0])] * (N_VEC - len(rows)), axis=0)
    mu_l = mu_lora[0].reshape(1, 2 * LORA_RANK)
    zeros_up = jnp.zeros_like(w_decay_up[0])
    wdec = jnp.concatenate([w_decay_up[0], zeros_up], axis=0)
    wicl = jnp.concatenate([zeros_up, w_iclr_up[0]], axis=0)
    mixed_r = _rwkv(proj_main, proj_lora, vec_tab, mu_l, wdec, wicl)

    lane = jnp.arange(LANES)
    inv_freq = ROPE_THETA ** (-(lane % ROT_HALF).astype(F32) / ROT_HALF)
    qs, kr, vt, bias = _moba_prep(proj_main, inv_freq.reshape(1, LANES))
    y_m = _moba_attn(qs, kr, vt, bias)

    wo = w_out[0].astype(BF16)
    tm_out = min(256, T)
    out = _out_proj(mixed_r, y_m, proj_main, x2, wo[:RWKV_WIDTH], wo[RWKV_WIDTH:],
                    final_norm_w.reshape(1, D_MODEL), tm_out)
    return out.reshape(B, T, D_MODEL)
```

```python
import functools

import jax
import jax.numpy as jnp
from jax import lax
from jax.experimental import pallas as pl
from jax.experimental.pallas import tpu as pltpu

F32 = jnp.float32
BF16 = jnp.bfloat16

D_MODEL = 2048
RWKV_WIDTH = 1024
RWKV_HEAD_DIM = 64
LORA_RANK = 64
MOBA_WIDTH = 1024
MOBA_HEAD_DIM = 128
MOBA_HEADS = 8
ROT_HALF = 16
ROPE_THETA = 500000.0
MOBA_BLOCK = 256
MOBA_TOPK = 3
RMS_EPS = 1e-6
LNX_EPS = 64e-5
NEG_INF = -1e30

LANES = 128
N_PAIRS = RWKV_WIDTH // LANES
CHUNK = 64
N_MAIN_BLOCKS = 8
VMEM_LIMIT = 56 * 1024 * 1024
ATTN_HEADS = 4
LOG2E = 1.4426950408889634


def _dot(a, b, precision=None):
    return jnp.dot(a, b, preferred_element_type=F32, precision=precision)


def _dot_nt(a, b, precision=None):
    return lax.dot_general(a, b, (((1,), (1,)), ((), ())), preferred_element_type=F32,
                           precision=precision)


def _silu(g):
    return g / (1.0 + jnp.exp(-g))


def _proj_kernel(x_ref, nw_ref, wm_ref, wl_ref, om_ref, ol_ref, xn_ref):
    n = pl.program_id(1)

    @pl.when(n == 0)
    def _():
        x = x_ref[...]
        ms = jnp.mean(x * x, axis=-1, keepdims=True)
        xn_ref[...] = (x * lax.rsqrt(ms + RMS_EPS) * nw_ref[...]).astype(BF16)

    @pl.when(n < N_MAIN_BLOCKS)
    def _():
        om_ref[...] = _dot(xn_ref[...], wm_ref[...])

    @pl.when(n == N_MAIN_BLOCKS)
    def _():
        ol_ref[...] = _dot(xn_ref[...], wl_ref[...])


def _project(x2, norm_w, w_main, w_lora, tm):
    T = x2.shape[0]
    last = N_MAIN_BLOCKS - 1
    return pl.pallas_call(
        _proj_kernel,
        grid=(T // tm, N_MAIN_BLOCKS + 1),
        in_specs=[
            pl.BlockSpec((tm, D_MODEL), lambda i, n: (i, 0)),
            pl.BlockSpec((1, D_MODEL), lambda i, n: (0, 0)),
            pl.BlockSpec((D_MODEL, 1024), lambda i, n: (0, jnp.minimum(n, last))),
            pl.BlockSpec((D_MODEL, LANES), lambda i, n: (0, 0)),
        ],
        out_specs=[
            pl.BlockSpec((tm, 1024), lambda i, n: (i, jnp.minimum(n, last))),
            pl.BlockSpec((tm, LANES), lambda i, n: (i, 0)),
        ],
        out_shape=[
            jax.ShapeDtypeStruct((T, N_MAIN_BLOCKS * 1024), F32),
            jax.ShapeDtypeStruct((T, LANES), F32),
        ],
        scratch_shapes=[pltpu.VMEM((tm, D_MODEL), BF16)],
        compiler_params=pltpu.CompilerParams(
            dimension_semantics=("arbitrary", "arbitrary"), vmem_limit_bytes=VMEM_LIMIT),
        name="proj",
    )(x2, norm_w, w_main, w_lora)


_V_MU_R, _V_MU_K, _V_MU_V, _V_DEC_B, _V_ICL_B, _V_KK, _V_KA, _V_RK, _V_LNW, _V_LNB = range(10)
N_VEC = 16


def _to_pairs(x):
    return jnp.stack([x[:, LANES * p:LANES * (p + 1)] for p in range(N_PAIRS)], axis=0)


def _shift_rows(z, prev_row):
    rolled = pltpu.roll(z, 1, axis=0)
    row = lax.broadcasted_iota(jnp.int32, z.shape, 0)
    return jnp.where(row == 0, prev_row, rolled)


def _rwkv_kernel(r_ref, k_ref, v_ref, g_ref, lo_ref, vec_ref, mul_ref, wdec_ref, wicl_ref,
                 o_ref,
                 s_ref, prev_ref, prevl_ref, xa_ref, yb_ref, yk_ref, vb_ref, vf_ref, wb_ref,
                 pc_ref, bon_ref, yo_ref):
    C = CHUNK
    c_idx = pl.program_id(0)

    @pl.when(c_idx == 0)
    def _():
        s_ref[...] = jnp.zeros_like(s_ref)
        prev_ref[...] = jnp.zeros_like(prev_ref)
        prevl_ref[...] = jnp.zeros_like(prevl_ref)

    def vec(i):
        return vec_ref[i:i + 1, :]

    def shifted(ref, slot, mu):
        z = ref[...]
        zp = _shift_rows(z, prev_ref[slot:slot + 1, :])
        prev_ref[slot:slot + 1, :] = z[C - 1:C, :]
        return z + mu * (zp - z)

    r = shifted(r_ref, 0, vec(_V_MU_R))
    k = shifted(k_ref, 1, vec(_V_MU_K))
    v = shifted(v_ref, 2, vec(_V_MU_V))
    lo = lo_ref[...]
    lop = _shift_rows(lo, prevl_ref[0:1, :])
    prevl_ref[0:1, :] = lo[C - 1:C, :]
    lo = lo + mul_ref[...] * (lop - lo)

    lane = lax.broadcasted_iota(jnp.int32, (C, LANES), 1)
    lmix = jnp.where(lane < LORA_RANK, jnp.tanh(lo), lo)
    dec_in = _dot(lmix, wdec_ref[...], lax.Precision.HIGHEST)
    icl_in = _dot(lmix, wicl_ref[...], lax.Precision.HIGHEST)
    z = -(vec(_V_DEC_B) + dec_in)
    softplus = jnp.maximum(z, 0.0) + jnp.log(1.0 + jnp.exp(-jnp.abs(z)))
    w_log = -softplus - 0.5
    lw = -jnp.exp(w_log)
    a = 1.0 / (1.0 + jnp.exp(-(vec(_V_ICL_B) + icl_in)))

    ti = lax.broadcasted_iota(jnp.int32, (C, C), 0)
    si = lax.broadcasted_iota(jnp.int32, (C, C), 1)
    tril = jnp.where(si <= ti, 1.0, 0.0).astype(BF16)
    lw_hi = lw.astype(BF16)
    rem = lw - lw_hi.astype(F32)
    lw_mid = rem.astype(BF16)
    lw_lo = (rem - lw_mid.astype(F32)).astype(BF16)
    cl = _dot(tril, lw_hi) + _dot(tril, lw_mid) + _dot(tril, lw_lo)

    gi = lax.broadcasted_iota(jnp.int32, (LANES, LANES), 0)
    gj = lax.broadcasted_iota(jnp.int32, (LANES, LANES), 1)
    same_head = (gi < RWKV_HEAD_DIM) == (gj < RWKV_HEAD_DIM)
    ones_bd = jnp.where(same_head, 1.0, 0.0).astype(BF16)

    def head_sum(x3):
        flat = x3.reshape(N_PAIRS * C, LANES).astype(BF16)
        return _dot(flat, ones_bd).reshape(N_PAIRS, C, LANES)

    kk3 = _to_pairs(k * vec(_V_KK))
    kk3 = kk3 / jnp.maximum(jnp.sqrt(head_sum(kk3 * kk3)), 1e-12)
    k2 = k * (1.0 + (a - 1.0) * vec(_V_KA))

    r3, k3, v3, a3, cl3, lw3 = (_to_pairs(t) for t in (r, k2, v, a, cl, lw))
    p_in = jnp.exp(cl3)
    p_ex = jnp.exp(cl3 - lw3)
    p_inv = jnp.exp(-cl3)
    cl_end = cl3[:, C - 1:C, :]
    p_tail = jnp.exp(cl_end - cl3)
    b3 = kk3 * a3

    xa_ref[:, 0:C, :] = (-kk3 * p_ex).astype(BF16)
    xa_ref[:, C:2 * C, :] = (r3 * p_in).astype(BF16)
    yb_ref[...] = (b3 * p_inv).astype(BF16)
    yk_ref[...] = (k3 * p_inv).astype(BF16)
    vb_ref[...] = v3.astype(BF16)
    vf_ref[...] = v3
    wb_ref[:, 0:C, :] = (b3 * p_tail).astype(BF16)
    wb_ref[:, C:2 * C, :] = (k3 * p_tail).astype(BF16)
    pc_ref[...] = jnp.broadcast_to(jnp.exp(cl_end), pc_ref.shape)
    rk3 = _to_pairs(r * k2 * vec(_V_RK))
    bon_ref[...] = head_sum(rk3) * v3

    lo_half = lane < RWKV_HEAD_DIM
    t_idx = lax.broadcasted_iota(jnp.int32, (C, LANES), 0)
    s_idx = lane & (RWKV_HEAD_DIM - 1)
    strict = s_idx < t_idx
    incl = s_idx <= t_idx

    def bd(m):
        zero = jnp.zeros_like(m)
        return jnp.concatenate([jnp.where(lo_half, m, zero), jnp.where(lo_half, zero, m)], axis=0)

    P = range(N_PAIRS)
    zero = jnp.zeros((C, LANES), F32)
    x = [xa_ref[p] for p in P]
    sc = [_dot_nt(x[p], jnp.concatenate([bd(yb_ref[p]), bd(yk_ref[p])], axis=0)) for p in P]
    xs = [_dot_nt(x[p], s_ref[p].astype(BF16)) for p in P]
    a_ab = [jnp.where(strict, sc[p][0:C, 0:LANES], zero).astype(BF16) for p in P]
    b_rb = [jnp.where(incl, sc[p][C:2 * C, 0:LANES], zero).astype(BF16) for p in P]
    akrk = [jnp.concatenate([jnp.where(strict, sc[p][0:C, LANES:2 * LANES], zero),
                             jnp.where(incl, sc[p][C:2 * C, LANES:2 * LANES], zero)],
                            axis=0).astype(BF16) for p in P]
    av = [_dot(akrk[p], bd(vb_ref[p])) for p in P]
    u = [xs[p][0:C] + av[p][0:C] for p in P]
    ap = a_ab
    n_steps = 6
    for step in range(n_steps):
        u = [u[p] + _dot(ap[p], bd(u[p].astype(BF16))) for p in P]
        if step < n_steps - 1:
            ap = [_dot(ap[p], bd(ap[p])).astype(BF16) for p in P]
    y = [xs[p][C:2 * C] + av[p][C:2 * C] + _dot(b_rb[p], bd(u[p].astype(BF16))) for p in P]
    uvt = [jnp.concatenate([u[p], vf_ref[p]], axis=0).T.astype(BF16) for p in P]
    zmat = [_dot(uvt[p], wb_ref[p]) for p in P]
    for p in P:
        s_ref[p] = s_ref[p] * pc_ref[p][0:1, :] + jnp.where(same_head, zmat[p], jnp.zeros_like(zmat[p]))
    inv_n = 1.0 / RWKV_HEAD_DIM
    d = [y[p] - _dot(y[p].astype(BF16), ones_bd) * inv_n for p in P]
    var = [_dot((d[p] * d[p]).astype(BF16), ones_bd) * inv_n for p in P]
    for p in P:
        yo_ref[p] = d[p] * lax.rsqrt(var[p] + LNX_EPS)

    gate = _silu(g_ref[...])
    lnw = vec(_V_LNW)
    lnb = vec(_V_LNB)
    for p in range(N_PAIRS):
        sl = slice(LANES * p, LANES * (p + 1))
        o_ref[:, sl] = (yo_ref[p] * lnw[:, sl] + lnb[:, sl] + bon_ref[p]) * gate[:, sl]


def _rwkv(proj_main, proj_lora, vec_tab, mu_l, wdec, wicl):
    T = proj_main.shape[0]
    C = CHUNK
    pair_bf = lambda rows: pltpu.VMEM((N_PAIRS, rows, LANES), BF16)
    pair_f32 = lambda rows: pltpu.VMEM((N_PAIRS, rows, LANES), F32)
    return pl.pallas_call(
        _rwkv_kernel,
        grid=(T // C,),
        in_specs=[
            pl.BlockSpec((C, 1024), lambda c: (c, 0)),
            pl.BlockSpec((C, 1024), lambda c: (c, 1)),
            pl.BlockSpec((C, 1024), lambda c: (c, 2)),
            pl.BlockSpec((C, 1024), lambda c: (c, 3)),
            pl.BlockSpec((C, LANES), lambda c: (c, 0)),
            pl.BlockSpec((N_VEC, 1024), lambda c: (0, 0)),
            pl.BlockSpec((1, LANES), lambda c: (0, 0)),
            pl.BlockSpec((LANES, 1024), lambda c: (0, 0)),
            pl.BlockSpec((LANES, 1024), lambda c: (0, 0)),
        ],
        out_specs=pl.BlockSpec((C, 1024), lambda c: (c, 0)),
        out_shape=jax.ShapeDtypeStruct((T, RWKV_WIDTH), F32),
        scratch_shapes=[
            pltpu.VMEM((N_PAIRS, LANES, LANES), F32),
            pltpu.VMEM((8, 1024), F32),
            pltpu.VMEM((8, LANES), F32),
            pair_bf(2 * C), pair_bf(C), pair_bf(C), pair_bf(C), pair_f32(C), pair_bf(2 * C),
            pair_f32(8), pair_f32(C), pair_f32(C),
        ],
        compiler_params=pltpu.CompilerParams(
            dimension_semantics=("arbitrary",), vmem_limit_bytes=VMEM_LIMIT),
        name="rwkv",
    )(proj_main, proj_main, proj_main, proj_main, proj_lora, vec_tab, mu_l, wdec, wicl)


def _moba_prep_kernel(q_ref, k_ref, v_ref, invf_ref, qs_ref, kr_ref, vt_ref, bias_ref, km_ref, *, nb):
    b = pl.program_id(0)
    BS = MOBA_BLOCK

    @pl.when(b == 0)
    def _():
        km_ref[...] = jnp.zeros_like(km_ref)

    lane = lax.broadcasted_iota(jnp.int32, (BS, LANES), 1)
    pos = (b * BS + lax.broadcasted_iota(jnp.int32, (BS, LANES), 0)).astype(F32)
    ang = pos * invf_ref[...]
    cos = jnp.cos(ang)
    sin = jnp.sin(ang)
    cos_f = jnp.where(lane < 2 * ROT_HALF, cos, 1.0)
    sin_a = jnp.where(lane < ROT_HALF, -sin, 0.0)
    sin_b = jnp.where((lane >= ROT_HALF) & (lane < 2 * ROT_HALF), sin, 0.0)

    def rope(x):
        return (x * cos_f + pltpu.roll(x, LANES - ROT_HALF, axis=1) * sin_a
                + pltpu.roll(x, ROT_HALF, axis=1) * sin_b)

    jidx = lax.broadcasted_iota(jnp.int32, (nb, BS), 0)
    past = jidx < b
    scale = MOBA_HEAD_DIM ** -0.5
    for h in range(MOBA_HEADS):
        sl = slice(LANES * h, LANES * (h + 1))
        qh = rope(q_ref[:, sl])
        kh = rope(k_ref[:, sl])
        qs_ref[h] = (qh * (scale * LOG2E)).astype(BF16)
        kr_ref[h, 0] = kh.astype(BF16)
        vt_ref[h, 0] = v_ref[:, sl].T.astype(BF16)
        gate = _dot_nt(km_ref[h], qh, lax.Precision.HIGHEST)
        gate = jnp.where(past, gate, NEG_INF)
        sel = jnp.zeros((nb, BS), jnp.bool_)
        for _ in range(MOBA_TOPK):
            m = jnp.max(gate, axis=0, keepdims=True)
            first = jnp.min(jnp.where(gate == m, jidx, nb), axis=0, keepdims=True)
            hit = jidx == first
            sel = sel | (hit & past)
            gate = jnp.where(hit, -3e38, gate)
        bias_ref[h, 0] = jnp.where(sel, 0.0, NEG_INF)
        km_ref[h, pl.ds(b, 1), :] = jnp.mean(kh, axis=0, keepdims=True)


def _moba_prep(proj_main, invf):
    T = proj_main.shape[0]
    nb = T // MOBA_BLOCK
    BS = MOBA_BLOCK
    H = MOBA_HEADS
    return pl.pallas_call(
        functools.partial(_moba_prep_kernel, nb=nb),
        grid=(nb,),
        in_specs=[
            pl.BlockSpec((BS, 1024), lambda b: (b, 4)),
            pl.BlockSpec((BS, 1024), lambda b: (b, 5)),
            pl.BlockSpec((BS, 1024), lambda b: (b, 6)),
            pl.BlockSpec((1, LANES), lambda b: (0, 0)),
        ],
        out_specs=[
            pl.BlockSpec((H, BS, LANES), lambda b: (0, b, 0)),
            pl.BlockSpec((H, 1, BS, LANES), lambda b: (0, b, 0, 0)),
            pl.BlockSpec((H, 1, LANES, BS), lambda b: (0, b, 0, 0)),
            pl.BlockSpec((H, 1, nb, BS), lambda b: (0, b, 0, 0)),
        ],
        out_shape=[
            jax.ShapeDtypeStruct((H, T, LANES), BF16),
            jax.ShapeDtypeStruct((H, nb, BS, LANES), BF16),
            jax.ShapeDtypeStruct((H, nb, LANES, BS), BF16),
            jax.ShapeDtypeStruct((H, nb, nb, BS), F32),
        ],
        scratch_shapes=[pltpu.VMEM((H, nb, LANES), F32)],
        compiler_params=pltpu.CompilerParams(
            dimension_semantics=("arbitrary",), vmem_limit_bytes=VMEM_LIMIT),
        name="moba_prep",
    )(proj_main, proj_main, proj_main, invf)


def _moba_attn_kernel(q_ref, k_ref, vt_ref, bias_ref, o_ref, m_ref, l_ref, acc_ref):
    i = pl.program_id(1)
    BS = MOBA_BLOCK
    kidx = lax.broadcasted_iota(jnp.int32, (BS, BS), 0)
    qidx = lax.broadcasted_iota(jnp.int32, (BS, BS), 1)
    causal = kidx <= qidx

    for g in range(ATTN_HEADS):
        s = jnp.where(causal, _dot_nt(k_ref[g, i], q_ref[g]), NEG_INF)
        m = jnp.max(s, axis=0, keepdims=True)
        p = jnp.exp2(s - m)
        m_ref[g] = m
        l_ref[g] = jnp.sum(p, axis=0, keepdims=True)
        acc_ref[g] = _dot(vt_ref[g, i], p.astype(BF16))

    def body(j, carry):
        G = range(ATTN_HEADS)
        s = [_dot_nt(k_ref[g, j], q_ref[g]) for g in G]
        pb, alpha = [], []
        for g in G:
            sg = s[g] + bias_ref[g, 0, pl.ds(j, 1), :]
            m = m_ref[g]
            m_new = jnp.maximum(m, jnp.max(sg, axis=0, keepdims=True))
            a = jnp.exp2(m - m_new)
            p = jnp.exp2(sg - m_new)
            m_ref[g] = m_new
            l_ref[g] = a * l_ref[g] + jnp.sum(p, axis=0, keepdims=True)
            pb.append(p.astype(BF16))
            alpha.append(a)
        pv = [_dot(vt_ref[g, j], pb[g]) for g in G]
        for g in G:
            acc_ref[g] = alpha[g] * acc_ref[g] + pv[g]
        return carry

    lax.fori_loop(0, i, body, 0)
    for g in range(ATTN_HEADS):
        o_ref[:, LANES * g:LANES * (g + 1)] = (acc_ref[g] / l_ref[g]).T


def _moba_attn(qs, kr, vt, bias):
    H, T, _ = qs.shape
    nb = T // MOBA_BLOCK
    BS = MOBA_BLOCK
    G = ATTN_HEADS
    resident = pl.Buffered(1)
    return pl.pallas_call(
        _moba_attn_kernel,
        grid=(H // G, nb),
        in_specs=[
            pl.BlockSpec((G, BS, LANES), lambda h, i: (h, i, 0)),
            pl.BlockSpec((G, nb, BS, LANES), lambda h, i: (h, 0, 0, 0), pipeline_mode=resident),
            pl.BlockSpec((G, nb, LANES, BS), lambda h, i: (h, 0, 0, 0), pipeline_mode=resident),
            pl.BlockSpec((G, 1, nb, BS), lambda h, i: (h, i, 0, 0)),
        ],
        out_specs=pl.BlockSpec((BS, G * LANES), lambda h, i: (i, h)),
        out_shape=jax.ShapeDtypeStruct((T, MOBA_WIDTH), F32),
        scratch_shapes=[
            pltpu.VMEM((G, 1, BS), F32), pltpu.VMEM((G, 1, BS), F32), pltpu.VMEM((G, LANES, BS), F32),
        ],
        compiler_params=pltpu.CompilerParams(
            dimension_semantics=("arbitrary", "arbitrary"), vmem_limit_bytes=VMEM_LIMIT),
        name="moba_attn",
    )(qs, kr, vt, bias)


def _out_kernel(mr_ref, ym_ref, gm_ref, x_ref, wt_ref, wb_ref, fw_ref, o_ref):
    mm = (ym_ref[...] * _silu(gm_ref[...])).astype(BF16)
    h = x_ref[...] + _dot(mr_ref[...].astype(BF16), wt_ref[...]) + _dot(mm, wb_ref[...])
    ms = jnp.mean(h * h, axis=-1, keepdims=True)
    o_ref[...] = h * lax.rsqrt(ms + RMS_EPS) * fw_ref[...]


def _out_proj(mixed_r, y_m, proj_main, x2, w_top, w_bot, fw, tm):
    T = x2.shape[0]
    return pl.pallas_call(
        _out_kernel,
        grid=(T // tm,),
        in_specs=[
            pl.BlockSpec((tm, 1024), lambda i: (i, 0)),
            pl.BlockSpec((tm, 1024), lambda i: (i, 0)),
            pl.BlockSpec((tm, 1024), lambda i: (i, 7)),
            pl.BlockSpec((tm, D_MODEL), lambda i: (i, 0)),
            pl.BlockSpec((1024, D_MODEL), lambda i: (0, 0)),
            pl.BlockSpec((1024, D_MODEL), lambda i: (0, 0)),
            pl.BlockSpec((1, D_MODEL), lambda i: (0, 0)),
        ],
        out_specs=pl.BlockSpec((tm, D_MODEL), lambda i: (i, 0)),
        out_shape=jax.ShapeDtypeStruct((T, D_MODEL), F32),
        compiler_params=pltpu.CompilerParams(
            dimension_semantics=("arbitrary",), vmem_limit_bytes=VMEM_LIMIT),
        name="out_proj",
    )(mixed_r, y_m, proj_main, x2, w_top, w_bot, fw)


def kernel(x, norm_w, w_in, mu_rkv, mu_lora, w_decay_up, w_decay_bias, w_iclr_up, w_iclr_bias,
           k_k, k_a, r_k, lnx_w, lnx_b, w_out, final_norm_w):
    B, T, _ = x.shape
    assert B == 1 and T % MOBA_BLOCK == 0
    x2 = x.reshape(T, D_MODEL)

    w = w_in[0]
    lora0 = 3 * RWKV_WIDTH
    lora1 = lora0 + 2 * LORA_RANK
    w_main = jnp.concatenate([w[:, :lora0], w[:, lora1:]], axis=1).astype(BF16)
    w_lora = w[:, lora0:lora1].astype(BF16)
    tm_proj = min(1024, T)
    proj_main, proj_lora = _project(x2, norm_w[0:1], w_main, w_lora, tm_proj)

    rows = [mu_rkv[0, 0], mu_rkv[0, 1], mu_rkv[0, 2], w_decay_bias[0], w_iclr_bias[0], k_k[0], k_a[0],
            r_k[0].reshape(-1), lnx_w[0], lnx_b[0]]
    vec_tab = jnp.stack(rows + [jnp.zeros_like(rows[0])] * (N_VEC - len(rows)), axis=0)
    mu_l = mu_lora[0].reshape(1, 2 * LORA_RANK)
    zeros_up = jnp.zeros_like(w_decay_up[0])
    wdec = jnp.concatenate([w_decay_up[0], zeros_up], axis=0)
    wicl = jnp.concatenate([zeros_up, w_iclr_up[0]], axis=0)
    mixed_r = _rwkv(proj_main, proj_lora, vec_tab, mu_l, wdec, wicl)

    lane = jnp.arange(LANES)
    inv_freq = ROPE_THETA ** (-(lane % ROT_HALF).astype(F32) / ROT_HALF)
    qs, kr, vt, bias = _moba_prep(proj_main, inv_freq.reshape(1, LANES))
    y_m = _moba_attn(qs, kr, vt, bias)

    wo = w_out[0].astype(BF16)
    tm_out = min(256, T)
    out = _out_proj(mixed_r, y_m, proj_main, x2, wo[:RWKV_WIDTH], wo[RWKV_WIDTH:],
                    final_norm_w.reshape(1, D_MODEL), tm_out)
    return out.reshape(B, T, D_MODEL)
```

```python
import functools

import jax
import jax.numpy as jnp
from jax import lax
from jax.experimental import pallas as pl
from jax.experimental.pallas import tpu as pltpu

F32 = jnp.float32
BF16 = jnp.bfloat16

D_MODEL = 2048
RWKV_WIDTH = 1024
RWKV_HEAD_DIM = 64
LORA_RANK = 64
MOBA_WIDTH = 1024
MOBA_HEAD_DIM = 128
MOBA_HEADS = 8
ROT_HALF = 16
ROPE_THETA = 500000.0
MOBA_BLOCK = 256
MOBA_TOPK = 3
RMS_EPS = 1e-6
LNX_EPS = 64e-5
NEG_INF = -1e30

LANES = 128
N_PAIRS = RWKV_WIDTH // LANES
CHUNK = 64
N_MAIN_BLOCKS = 8
VMEM_LIMIT = 56 * 1024 * 1024
ATTN_HEADS = 4
ATTN_UNROLL = 4
LOG2E = 1.4426950408889634
COL_BLOCK = 1024
V_AUG_ROWS = 144
SHIFT_LANE = 64
SHIFT_HEADROOM = 20.0
FAST_WINDOW = 120.0


def _dot(a, b, precision=None):
    return jnp.dot(a, b, preferred_element_type=F32, precision=precision)


def _dot_nt(a, b, precision=None):
    return lax.dot_general(a, b, (((1,), (1,)), ((), ())), preferred_element_type=F32,
                           precision=precision)


def _silu(g):
    return g / (1.0 + jnp.exp(-g))


def _proj_kernel(x_ref, nw_ref, wm_ref, wl_ref, om_ref, ol_ref, xn_ref):
    n = pl.program_id(1)

    @pl.when(n == 0)
    def _():
        x = x_ref[...]
        ms = jnp.mean(x * x, axis=-1, keepdims=True)
        xn_ref[...] = (x * lax.rsqrt(ms + RMS_EPS) * nw_ref[...]).astype(BF16)

    @pl.when(n < N_MAIN_BLOCKS)
    def _():
        om_ref[...] = _dot(xn_ref[...], wm_ref[...])

    @pl.when(n == N_MAIN_BLOCKS)
    def _():
        ol_ref[...] = _dot(xn_ref[...], wl_ref[...])


def _project(x2, norm_w, w_main, w_lora, tm):
    T = x2.shape[0]
    last = N_MAIN_BLOCKS - 1
    return pl.pallas_call(
        _proj_kernel,
        grid=(T // tm, N_MAIN_BLOCKS + 1),
        in_specs=[
            pl.BlockSpec((tm, D_MODEL), lambda i, n: (i, 0)),
            pl.BlockSpec((1, D_MODEL), lambda i, n: (0, 0)),
            pl.BlockSpec((D_MODEL, 1024), lambda i, n: (0, jnp.minimum(n, last))),
            pl.BlockSpec((D_MODEL, LANES), lambda i, n: (0, 0)),
        ],
        out_specs=[
            pl.BlockSpec((tm, 1024), lambda i, n: (i, jnp.minimum(n, last))),
            pl.BlockSpec((tm, LANES), lambda i, n: (i, 0)),
        ],
        out_shape=[
            jax.ShapeDtypeStruct((T, N_MAIN_BLOCKS * 1024), F32),
            jax.ShapeDtypeStruct((T, LANES), F32),
        ],
        scratch_shapes=[pltpu.VMEM((tm, D_MODEL), BF16)],
        compiler_params=pltpu.CompilerParams(
            dimension_semantics=("arbitrary", "arbitrary"), vmem_limit_bytes=VMEM_LIMIT),
        name="proj",
    )(x2, norm_w, w_main, w_lora)


_V_MU_R, _V_MU_K, _V_MU_V, _V_DEC_B, _V_ICL_B, _V_KK, _V_KA, _V_RK, _V_LNW, _V_LNB = range(10)
N_VEC = 16


def _to_pairs(x):
    return jnp.stack([x[:, LANES * p:LANES * (p + 1)] for p in range(N_PAIRS)], axis=0)


def _shift_rows(z, prev_row):
    rolled = pltpu.roll(z, 1, axis=0)
    row = lax.broadcasted_iota(jnp.int32, z.shape, 0)
    return jnp.where(row == 0, prev_row, rolled)


def _rwkv_kernel(r_ref, k_ref, v_ref, g_ref, lo_ref, vec_ref, mul_ref, wdec_ref, wicl_ref,
                 o_ref,
                 s_ref, prev_ref, prevl_ref, xa_ref, yb_ref, yk_ref, vb_ref, vf_ref, wb_ref,
                 pc_ref, bon_ref, yo_ref):
    C = CHUNK
    c_idx = pl.program_id(0)

    @pl.when(c_idx == 0)
    def _():
        s_ref[...] = jnp.zeros_like(s_ref)
        prev_ref[...] = jnp.zeros_like(prev_ref)
        prevl_ref[...] = jnp.zeros_like(prevl_ref)

    def vec(i):
        return vec_ref[i:i + 1, :]

    def shifted(ref, slot, mu):
        z = ref[...]
        zp = _shift_rows(z, prev_ref[slot:slot + 1, :])
        prev_ref[slot:slot + 1, :] = z[C - 1:C, :]
        return z + mu * (zp - z)

    r = shifted(r_ref, 0, vec(_V_MU_R))
    k = shifted(k_ref, 1, vec(_V_MU_K))
    v = shifted(v_ref, 2, vec(_V_MU_V))
    lo = lo_ref[...]
    lop = _shift_rows(lo, prevl_ref[0:1, :])
    prevl_ref[0:1, :] = lo[C - 1:C, :]
    lo = lo + mul_ref[...] * (lop - lo)

    lane = lax.broadcasted_iota(jnp.int32, (C, LANES), 1)
    lmix = jnp.where(lane < LORA_RANK, jnp.tanh(lo), lo)
    dec_in = _dot(lmix, wdec_ref[...], lax.Precision.HIGHEST)
    icl_in = _dot(lmix, wicl_ref[...], lax.Precision.HIGHEST)
    z = -(vec(_V_DEC_B) + dec_in)
    softplus = jnp.maximum(z, 0.0) + jnp.log(1.0 + jnp.exp(-jnp.abs(z)))
    w_log = -softplus - 0.5
    lw = -jnp.exp(w_log)
    a = 1.0 / (1.0 + jnp.exp(-(vec(_V_ICL_B) + icl_in)))

    ti = lax.broadcasted_iota(jnp.int32, (C, C), 0)
    si = lax.broadcasted_iota(jnp.int32, (C, C), 1)
    tril = jnp.where(si <= ti, 1.0, 0.0).astype(BF16)
    lw_hi = lw.astype(BF16)
    rem = lw - lw_hi.astype(F32)
    lw_mid = rem.astype(BF16)
    lw_lo = (rem - lw_mid.astype(F32)).astype(BF16)
    cl = _dot(tril, lw_hi) + _dot(tril, lw_mid) + _dot(tril, lw_lo)

    gi = lax.broadcasted_iota(jnp.int32, (LANES, LANES), 0)
    gj = lax.broadcasted_iota(jnp.int32, (LANES, LANES), 1)
    same_head = (gi < RWKV_HEAD_DIM) == (gj < RWKV_HEAD_DIM)
    ones_bd = jnp.where(same_head, 1.0, 0.0).astype(BF16)

    def head_sum(x3):
        flat = x3.reshape(N_PAIRS * C, LANES).astype(BF16)
        return _dot(flat, ones_bd).reshape(N_PAIRS, C, LANES)

    kk3 = _to_pairs(k * vec(_V_KK))
    kk3 = kk3 / jnp.maximum(jnp.sqrt(head_sum(kk3 * kk3)), 1e-12)
    k2 = k * (1.0 + (a - 1.0) * vec(_V_KA))

    r3, k3, v3, a3, cl3, lw3 = (_to_pairs(t) for t in (r, k2, v, a, cl, lw))
    p_in = jnp.exp(cl3)
    p_ex = jnp.exp(cl3 - lw3)
    p_inv = jnp.exp(-cl3)
    cl_end = cl3[:, C - 1:C, :]
    p_tail = jnp.exp(cl_end - cl3)
    b3 = kk3 * a3

    xa_ref[:, 0:C, :] = (-kk3 * p_ex).astype(BF16)
    xa_ref[:, C:2 * C, :] = (r3 * p_in).astype(BF16)
    yb_ref[...] = (b3 * p_inv).astype(BF16)
    yk_ref[...] = (k3 * p_inv).astype(BF16)
    vb_ref[...] = v3.astype(BF16)
    vf_ref[...] = v3
    wb_ref[:, 0:C, :] = (b3 * p_tail).astype(BF16)
    wb_ref[:, C:2 * C, :] = (k3 * p_tail).astype(BF16)
    pc_ref[...] = jnp.broadcast_to(jnp.exp(cl_end), pc_ref.shape)
    rk3 = _to_pairs(r * k2 * vec(_V_RK))
    bon_ref[...] = head_sum(rk3) * v3

    lo_half = lane < RWKV_HEAD_DIM
    t_idx = lax.broadcasted_iota(jnp.int32, (C, LANES), 0)
    s_idx = lane & (RWKV_HEAD_DIM - 1)
    strict = s_idx < t_idx
    incl = s_idx <= t_idx

    def bd(m):
        zero = jnp.zeros_like(m)
        return jnp.concatenate([jnp.where(lo_half, m, zero), jnp.where(lo_half, zero, m)], axis=0)

    P = range(N_PAIRS)
    zero = jnp.zeros((C, LANES), F32)
    x = [xa_ref[p] for p in P]
    sc = [_dot_nt(x[p], jnp.concatenate([bd(yb_ref[p]), bd(yk_ref[p])], axis=0)) for p in P]
    xs = [_dot_nt(x[p], s_ref[p].astype(BF16)) for p in P]
    a_ab = [jnp.where(strict, sc[p][0:C, 0:LANES], zero).astype(BF16) for p in P]
    b_rb = [jnp.where(incl, sc[p][C:2 * C, 0:LANES], zero).astype(BF16) for p in P]
    akrk = [jnp.concatenate([jnp.where(strict, sc[p][0:C, LANES:2 * LANES], zero),
                             jnp.where(incl, sc[p][C:2 * C, LANES:2 * LANES], zero)],
                            axis=0).astype(BF16) for p in P]
    av = [_dot(akrk[p], bd(vb_ref[p])) for p in P]
    u = [xs[p][0:C] + av[p][0:C] for p in P]
    ap = a_ab
    n_steps = 6
    for step in range(n_steps):
        u = [u[p] + _dot(ap[p], bd(u[p].astype(BF16))) for p in P]
        if step < n_steps - 1:
            ap = [_dot(ap[p], bd(ap[p])).astype(BF16) for p in P]
    y = [xs[p][C:2 * C] + av[p][C:2 * C] + _dot(b_rb[p], bd(u[p].astype(BF16))) for p in P]
    uvt = [jnp.concatenate([u[p], vf_ref[p]], axis=0).T.astype(BF16) for p in P]
    zmat = [_dot(uvt[p], wb_ref[p]) for p in P]
    for p in P:
        s_ref[p] = s_ref[p] * pc_ref[p][0:1, :] + jnp.where(same_head, zmat[p], jnp.zeros_like(zmat[p]))
    inv_n = 1.0 / RWKV_HEAD_DIM
    d = [y[p] - _dot(y[p].astype(BF16), ones_bd) * inv_n for p in P]
    var = [_dot((d[p] * d[p]).astype(BF16), ones_bd) * inv_n for p in P]
    for p in P:
        yo_ref[p] = d[p] * lax.rsqrt(var[p] + LNX_EPS)

    gate = _silu(g_ref[...])
    lnw = vec(_V_LNW)
    lnb = vec(_V_LNB)
    for p in range(N_PAIRS):
        sl = slice(LANES * p, LANES * (p + 1))
        o_ref[:, sl] = (yo_ref[p] * lnw[:, sl] + lnb[:, sl] + bon_ref[p]) * gate[:, sl]


def _rwkv(proj_main, proj_lora, vec_tab, mu_l, wdec, wicl):
    T = proj_main.shape[0]
    C = CHUNK
    pair_bf = lambda rows: pltpu.VMEM((N_PAIRS, rows, LANES), BF16)
    pair_f32 = lambda rows: pltpu.VMEM((N_PAIRS, rows, LANES), F32)
    return pl.pallas_call(
        _rwkv_kernel,
        grid=(T // C,),
        in_specs=[
            pl.BlockSpec((C, 1024), lambda c: (c, 0)),
            pl.BlockSpec((C, 1024), lambda c: (c, 1)),
            pl.BlockSpec((C, 1024), lambda c: (c, 2)),
            pl.BlockSpec((C, 1024), lambda c: (c, 3)),
            pl.BlockSpec((C, LANES), lambda c: (c, 0)),
            pl.BlockSpec((N_VEC, 1024), lambda c: (0, 0)),
            pl.BlockSpec((1, LANES), lambda c: (0, 0)),
            pl.BlockSpec((LANES, 1024), lambda c: (0, 0)),
            pl.BlockSpec((LANES, 1024), lambda c: (0, 0)),
        ],
        out_specs=pl.BlockSpec((C, 1024), lambda c: (c, 0)),
        out_shape=jax.ShapeDtypeStruct((T, RWKV_WIDTH), F32),
        scratch_shapes=[
            pltpu.VMEM((N_PAIRS, LANES, LANES), F32),
            pltpu.VMEM((8, 1024), F32),
            pltpu.VMEM((8, LANES), F32),
            pair_bf(2 * C), pair_bf(C), pair_bf(C), pair_bf(C), pair_f32(C), pair_bf(2 * C),
            pair_f32(8), pair_f32(C), pair_f32(C),
        ],
        compiler_params=pltpu.CompilerParams(
            dimension_semantics=("arbitrary",), vmem_limit_bytes=VMEM_LIMIT),
        name="rwkv",
    )(proj_main, proj_main, proj_main, proj_main, proj_lora, vec_tab, mu_l, wdec, wicl)


def _moba_prep_kernel(q_ref, k_ref, v_ref, invf_ref, qs_ref, kr_ref, vt_ref, margin_ref, km_ref, kmax_ref):
    b = pl.program_id(0)
    BS = MOBA_BLOCK

    @pl.when(b == 0)
    def _():
        km_ref[...] = jnp.zeros_like(km_ref)
        kmax_ref[...] = jnp.zeros_like(kmax_ref)

    lane = lax.broadcasted_iota(jnp.int32, (BS, LANES), 1)
    pos = (b * BS + lax.broadcasted_iota(jnp.int32, (BS, LANES), 0)).astype(F32)
    ang = pos * invf_ref[...]
    cos = jnp.cos(ang)
    sin = jnp.sin(ang)
    cos_f = jnp.where(lane < 2 * ROT_HALF, cos, 1.0)
    sin_a = jnp.where(lane < ROT_HALF, -sin, 0.0)
    sin_b = jnp.where((lane >= ROT_HALF) & (lane < 2 * ROT_HALF), sin, 0.0)

    def rope(x):
        return (x * cos_f + pltpu.roll(x, LANES - ROT_HALF, axis=1) * sin_a
                + pltpu.roll(x, ROT_HALF, axis=1) * sin_b)

    jidx = lax.broadcasted_iota(jnp.int32, (LANES, BS), 0)
    past = jidx < b
    scale = MOBA_HEAD_DIM ** -0.5
    ones_rows = jnp.where(lax.broadcasted_iota(jnp.int32, (V_AUG_ROWS - LANES, BS), 0) == 0, 1.0, 0.0)
    for h in range(MOBA_HEADS):
        sl = slice(LANES * h, LANES * (h + 1))
        qh = rope(q_ref[:, sl])
        kh = rope(k_ref[:, sl])
        qb = (qh * (scale * LOG2E)).astype(BF16)
        kb = kh.astype(BF16)
        kr_ref[h, 0] = kb
        vt_ref[h, 0] = jnp.concatenate([v_ref[:, sl].T, ones_rows], axis=0).astype(BF16)
        gate = _dot_nt(km_ref[h], qh, lax.Precision.HIGHEST)
        gate = jnp.where(past, gate, NEG_INF)
        sel = jidx == b
        for _ in range(MOBA_TOPK):
            m = jnp.max(gate, axis=0, keepdims=True)
            first = jnp.min(jnp.where(gate == m, jidx, LANES), axis=0, keepdims=True)
            hit = jidx == first
            sel = sel | (hit & past)
            gate = jnp.where(hit, -3e38, gate)
        bias_t = jnp.where(sel, 0.0, NEG_INF).T
        km_ref[h, pl.ds(b, 1), :] = jnp.mean(kh, axis=0, keepdims=True)
        qf = qb.astype(F32)
        kf = kb.astype(F32)
        qn = jnp.sqrt(jnp.sum(qf * qf, axis=1, keepdims=True))
        kn = jnp.sqrt(jnp.max(jnp.sum(kf * kf, axis=1, keepdims=True), axis=0, keepdims=True))
        kmax = jnp.maximum(kmax_ref[h], kn)
        kmax_ref[h] = kmax
        bound = qn * kmax[0:1, 0:1] * 1.01 + 1e-3
        shift = bound - SHIFT_HEADROOM
        own = jnp.sum(qf * kf, axis=1, keepdims=True)
        margin_ref[0, h:h + 1, :] = jnp.broadcast_to(jnp.max(bound - own, axis=0, keepdims=True), (1, LANES))
        sh_hi = shift.astype(BF16).astype(F32)
        sh_lo = shift - sh_hi
        aux = jnp.where(lane == SHIFT_LANE, -sh_hi, jnp.where(lane == SHIFT_LANE + 1, -sh_lo, bias_t))
        qs_ref[h] = jnp.concatenate([qb, aux.astype(BF16)], axis=1)


def _moba_prep(proj_main, invf):
    T = proj_main.shape[0]
    nb = T // MOBA_BLOCK
    assert nb <= SHIFT_LANE, "key-block one-hot lanes must stay below the shift lanes"
    BS = MOBA_BLOCK
    H = MOBA_HEADS
    return pl.pallas_call(
        _moba_prep_kernel,
        grid=(nb,),
        in_specs=[
            pl.BlockSpec((BS, COL_BLOCK), lambda b: (b, 4)),
            pl.BlockSpec((BS, COL_BLOCK), lambda b: (b, 5)),
            pl.BlockSpec((BS, COL_BLOCK), lambda b: (b, 6)),
            pl.BlockSpec((1, LANES), lambda b: (0, 0)),
        ],
        out_specs=[
            pl.BlockSpec((H, BS, 2 * LANES), lambda b: (0, b, 0)),
            pl.BlockSpec((H, 1, BS, LANES), lambda b: (0, b, 0, 0)),
            pl.BlockSpec((H, 1, V_AUG_ROWS, BS), lambda b: (0, b, 0, 0)),
            pl.BlockSpec((1, H, LANES), lambda b: (b, 0, 0)),
        ],
        out_shape=[
            jax.ShapeDtypeStruct((H, T, 2 * LANES), BF16),
            jax.ShapeDtypeStruct((H, nb, BS, LANES), BF16),
            jax.ShapeDtypeStruct((H, nb, V_AUG_ROWS, BS), BF16),
            jax.ShapeDtypeStruct((nb, H, LANES), F32),
        ],
        scratch_shapes=[pltpu.VMEM((H, LANES, LANES), F32), pltpu.VMEM((H, 8, LANES), F32)],
        compiler_params=pltpu.CompilerParams(
            dimension_semantics=("arbitrary",), vmem_limit_bytes=VMEM_LIMIT),
        name="moba_prep",
    )(proj_main, proj_main, proj_main, invf)


def _moba_attn_kernel(fast_ref, q_ref, k_ref, vt_ref, o_ref, m_ref, l_ref, acc_ref):
    hg = pl.program_id(0)
    i = pl.program_id(1)
    BS = MOBA_BLOCK
    G = range(ATTN_HEADS)
    kidx = lax.broadcasted_iota(jnp.int32, (BS, BS), 0)
    qidx = lax.broadcasted_iota(jnp.int32, (BS, BS), 1)
    causal = kidx <= qidx
    lane16 = lax.broadcasted_iota(jnp.int32, (16, LANES), 1)
    shift_lanes = (lane16 == SHIFT_LANE) | (lane16 == SHIFT_LANE + 1)

    def aug_keys(g, j, with_shift):
        hot = (lane16 == j) | shift_lanes if with_shift else lane16 == j
        e = jnp.tile(jnp.where(hot, 1.0, 0.0).astype(BF16), (BS // 16, 1))
        return jnp.concatenate([k_ref[g, j], e], axis=1)

    def write_out(g, acc, denom):
        o_ref[:, LANES * g:LANES * (g + 1)] = (acc / denom).T

    @pl.when(fast_ref[hg, i] != 0)
    def _():
        def tiles(js, mask_causal):
            s = [[_dot_nt(aug_keys(g, j, True), q_ref[g]) for g in G] for j in js]
            if mask_causal:
                s = [[jnp.where(causal, sg, NEG_INF) for sg in sj] for sj in s]
            p = [[jnp.exp2(sg).astype(BF16) for sg in sj] for sj in s]
            pv = [[_dot(vt_ref[g, j], p[u][g]) for g in G] for u, j in enumerate(js)]
            return [functools.reduce(lambda a, b: a + b, [pv[u][g] for u in range(len(js))]) for g in G]

        own = tiles([i], True)
        for g in G:
            acc_ref[g] = own[g]

        def body(js):
            pv = tiles(js, False)
            for g in G:
                acc_ref[g] = acc_ref[g] + pv[g]

        n_main = i // ATTN_UNROLL

        def main_body(t, carry):
            body([t * ATTN_UNROLL + u for u in range(ATTN_UNROLL)])
            return carry

        def rest_body(j, carry):
            body([j])
            return carry

        lax.fori_loop(0, n_main, main_body, 0)
        lax.fori_loop(n_main * ATTN_UNROLL, i, rest_body, 0)
        for g in G:
            acc = acc_ref[g]
            write_out(g, acc[0:LANES], acc[LANES:LANES + 1])

    @pl.when(fast_ref[hg, i] == 0)
    def _():
        for g in G:
            s = jnp.where(causal, _dot_nt(aug_keys(g, i, False), q_ref[g]), NEG_INF)
            m = jnp.max(s, axis=0, keepdims=True)
            p = jnp.exp2(s - m)
            m_ref[g] = m
            l_ref[g] = jnp.sum(p, axis=0, keepdims=True)
            acc_ref[g] = _dot(vt_ref[g, i], p.astype(BF16))

        def body(j, carry):
            s = [_dot_nt(aug_keys(g, j, False), q_ref[g]) for g in G]
            pb, alpha = [], []
            for g in G:
                m = m_ref[g]
                m_new = jnp.maximum(m, jnp.max(s[g], axis=0, keepdims=True))
                a = jnp.exp2(m - m_new)
                p = jnp.exp2(s[g] - m_new)
                m_ref[g] = m_new
                l_ref[g] = a * l_ref[g] + jnp.sum(p, axis=0, keepdims=True)
                pb.append(p.astype(BF16))
                alpha.append(a)
            pv = [_dot(vt_ref[g, j], pb[g]) for g in G]
            for g in G:
                acc_ref[g] = alpha[g] * acc_ref[g] + pv[g]
            return carry

        lax.fori_loop(0, i, body, 0)
        for g in G:
            write_out(g, acc_ref[g][0:LANES], l_ref[g])


def _moba_attn(fast, qs, kr, vt):
    H, T, _ = qs.shape
    nb = T // MOBA_BLOCK
    BS = MOBA_BLOCK
    G = ATTN_HEADS
    resident = pl.Buffered(1)
    grid_spec = pltpu.PrefetchScalarGridSpec(
        num_scalar_prefetch=1,
        grid=(H // G, nb),
        in_specs=[
            pl.BlockSpec((G, BS, 2 * LANES), lambda h, i, f: (h, i, 0)),
            pl.BlockSpec((G, nb, BS, LANES), lambda h, i, f: (h, 0, 0, 0), pipeline_mode=resident),
            pl.BlockSpec((G, nb, V_AUG_ROWS, BS), lambda h, i, f: (h, 0, 0, 0), pipeline_mode=resident),
        ],
        out_specs=pl.BlockSpec((BS, G * LANES), lambda h, i, f: (i, h)),
        scratch_shapes=[
            pltpu.VMEM((G, 1, BS), F32), pltpu.VMEM((G, 1, BS), F32), pltpu.VMEM((G, V_AUG_ROWS, BS), F32),
        ],
    )
    return pl.pallas_call(
        _moba_attn_kernel,
        grid_spec=grid_spec,
        out_shape=jax.ShapeDtypeStruct((T, MOBA_WIDTH), F32),
        compiler_params=pltpu.CompilerParams(
            dimension_semantics=("arbitrary", "arbitrary"), vmem_limit_bytes=VMEM_LIMIT),
        name="moba_attn",
    )(fast, qs, kr, vt)


def _out_kernel(mr_ref, ym_ref, gm_ref, x_ref, wt_ref, wb_ref, fw_ref, o_ref):
    mm = (ym_ref[...] * _silu(gm_ref[...])).astype(BF16)
    h = x_ref[...] + _dot(mr_ref[...].astype(BF16), wt_ref[...]) + _dot(mm, wb_ref[...])
    ms = jnp.mean(h * h, axis=-1, keepdims=True)
    o_ref[...] = h * lax.rsqrt(ms + RMS_EPS) * fw_ref[...]


def _out_proj(mixed_r, y_m, proj_main, x2, w_top, w_bot, fw, tm):
    T = x2.shape[0]
    return pl.pallas_call(
        _out_kernel,
        grid=(T // tm,),
        in_specs=[
            pl.BlockSpec((tm, 1024), lambda i: (i, 0)),
            pl.BlockSpec((tm, 1024), lambda i: (i, 0)),
            pl.BlockSpec((tm, 1024), lambda i: (i, 7)),
            pl.BlockSpec((tm, D_MODEL), lambda i: (i, 0)),
            pl.BlockSpec((1024, D_MODEL), lambda i: (0, 0)),
            pl.BlockSpec((1024, D_MODEL), lambda i: (0, 0)),
            pl.BlockSpec((1, D_MODEL), lambda i: (0, 0)),
        ],
        out_specs=pl.BlockSpec((tm, D_MODEL), lambda i: (i, 0)),
        out_shape=jax.ShapeDtypeStruct((T, D_MODEL), F32),
        compiler_params=pltpu.CompilerParams(
            dimension_semantics=("arbitrary",), vmem_limit_bytes=VMEM_LIMIT),
        name="out_proj",
    )(mixed_r, y_m, proj_main, x2, w_top, w_bot, fw)


def kernel(x, norm_w, w_in, mu_rkv, mu_lora, w_decay_up, w_decay_bias, w_iclr_up, w_iclr_bias,
           k_k, k_a, r_k, lnx_w, lnx_b, w_out, final_norm_w):
    B, T, _ = x.shape
    assert B == 1 and T % MOBA_BLOCK == 0
    x2 = x.reshape(T, D_MODEL)

    w = w_in[0]
    lora0 = 3 * RWKV_WIDTH
    lora1 = lora0 + 2 * LORA_RANK
    w_main = jnp.concatenate([w[:, :lora0], w[:, lora1:]], axis=1).astype(BF16)
    w_lora = w[:, lora0:lora1].astype(BF16)
    tm_proj = min(1024, T)
    proj_main, proj_lora = _project(x2, norm_w[0:1], w_main, w_lora, tm_proj)

    rows = [mu_rkv[0, 0], mu_rkv[0, 1], mu_rkv[0, 2], w_decay_bias[0], w_iclr_bias[0], k_k[0], k_a[0],
            r_k[0].reshape(-1), lnx_w[0], lnx_b[0]]
    vec_tab = jnp.stack(rows + [jnp.zeros_like(rows[0])] * (N_VEC - len(rows)), axis=0)
    mu_l = mu_lora[0].reshape(1, 2 * LORA_RANK)
    zeros_up = jnp.zeros_like(w_decay_up[0])
    wdec = jnp.concatenate([w_decay_up[0], zeros_up], axis=0)
    wicl = jnp.concatenate([zeros_up, w_iclr_up[0]], axis=0)
    mixed_r = _rwkv(proj_main, proj_lora, vec_tab, mu_l, wdec, wicl)

    lane = jnp.arange(LANES)
    inv_freq = ROPE_THETA ** (-(lane % ROT_HALF).astype(F32) / ROT_HALF)
    qs, kr, vt, margin = _moba_prep(proj_main, inv_freq.reshape(1, LANES))
    ok = (margin[:, :, 0] <= FAST_WINDOW).reshape(-1, MOBA_HEADS // ATTN_HEADS, ATTN_HEADS)
    fast = jnp.all(ok, axis=-1).T.astype(jnp.int32)
    y_m = _moba_attn(fast, qs, kr, vt)

    wo = w_out[0].astype(BF16)
    tm_out = min(256, T)
    out = _out_proj(mixed_r, y_m, proj_main, x2, wo[:RWKV_WIDTH], wo[RWKV_WIDTH:],
                    final_norm_w.reshape(1, D_MODEL), tm_out)
    return out.reshape(B, T, D_MODEL)
```

```python
import functools

import jax
import jax.numpy as jnp
from jax import lax
from jax.experimental import pallas as pl
from jax.experimental.pallas import tpu as pltpu

F32 = jnp.float32
BF16 = jnp.bfloat16

D_MODEL = 2048
RWKV_WIDTH = 1024
RWKV_HEAD_DIM = 64
LORA_RANK = 64
MOBA_WIDTH = 1024
MOBA_HEAD_DIM = 128
MOBA_HEADS = 8
ROT_HALF = 16
ROPE_THETA = 500000.0
MOBA_BLOCK = 256
MOBA_TOPK = 3
RMS_EPS = 1e-6
LNX_EPS = 64e-5
NEG_INF = -1e30

LANES = 128
N_PAIRS = RWKV_WIDTH // LANES
CHUNK = 64
RWKV_CHUNKS_PER_STEP = 4
N_MAIN_BLOCKS = 8
COL_BLOCK = 1024
VMEM_LIMIT = 56 * 1024 * 1024
ATTN_HEADS = 4
ATTN_UNROLL = 4
LOG2E = 1.4426950408889634
V_AUG_ROWS = 144
SHIFT_LANE = 64
SHIFT_HEADROOM = 20.0
FAST_WINDOW = 120.0


def _dot(a, b, precision=None):
    return jnp.dot(a, b, preferred_element_type=F32, precision=precision)


def _dot_nt(a, b, precision=None):
    return lax.dot_general(a, b, (((1,), (1,)), ((), ())), preferred_element_type=F32,
                           precision=precision)


def _silu(g):
    return g / (1.0 + jnp.exp(-g))


def _proj_kernel(x_ref, nw_ref, wm_ref, wl_ref, om_ref, ol_ref, xn_ref):
    n = pl.program_id(1)

    @pl.when(n == 0)
    def _():
        x = x_ref[...]
        ms = jnp.mean(x * x, axis=-1, keepdims=True)
        xn_ref[...] = (x * lax.rsqrt(ms + RMS_EPS) * nw_ref[...]).astype(BF16)

    @pl.when(n < N_MAIN_BLOCKS)
    def _():
        om_ref[...] = _dot(xn_ref[...], wm_ref[...])

    @pl.when(n == N_MAIN_BLOCKS)
    def _():
        ol_ref[...] = _dot(xn_ref[...], wl_ref[...])


def _project(x2, norm_w, w_main, w_lora, tm):
    T = x2.shape[0]
    last = N_MAIN_BLOCKS - 1
    return pl.pallas_call(
        _proj_kernel,
        grid=(T // tm, N_MAIN_BLOCKS + 1),
        in_specs=[
            pl.BlockSpec((tm, D_MODEL), lambda i, n: (i, 0)),
            pl.BlockSpec((1, D_MODEL), lambda i, n: (0, 0)),
            pl.BlockSpec((D_MODEL, COL_BLOCK), lambda i, n: (0, jnp.minimum(n, last))),
            pl.BlockSpec((D_MODEL, LANES), lambda i, n: (0, 0)),
        ],
        out_specs=[
            pl.BlockSpec((tm, COL_BLOCK), lambda i, n: (i, jnp.minimum(n, last))),
            pl.BlockSpec((tm, LANES), lambda i, n: (i, 0)),
        ],
        out_shape=[
            jax.ShapeDtypeStruct((T, N_MAIN_BLOCKS * COL_BLOCK), F32),
            jax.ShapeDtypeStruct((T, LANES), F32),
        ],
        scratch_shapes=[pltpu.VMEM((tm, D_MODEL), BF16)],
        compiler_params=pltpu.CompilerParams(
            dimension_semantics=("arbitrary", "arbitrary"), vmem_limit_bytes=VMEM_LIMIT),
        name="proj",
    )(x2, norm_w, w_main, w_lora)


_V_MU_R, _V_MU_K, _V_MU_V, _V_DEC_B, _V_ICL_B, _V_KK, _V_KA, _V_RK, _V_LNW, _V_LNB = range(10)
N_VEC = 16


def _shift_rows(z, prev_row):
    rolled = pltpu.roll(z, 1, axis=0)
    row = lax.broadcasted_iota(jnp.int32, z.shape, 0)
    return jnp.where(row == 0, prev_row, rolled)


def _split3(x):
    hi = x.astype(BF16)
    rem = x - hi.astype(F32)
    mid = rem.astype(BF16)
    return hi, mid, (rem - mid.astype(F32)).astype(BF16)


def _interleave(main, side, main_per_side):
    main_live = side_live = True
    while main_live or side_live:
        for _ in range(main_per_side):
            if main_live and next(main, StopIteration) is StopIteration:
                main_live = False
        if side_live and next(side, StopIteration) is StopIteration:
            side_live = False


def _rwkv_kernel(r_ref, k_ref, v_ref, g_ref, lo_ref, vec_ref, mul_ref, wdh_ref, wdl_ref, wih_ref, wil_ref,
                 o_ref, s_ref, prev_ref, prevl_ref, lw_ref, a_ref, *slots):
    C = CHUNK
    NC = RWKV_CHUNKS_PER_STEP
    TB = NC * C
    n_slot = len(slots) // 2
    slots = (slots[:n_slot], slots[n_slot:])

    @pl.when(pl.program_id(0) == 0)
    def _():
        s_ref[...] = jnp.zeros_like(s_ref)
        prev_ref[...] = jnp.zeros_like(prev_ref)
        prevl_ref[...] = jnp.zeros_like(prevl_ref)

    def vec(i, sl=slice(None)):
        return vec_ref[i:i + 1, sl]

    lo = lo_ref[...]
    lop = _shift_rows(lo, prevl_ref[0:1, :])
    lo = lo + mul_ref[...] * (lop - lo)
    lane_t = lax.broadcasted_iota(jnp.int32, (TB, LANES), 1)
    lmix = jnp.where(lane_t < LORA_RANK, jnp.tanh(lo), lo)
    l_hi = lmix.astype(BF16)
    l_lo = (lmix - l_hi.astype(F32)).astype(BF16)

    def dot3(w_hi_ref, w_lo_ref):
        w_hi = w_hi_ref[...]
        return _dot(l_hi, w_hi) + (_dot(l_lo, w_hi) + _dot(l_hi, w_lo_ref[...]))

    z = -(vec(_V_DEC_B) + dot3(wdh_ref, wdl_ref))
    softplus = jnp.maximum(z, 0.0) + jnp.log(1.0 + jnp.exp(-jnp.abs(z)))
    lw_ref[...] = -jnp.exp(-softplus - 0.5)
    a_ref[...] = 1.0 / (1.0 + jnp.exp(-(vec(_V_ICL_B) + dot3(wih_ref, wil_ref))))

    lane = lax.broadcasted_iota(jnp.int32, (C, LANES), 1)
    ti = lax.broadcasted_iota(jnp.int32, (C, C), 0)
    si = lax.broadcasted_iota(jnp.int32, (C, C), 1)
    tril = jnp.where(si <= ti, 1.0, 0.0).astype(BF16)
    gi = lax.broadcasted_iota(jnp.int32, (LANES, LANES), 0)
    gj = lax.broadcasted_iota(jnp.int32, (LANES, LANES), 1)
    same_head = (gi < RWKV_HEAD_DIM) == (gj < RWKV_HEAD_DIM)
    ones_bd = jnp.where(same_head, 1.0, 0.0).astype(BF16)
    lo_half = lane < RWKV_HEAD_DIM
    t_idx = lax.broadcasted_iota(jnp.int32, (C, LANES), 0)
    s_idx = lane & (RWKV_HEAD_DIM - 1)
    strict = s_idx < t_idx
    incl = s_idx <= t_idx
    zero = jnp.zeros((C, LANES), F32)

    def bd(m):
        zeros = jnp.zeros_like(m)
        return jnp.concatenate([jnp.where(lo_half, m, zeros), jnp.where(lo_half, zeros, m)], axis=0)

    def prepare(c):
        xa_ref, yb_ref, yk_ref, vb_ref, vf_ref, wb_ref, pc_ref, bon_ref = slots[c % 2]
        rows = slice(c * C, (c + 1) * C)
        for p in range(N_PAIRS):
            sl = slice(LANES * p, LANES * (p + 1))

            def shifted(ref, slot, mu):
                zc = ref[rows, sl]
                before = prev_ref[slot:slot + 1, sl] if c == 0 else ref[c * C - 1:c * C, sl]
                return zc + mu * (_shift_rows(zc, before) - zc)

            r = shifted(r_ref, 0, vec(_V_MU_R, sl))
            k = shifted(k_ref, 1, vec(_V_MU_K, sl))
            v = shifted(v_ref, 2, vec(_V_MU_V, sl))
            lw = lw_ref[rows, sl]
            a = a_ref[rows, sl]
            hi, mid, low = _split3(lw)
            cl = _dot(tril, hi) + (_dot(tril, mid) + _dot(tril, low))
            kk = k * vec(_V_KK, sl)
            kk = kk / jnp.maximum(jnp.sqrt(_dot((kk * kk).astype(BF16), ones_bd)), 1e-12)
            k2 = k * (1.0 + (a - 1.0) * vec(_V_KA, sl))
            b = kk * a
            cl_end = cl[C - 1:C, :]
            p_inv = jnp.exp(-cl)
            p_tail = jnp.exp(cl_end - cl)
            xa_ref[p, 0:C, :] = (-kk * jnp.exp(cl - lw)).astype(BF16)
            xa_ref[p, C:2 * C, :] = (r * jnp.exp(cl)).astype(BF16)
            yb_ref[p] = (b * p_inv).astype(BF16)
            yk_ref[p] = (k2 * p_inv).astype(BF16)
            vb_ref[p] = v.astype(BF16)
            vf_ref[p] = v
            wb_ref[p, 0:C, :] = (b * p_tail).astype(BF16)
            wb_ref[p, C:2 * C, :] = (k2 * p_tail).astype(BF16)
            pc_ref[p] = jnp.broadcast_to(jnp.exp(cl_end), (8, LANES))
            bon_ref[p] = _dot((r * k2 * vec(_V_RK, sl)).astype(BF16), ones_bd) * v
            yield

    def recur(c):
        xa_ref, yb_ref, yk_ref, vb_ref, vf_ref, wb_ref, pc_ref, bon_ref = slots[c % 2]
        rows = slice(c * C, (c + 1) * C)
        P = range(N_PAIRS)
        x = [xa_ref[p] for p in P]
        sc = [_dot_nt(x[p], jnp.concatenate([bd(yb_ref[p]), bd(yk_ref[p])], axis=0)) for p in P]
        yield
        xs = [_dot_nt(x[p], s_ref[p].astype(BF16)) for p in P]
        a_ab = [jnp.where(strict, sc[p][0:C, 0:LANES], zero).astype(BF16) for p in P]
        b_rb = [jnp.where(incl, sc[p][C:2 * C, 0:LANES], zero).astype(BF16) for p in P]
        akrk = [jnp.concatenate([jnp.where(strict, sc[p][0:C, LANES:2 * LANES], zero),
                                 jnp.where(incl, sc[p][C:2 * C, LANES:2 * LANES], zero)],
                                axis=0).astype(BF16) for p in P]
        yield
        av = [_dot(akrk[p], bd(vb_ref[p])) for p in P]
        yield
        u = [xs[p][0:C] + av[p][0:C] for p in P]
        ap = a_ab
        n_steps = 6
        for step in range(n_steps):
            u = [u[p] + _dot(ap[p], bd(u[p].astype(BF16))) for p in P]
            yield
            if step < n_steps - 1:
                ap = [_dot(ap[p], bd(ap[p])).astype(BF16) for p in P]
                yield
        y = [xs[p][C:2 * C] + av[p][C:2 * C] + _dot(b_rb[p], bd(u[p].astype(BF16))) for p in P]
        yield
        uvt = [jnp.concatenate([u[p], vf_ref[p]], axis=0).T.astype(BF16) for p in P]
        zmat = [_dot(uvt[p], wb_ref[p]) for p in P]
        for p in P:
            s_ref[p] = s_ref[p] * pc_ref[p][0:1, :] + jnp.where(same_head, zmat[p], jnp.zeros_like(zmat[p]))
        yield
        inv_n = 1.0 / RWKV_HEAD_DIM
        d = [y[p] - _dot(y[p].astype(BF16), ones_bd) * inv_n for p in P]
        yield
        var = [_dot((d[p] * d[p]).astype(BF16), ones_bd) * inv_n for p in P]
        for p in P:
            sl = slice(LANES * p, LANES * (p + 1))
            yn = d[p] * lax.rsqrt(var[p] + LNX_EPS) * vec(_V_LNW, sl) + vec(_V_LNB, sl)
            o_ref[rows, sl] = (yn + bon_ref[p]) * _silu(g_ref[rows, sl])
        yield

    for _ in prepare(0):
        pass
    for c in range(NC):
        side = prepare(c + 1) if c + 1 < NC else iter(())
        _interleave(recur(c), side, main_per_side=2)

    prev_ref[0:1, :] = r_ref[TB - 1:TB, :]
    prev_ref[1:2, :] = k_ref[TB - 1:TB, :]
    prev_ref[2:3, :] = v_ref[TB - 1:TB, :]
    prevl_ref[0:1, :] = lo_ref[TB - 1:TB, :]


def _rwkv(proj_main, proj_lora, vec_tab, mu_l, w_lora_split):
    T = proj_main.shape[0]
    C = CHUNK
    TB = RWKV_CHUNKS_PER_STEP * C
    assert T % TB == 0
    pair_bf = lambda rows: pltpu.VMEM((N_PAIRS, rows, LANES), BF16)
    pair_f32 = lambda rows: pltpu.VMEM((N_PAIRS, rows, LANES), F32)
    slot = [pair_bf(2 * C), pair_bf(C), pair_bf(C), pair_bf(C), pair_f32(C), pair_bf(2 * C),
            pair_f32(8), pair_f32(C)]
    col = lambda j: pl.BlockSpec((TB, COL_BLOCK), lambda c: (c, j))
    whole = lambda shape: pl.BlockSpec(shape, lambda c: (0,) * len(shape))
    return pl.pallas_call(
        _rwkv_kernel,
        grid=(T // TB,),
        in_specs=[col(0), col(1), col(2), col(3),
                  pl.BlockSpec((TB, LANES), lambda c: (c, 0)),
                  whole((N_VEC, RWKV_WIDTH)), whole((1, LANES))]
                 + [whole((LANES, RWKV_WIDTH))] * 4,
        out_specs=pl.BlockSpec((TB, RWKV_WIDTH), lambda c: (c, 0)),
        out_shape=jax.ShapeDtypeStruct((T, RWKV_WIDTH), F32),
        scratch_shapes=[
            pltpu.VMEM((N_PAIRS, LANES, LANES), F32),
            pltpu.VMEM((8, RWKV_WIDTH), F32),
            pltpu.VMEM((8, LANES), F32),
            pltpu.VMEM((TB, RWKV_WIDTH), F32),
            pltpu.VMEM((TB, RWKV_WIDTH), F32),
        ] + slot + slot,
        compiler_params=pltpu.CompilerParams(
            dimension_semantics=("arbitrary",), vmem_limit_bytes=VMEM_LIMIT),
        name="rwkv",
    )(proj_main, proj_main, proj_main, proj_main, proj_lora, vec_tab, mu_l, *w_lora_split)


def _moba_prep_kernel(q_ref, k_ref, v_ref, invf_ref, qs_ref, kr_ref, vt_ref, margin_ref, km_ref, kmax_ref):
    b = pl.program_id(0)
    BS = MOBA_BLOCK

    @pl.when(b == 0)
    def _():
        km_ref[...] = jnp.zeros_like(km_ref)
        kmax_ref[...] = jnp.zeros_like(kmax_ref)

    lane = lax.broadcasted_iota(jnp.int32, (BS, LANES), 1)
    pos = (b * BS + lax.broadcasted_iota(jnp.int32, (BS, LANES), 0)).astype(F32)
    ang = pos * invf_ref[...]
    cos = jnp.cos(ang)
    sin = jnp.sin(ang)
    cos_f = jnp.where(lane < 2 * ROT_HALF, cos, 1.0)
    sin_a = jnp.where(lane < ROT_HALF, -sin, 0.0)
    sin_b = jnp.where((lane >= ROT_HALF) & (lane < 2 * ROT_HALF), sin, 0.0)

    def rope(x):
        return (x * cos_f + pltpu.roll(x, LANES - ROT_HALF, axis=1) * sin_a
                + pltpu.roll(x, ROT_HALF, axis=1) * sin_b)

    jidx = lax.broadcasted_iota(jnp.int32, (LANES, BS), 0)
    past = jidx < b
    scale = MOBA_HEAD_DIM ** -0.5
    ones_rows = jnp.where(lax.broadcasted_iota(jnp.int32, (V_AUG_ROWS - LANES, BS), 0) == 0, 1.0, 0.0)
    for h in range(MOBA_HEADS):
        sl = slice(LANES * h, LANES * (h + 1))
        qh = rope(q_ref[:, sl])
        kh = rope(k_ref[:, sl])
        qb = (qh * (scale * LOG2E)).astype(BF16)
        kb = kh.astype(BF16)
        kr_ref[h, 0] = kb
        vt_ref[h, 0] = jnp.concatenate([v_ref[:, sl].T, ones_rows], axis=0).astype(BF16)
        gate = _dot_nt(km_ref[h], qh, lax.Precision.HIGHEST)
        gate = jnp.where(past, gate, NEG_INF)
        sel = jidx == b
        for _ in range(MOBA_TOPK):
            m = jnp.max(gate, axis=0, keepdims=True)
            first = jnp.min(jnp.where(gate == m, jidx, LANES), axis=0, keepdims=True)
            hit = jidx == first
            sel = sel | (hit & past)
            gate = jnp.where(hit, -3e38, gate)
        bias_t = jnp.where(sel, 0.0, NEG_INF).T
        km_ref[h, pl.ds(b, 1), :] = jnp.mean(kh, axis=0, keepdims=True)
        qf = qb.astype(F32)
        kf = kb.astype(F32)
        qn = jnp.sqrt(jnp.sum(qf * qf, axis=1, keepdims=True))
        kn = jnp.sqrt(jnp.max(jnp.sum(kf * kf, axis=1, keepdims=True), axis=0, keepdims=True))
        kmax = jnp.maximum(kmax_ref[h], kn)
        kmax_ref[h] = kmax
        bound = qn * kmax[0:1, 0:1] * 1.01 + 1e-3
        shift = bound - SHIFT_HEADROOM
        own = jnp.sum(qf * kf, axis=1, keepdims=True)
        margin_ref[0, h:h + 1, :] = jnp.broadcast_to(jnp.max(bound - own, axis=0, keepdims=True), (1, LANES))
        sh_hi = shift.astype(BF16).astype(F32)
        sh_lo = shift - sh_hi
        aux = jnp.where(lane == SHIFT_LANE, -sh_hi, jnp.where(lane == SHIFT_LANE + 1, -sh_lo, bias_t))
        qs_ref[h] = jnp.concatenate([qb, aux.astype(BF16)], axis=1)


def _moba_prep(proj_main, invf):
    T = proj_main.shape[0]
    nb = T // MOBA_BLOCK
    assert nb <= SHIFT_LANE, "key-block one-hot lanes must stay below the shift lanes"
    BS = MOBA_BLOCK
    H = MOBA_HEADS
    return pl.pallas_call(
        _moba_prep_kernel,
        grid=(nb,),
        in_specs=[
            pl.BlockSpec((BS, COL_BLOCK), lambda b: (b, 4)),
            pl.BlockSpec((BS, COL_BLOCK), lambda b: (b, 5)),
            pl.BlockSpec((BS, COL_BLOCK), lambda b: (b, 6)),
            pl.BlockSpec((1, LANES), lambda b: (0, 0)),
        ],
        out_specs=[
            pl.BlockSpec((H, BS, 2 * LANES), lambda b: (0, b, 0)),
            pl.BlockSpec((H, 1, BS, LANES), lambda b: (0, b, 0, 0)),
            pl.BlockSpec((H, 1, V_AUG_ROWS, BS), lambda b: (0, b, 0, 0)),
            pl.BlockSpec((1, H, LANES), lambda b: (b, 0, 0)),
        ],
        out_shape=[
            jax.ShapeDtypeStruct((H, T, 2 * LANES), BF16),
            jax.ShapeDtypeStruct((H, nb, BS, LANES), BF16),
            jax.ShapeDtypeStruct((H, nb, V_AUG_ROWS, BS), BF16),
            jax.ShapeDtypeStruct((nb, H, LANES), F32),
        ],
        scratch_shapes=[pltpu.VMEM((H, LANES, LANES), F32), pltpu.VMEM((H, 8, LANES), F32)],
        compiler_params=pltpu.CompilerParams(
            dimension_semantics=("arbitrary",), vmem_limit_bytes=VMEM_LIMIT),
        name="moba_prep",
    )(proj_main, proj_main, proj_main, invf)


def _moba_attn_kernel(fast_ref, q_ref, k_ref, vt_ref, o_ref, m_ref, l_ref, acc_ref):
    hg = pl.program_id(0)
    i = pl.program_id(1)
    BS = MOBA_BLOCK
    G = range(ATTN_HEADS)
    kidx = lax.broadcasted_iota(jnp.int32, (BS, BS), 0)
    qidx = lax.broadcasted_iota(jnp.int32, (BS, BS), 1)
    causal = kidx <= qidx
    lane16 = lax.broadcasted_iota(jnp.int32, (16, LANES), 1)
    shift_lanes = (lane16 == SHIFT_LANE) | (lane16 == SHIFT_LANE + 1)

    def aug_keys(g, j, with_shift):
        hot = (lane16 == j) | shift_lanes if with_shift else lane16 == j
        e = jnp.tile(jnp.where(hot, 1.0, 0.0).astype(BF16), (BS // 16, 1))
        return jnp.concatenate([k_ref[g, j], e], axis=1)

    def write_out(g, acc, denom):
        o_ref[:, LANES * g:LANES * (g + 1)] = (acc / denom).T

    @pl.when(fast_ref[hg, i] != 0)
    def _():
        def tiles(js, mask_causal):
            s = [[_dot_nt(aug_keys(g, j, True), q_ref[g]) for g in G] for j in js]
            if mask_causal:
                s = [[jnp.where(causal, sg, NEG_INF) for sg in sj] for sj in s]
            p = [[jnp.exp2(sg).astype(BF16) for sg in sj] for sj in s]
            pv = [[_dot(vt_ref[g, j], p[u][g]) for g in G] for u, j in enumerate(js)]
            return [functools.reduce(lambda a, b: a + b, [pv[u][g] for u in range(len(js))]) for g in G]

        own = tiles([i], True)
        for g in G:
            acc_ref[g] = own[g]

        def body(js):
            pv = tiles(js, False)
            for g in G:
                acc_ref[g] = acc_ref[g] + pv[g]

        n_main = i // ATTN_UNROLL

        def main_body(t, carry):
            body([t * ATTN_UNROLL + u for u in range(ATTN_UNROLL)])
            return carry

        def rest_body(j, carry):
            body([j])
            return carry

        lax.fori_loop(0, n_main, main_body, 0)
        lax.fori_loop(n_main * ATTN_UNROLL, i, rest_body, 0)
        for g in G:
            acc = acc_ref[g]
            write_out(g, acc[0:LANES], acc[LANES:LANES + 1])

    @pl.when(fast_ref[hg, i] == 0)
    def _():
        for g in G:
            s = jnp.where(causal, _dot_nt(aug_keys(g, i, False), q_ref[g]), NEG_INF)
            m = jnp.max(s, axis=0, keepdims=True)
            p = jnp.exp2(s - m)
            m_ref[g] = m
            l_ref[g] = jnp.sum(p, axis=0, keepdims=True)
            acc_ref[g] = _dot(vt_ref[g, i], p.astype(BF16))

        def body(j, carry):
            s = [_dot_nt(aug_keys(g, j, False), q_ref[g]) for g in G]
            pb, alpha = [], []
            for g in G:
                m = m_ref[g]
                m_new = jnp.maximum(m, jnp.max(s[g], axis=0, keepdims=True))
                a = jnp.exp2(m - m_new)
                p = jnp.exp2(s[g] - m_new)
                m_ref[g] = m_new
                l_ref[g] = a * l_ref[g] + jnp.sum(p, axis=0, keepdims=True)
                pb.append(p.astype(BF16))
                alpha.append(a)
            pv = [_dot(vt_ref[g, j], pb[g]) for g in G]
            for g in G:
                acc_ref[g] = alpha[g] * acc_ref[g] + pv[g]
            return carry

        lax.fori_loop(0, i, body, 0)
        for g in G:
            write_out(g, acc_ref[g][0:LANES], l_ref[g])


def _moba_attn(fast, qs, kr, vt):
    H, T, _ = qs.shape
    nb = T // MOBA_BLOCK
    BS = MOBA_BLOCK
    G = ATTN_HEADS
    resident = pl.Buffered(1)
    grid_spec = pltpu.PrefetchScalarGridSpec(
        num_scalar_prefetch=1,
        grid=(H // G, nb),
        in_specs=[
            pl.BlockSpec((G, BS, 2 * LANES), lambda h, i, f: (h, i, 0)),
            pl.BlockSpec((G, nb, BS, LANES), lambda h, i, f: (h, 0, 0, 0), pipeline_mode=resident),
            pl.BlockSpec((G, nb, V_AUG_ROWS, BS), lambda h, i, f: (h, 0, 0, 0), pipeline_mode=resident),
        ],
        out_specs=pl.BlockSpec((BS, G * LANES), lambda h, i, f: (i, h)),
        scratch_shapes=[
            pltpu.VMEM((G, 1, BS), F32), pltpu.VMEM((G, 1, BS), F32), pltpu.VMEM((G, V_AUG_ROWS, BS), F32),
        ],
    )
    return pl.pallas_call(
        _moba_attn_kernel,
        grid_spec=grid_spec,
        out_shape=jax.ShapeDtypeStruct((T, MOBA_WIDTH), F32),
        compiler_params=pltpu.CompilerParams(
            dimension_semantics=("arbitrary", "arbitrary"), vmem_limit_bytes=VMEM_LIMIT),
        name="moba_attn",
    )(fast, qs, kr, vt)


def _out_kernel(mr_ref, ym_ref, gm_ref, x_ref, wt_ref, wb_ref, fw_ref, o_ref):
    mm = (ym_ref[...] * _silu(gm_ref[...])).astype(BF16)
    h = x_ref[...] + _dot(mr_ref[...].astype(BF16), wt_ref[...]) + _dot(mm, wb_ref[...])
    ms = jnp.mean(h * h, axis=-1, keepdims=True)
    o_ref[...] = h * lax.rsqrt(ms + RMS_EPS) * fw_ref[...]


def _out_proj(mixed_r, y_m, proj_main, x2, w_top, w_bot, fw, tm):
    T = x2.shape[0]
    return pl.pallas_call(
        _out_kernel,
        grid=(T // tm,),
        in_specs=[
            pl.BlockSpec((tm, COL_BLOCK), lambda i: (i, 0)),
            pl.BlockSpec((tm, COL_BLOCK), lambda i: (i, 0)),
            pl.BlockSpec((tm, COL_BLOCK), lambda i: (i, 7)),
            pl.BlockSpec((tm, D_MODEL), lambda i: (i, 0)),
            pl.BlockSpec((RWKV_WIDTH, D_MODEL), lambda i: (0, 0)),
            pl.BlockSpec((MOBA_WIDTH, D_MODEL), lambda i: (0, 0)),
            pl.BlockSpec((1, D_MODEL), lambda i: (0, 0)),
        ],
        out_specs=pl.BlockSpec((tm, D_MODEL), lambda i: (i, 0)),
        out_shape=jax.ShapeDtypeStruct((T, D_MODEL), F32),
        compiler_params=pltpu.CompilerParams(
            dimension_semantics=("arbitrary",), vmem_limit_bytes=VMEM_LIMIT),
        name="out_proj",
    )(mixed_r, y_m, proj_main, x2, w_top, w_bot, fw)


def kernel(x, norm_w, w_in, mu_rkv, mu_lora, w_decay_up, w_decay_bias, w_iclr_up, w_iclr_bias,
           k_k, k_a, r_k, lnx_w, lnx_b, w_out, final_norm_w):
    B, T, _ = x.shape
    assert B == 1 and T % MOBA_BLOCK == 0
    x2 = x.reshape(T, D_MODEL)

    w = w_in[0]
    lora0 = 3 * RWKV_WIDTH
    lora1 = lora0 + 2 * LORA_RANK
    w_main = jnp.concatenate([w[:, :lora0], w[:, lora1:]], axis=1).astype(BF16)
    w_lora = w[:, lora0:lora1].astype(BF16)
    tm_proj = min(1024, T)
    proj_main, proj_lora = _project(x2, norm_w[0:1], w_main, w_lora, tm_proj)

    rows = [mu_rkv[0, 0], mu_rkv[0, 1], mu_rkv[0, 2], w_decay_bias[0], w_iclr_bias[0], k_k[0], k_a[0],
            r_k[0].reshape(-1), lnx_w[0], lnx_b[0]]
    vec_tab = jnp.stack(rows + [jnp.zeros_like(rows[0])] * (N_VEC - len(rows)), axis=0)
    mu_l = mu_lora[0].reshape(1, 2 * LORA_RANK)
    zeros_up = jnp.zeros_like(w_decay_up[0])
    wdec = jnp.concatenate([w_decay_up[0], zeros_up], axis=0)
    wicl = jnp.concatenate([zeros_up, w_iclr_up[0]], axis=0)
    w_lora_split = []
    for w_up in (wdec, wicl):
        w_hi = w_up.astype(BF16)
        w_lora_split += [w_hi, (w_up - w_hi.astype(F32)).astype(BF16)]
    mixed_r = _rwkv(proj_main, proj_lora, vec_tab, mu_l, w_lora_split)

    lane = jnp.arange(LANES)
    inv_freq = ROPE_THETA ** (-(lane % ROT_HALF).astype(F32) / ROT_HALF)
    qs, kr, vt, margin = _moba_prep(proj_main, inv_freq.reshape(1, LANES))
    ok = (margin[:, :, 0] <= FAST_WINDOW).reshape(-1, MOBA_HEADS // ATTN_HEADS, ATTN_HEADS)
    fast = jnp.all(ok, axis=-1).T.astype(jnp.int32)
    y_m = _moba_attn(fast, qs, kr, vt)

    wo = w_out[0].astype(BF16)
    tm_out = min(256, T)
    out = _out_proj(mixed_r, y_m, proj_main, x2, wo[:RWKV_WIDTH], wo[RWKV_WIDTH:],
                    final_norm_w.reshape(1, D_MODEL), tm_out)
    return out.reshape(B, T, D_MODEL)
```

```python
import functools

import jax
import jax.numpy as jnp
from jax import lax
from jax.experimental import pallas as pl
from jax.experimental.pallas import tpu as pltpu

F32 = jnp.float32
BF16 = jnp.bfloat16

D_MODEL = 2048
RWKV_WIDTH = 1024
RWKV_HEAD_DIM = 64
LORA_RANK = 64
MOBA_WIDTH = 1024
MOBA_HEAD_DIM = 128
MOBA_HEADS = 8
ROT_HALF = 16
ROPE_THETA = 500000.0
MOBA_BLOCK = 256
MOBA_TOPK = 3
RMS_EPS = 1e-6
LNX_EPS = 64e-5
NEG_INF = -1e30

LANES = 128
N_PAIRS = RWKV_WIDTH // LANES
CHUNK = 64
RWKV_CHUNKS_PER_STEP = 4
N_MAIN_BLOCKS = 8
GATE_M_COL = 7
COL_BLOCK = 1024
PROJ_SUB_ROWS = 256
VMEM_LIMIT = 56 * 1024 * 1024
ATTN_HEADS = 4
ATTN_UNROLL = 4
LOG2E = 1.4426950408889634
V_AUG_ROWS = 144
SHIFT_LANE = 64
SHIFT_HEADROOM = 20.0
FAST_WINDOW = 120.0


def _dot(a, b, precision=None):
    return jnp.dot(a, b, preferred_element_type=F32, precision=precision)


def _dot_nt(a, b, precision=None):
    return lax.dot_general(a, b, (((1,), (1,)), ((), ())), preferred_element_type=F32,
                           precision=precision)


def _silu(g):
    return g / (1.0 + jnp.exp(-g))


def _proj_kernel(x_ref, nw_ref, wm_ref, wl_ref, om_ref, ol_ref, xn_ref):
    n = pl.program_id(1)

    @pl.when(n == 0)
    def _():
        for r0 in range(0, x_ref.shape[0], PROJ_SUB_ROWS):
            rows = slice(r0, r0 + PROJ_SUB_ROWS)
            x = x_ref[rows, :]
            ms = jnp.mean(x * x, axis=-1, keepdims=True)
            xn = (x * lax.rsqrt(ms + RMS_EPS) * nw_ref[...]).astype(BF16)
            xn_ref[rows, :] = xn
            ol_ref[rows, :] = _dot(xn, wl_ref[...])

    om_ref[...] = _dot(xn_ref[...], wm_ref[...])


def _project(x2, norm_w, w_main, w_lora, tm):
    T = x2.shape[0]
    assert tm % PROJ_SUB_ROWS == 0
    return pl.pallas_call(
        _proj_kernel,
        grid=(T // tm, N_MAIN_BLOCKS),
        in_specs=[
            pl.BlockSpec((tm, D_MODEL), lambda i, n: (i, 0)),
            pl.BlockSpec((1, D_MODEL), lambda i, n: (0, 0)),
            pl.BlockSpec((D_MODEL, COL_BLOCK), lambda i, n: (0, n)),
            pl.BlockSpec((D_MODEL, LANES), lambda i, n: (0, 0)),
        ],
        out_specs=[
            pl.BlockSpec((tm, COL_BLOCK), lambda i, n: (i, n)),
            pl.BlockSpec((tm, LANES), lambda i, n: (i, 0)),
        ],
        out_shape=[
            jax.ShapeDtypeStruct((T, N_MAIN_BLOCKS * COL_BLOCK), F32),
            jax.ShapeDtypeStruct((T, LANES), F32),
        ],
        scratch_shapes=[pltpu.VMEM((tm, D_MODEL), BF16)],
        compiler_params=pltpu.CompilerParams(
            dimension_semantics=("arbitrary", "arbitrary"), vmem_limit_bytes=VMEM_LIMIT),
        name="proj",
    )(x2, norm_w, w_main, w_lora)


_V_MU_R, _V_MU_K, _V_MU_V, _V_DEC_B, _V_ICL_B, _V_KK, _V_KA, _V_RK, _V_LNW, _V_LNB = range(10)
N_VEC = 16


def _shift_rows(z, prev_row):
    rolled = pltpu.roll(z, 1, axis=0)
    row = lax.broadcasted_iota(jnp.int32, z.shape, 0)
    return jnp.where(row == 0, prev_row, rolled)


def _split3(x):
    hi = x.astype(BF16)
    rem = x - hi.astype(F32)
    mid = rem.astype(BF16)
    return hi, mid, (rem - mid.astype(F32)).astype(BF16)


def _interleave(main, side, main_per_side):
    main_live = side_live = True
    while main_live or side_live:
        for _ in range(main_per_side):
            if main_live and next(main, StopIteration) is StopIteration:
                main_live = False
        if side_live and next(side, StopIteration) is StopIteration:
            side_live = False


def _rwkv_kernel(r_ref, k_ref, v_ref, g_ref, lo_ref, vec_ref, mul_ref, wdh_ref, wdl_ref, wih_ref, wil_ref,
                 o_ref, s_ref, prev_ref, prevl_ref, lw_ref, a_ref, *slots):
    C = CHUNK
    NC = RWKV_CHUNKS_PER_STEP
    TB = NC * C
    n_slot = len(slots) // 2
    slots = (slots[:n_slot], slots[n_slot:])

    @pl.when(pl.program_id(0) == 0)
    def _():
        s_ref[...] = jnp.zeros_like(s_ref)
        prev_ref[...] = jnp.zeros_like(prev_ref)
        prevl_ref[...] = jnp.zeros_like(prevl_ref)

    def vec(i, sl=slice(None)):
        return vec_ref[i:i + 1, sl]

    lo = lo_ref[...]
    lop = _shift_rows(lo, prevl_ref[0:1, :])
    lo = lo + mul_ref[...] * (lop - lo)
    lane_t = lax.broadcasted_iota(jnp.int32, (TB, LANES), 1)
    lmix = jnp.where(lane_t < LORA_RANK, jnp.tanh(lo), lo)
    l_hi = lmix.astype(BF16)
    l_lo = (lmix - l_hi.astype(F32)).astype(BF16)

    def dot3(w_hi_ref, w_lo_ref):
        w_hi = w_hi_ref[...]
        return _dot(l_hi, w_hi) + (_dot(l_lo, w_hi) + _dot(l_hi, w_lo_ref[...]))

    z = -(vec(_V_DEC_B) + dot3(wdh_ref, wdl_ref))
    softplus = jnp.maximum(z, 0.0) + jnp.log(1.0 + jnp.exp(-jnp.abs(z)))
    lw_ref[...] = -jnp.exp(-softplus - 0.5)
    a_ref[...] = 1.0 / (1.0 + jnp.exp(-(vec(_V_ICL_B) + dot3(wih_ref, wil_ref))))

    lane = lax.broadcasted_iota(jnp.int32, (C, LANES), 1)
    ti = lax.broadcasted_iota(jnp.int32, (C, C), 0)
    si = lax.broadcasted_iota(jnp.int32, (C, C), 1)
    tril = jnp.where(si <= ti, 1.0, 0.0).astype(BF16)
    gi = lax.broadcasted_iota(jnp.int32, (LANES, LANES), 0)
    gj = lax.broadcasted_iota(jnp.int32, (LANES, LANES), 1)
    same_head = (gi < RWKV_HEAD_DIM) == (gj < RWKV_HEAD_DIM)
    ones_bd = jnp.where(same_head, 1.0, 0.0).astype(BF16)
    lo_half = lane < RWKV_HEAD_DIM
    t_idx = lax.broadcasted_iota(jnp.int32, (C, LANES), 0)
    s_idx = lane & (RWKV_HEAD_DIM - 1)
    strict = s_idx < t_idx
    incl = s_idx <= t_idx
    zero = jnp.zeros((C, LANES), F32)

    def bd(m):
        zeros = jnp.zeros_like(m)
        return jnp.concatenate([jnp.where(lo_half, m, zeros), jnp.where(lo_half, zeros, m)], axis=0)

    def prepare(c):
        xa_ref, yb_ref, yk_ref, vb_ref, vf_ref, wb_ref, pc_ref, bon_ref = slots[c % 2]
        rows = slice(c * C, (c + 1) * C)
        for p in range(N_PAIRS):
            sl = slice(LANES * p, LANES * (p + 1))

            def shifted(ref, slot, mu):
                zc = ref[rows, sl]
                before = prev_ref[slot:slot + 1, sl] if c == 0 else ref[c * C - 1:c * C, sl]
                return zc + mu * (_shift_rows(zc, before) - zc)

            r = shifted(r_ref, 0, vec(_V_MU_R, sl))
            k = shifted(k_ref, 1, vec(_V_MU_K, sl))
            v = shifted(v_ref, 2, vec(_V_MU_V, sl))
            lw = lw_ref[rows, sl]
            a = a_ref[rows, sl]
            hi, mid, low = _split3(lw)
            cl = _dot(tril, hi) + (_dot(tril, mid) + _dot(tril, low))
            kk = k * vec(_V_KK, sl)
            kk = kk / jnp.maximum(jnp.sqrt(_dot((kk * kk).astype(BF16), ones_bd)), 1e-12)
            k2 = k * (1.0 + (a - 1.0) * vec(_V_KA, sl))
            b = kk * a
            cl_end = cl[C - 1:C, :]
            p_inv = jnp.exp(-cl)
            p_tail = jnp.exp(cl_end - cl)
            xa_ref[p, 0:C, :] = (-kk * jnp.exp(cl - lw)).astype(BF16)
            xa_ref[p, C:2 * C, :] = (r * jnp.exp(cl)).astype(BF16)
            yb_ref[p] = (b * p_inv).astype(BF16)
            yk_ref[p] = (k2 * p_inv).astype(BF16)
            vb_ref[p] = v.astype(BF16)
            vf_ref[p] = v
            wb_ref[p, 0:C, :] = (b * p_tail).astype(BF16)
            wb_ref[p, C:2 * C, :] = (k2 * p_tail).astype(BF16)
            pc_ref[p] = jnp.broadcast_to(jnp.exp(cl_end), (8, LANES))
            bon_ref[p] = _dot((r * k2 * vec(_V_RK, sl)).astype(BF16), ones_bd) * v
            yield

    def recur(c):
        xa_ref, yb_ref, yk_ref, vb_ref, vf_ref, wb_ref, pc_ref, bon_ref = slots[c % 2]
        rows = slice(c * C, (c + 1) * C)
        P = range(N_PAIRS)
        x = [xa_ref[p] for p in P]
        sc = [_dot_nt(x[p], jnp.concatenate([bd(yb_ref[p]), bd(yk_ref[p])], axis=0)) for p in P]
        yield
        xs = [_dot_nt(x[p], s_ref[p].astype(BF16)) for p in P]
        a_ab = [jnp.where(strict, sc[p][0:C, 0:LANES], zero).astype(BF16) for p in P]
        b_rb = [jnp.where(incl, sc[p][C:2 * C, 0:LANES], zero).astype(BF16) for p in P]
        akrk = [jnp.concatenate([jnp.where(strict, sc[p][0:C, LANES:2 * LANES], zero),
                                 jnp.where(incl, sc[p][C:2 * C, LANES:2 * LANES], zero)],
                                axis=0).astype(BF16) for p in P]
        yield
        av = [_dot(akrk[p], bd(vb_ref[p])) for p in P]
        yield
        u = [xs[p][0:C] + av[p][0:C] for p in P]
        ap = a_ab
        n_steps = 6
        for step in range(n_steps):
            u = [u[p] + _dot(ap[p], bd(u[p].astype(BF16))) for p in P]
            yield
            if step < n_steps - 1:
                ap = [_dot(ap[p], bd(ap[p])).astype(BF16) for p in P]
                yield
        y = [xs[p][C:2 * C] + av[p][C:2 * C] + _dot(b_rb[p], bd(u[p].astype(BF16))) for p in P]
        yield
        uvt = [jnp.concatenate([u[p], vf_ref[p]], axis=0).T.astype(BF16) for p in P]
        zmat = [_dot(uvt[p], wb_ref[p]) for p in P]
        for p in P:
            s_ref[p] = s_ref[p] * pc_ref[p][0:1, :] + jnp.where(same_head, zmat[p], jnp.zeros_like(zmat[p]))
        yield
        inv_n = 1.0 / RWKV_HEAD_DIM
        d = [y[p] - _dot(y[p].astype(BF16), ones_bd) * inv_n for p in P]
        yield
        var = [_dot((d[p] * d[p]).astype(BF16), ones_bd) * inv_n for p in P]
        for p in P:
            sl = slice(LANES * p, LANES * (p + 1))
            yn = d[p] * lax.rsqrt(var[p] + LNX_EPS) * vec(_V_LNW, sl) + vec(_V_LNB, sl)
            o_ref[rows, sl] = ((yn + bon_ref[p]) * _silu(g_ref[rows, sl])).astype(BF16)
        yield

    for _ in prepare(0):
        pass
    for c in range(NC):
        side = prepare(c + 1) if c + 1 < NC else iter(())
        _interleave(recur(c), side, main_per_side=2)

    prev_ref[0:1, :] = r_ref[TB - 1:TB, :]
    prev_ref[1:2, :] = k_ref[TB - 1:TB, :]
    prev_ref[2:3, :] = v_ref[TB - 1:TB, :]
    prevl_ref[0:1, :] = lo_ref[TB - 1:TB, :]


def _rwkv(proj_main, proj_lora, vec_tab, mu_l, w_lora_split):
    T = proj_main.shape[0]
    C = CHUNK
    TB = RWKV_CHUNKS_PER_STEP * C
    assert T % TB == 0
    pair_bf = lambda rows: pltpu.VMEM((N_PAIRS, rows, LANES), BF16)
    pair_f32 = lambda rows: pltpu.VMEM((N_PAIRS, rows, LANES), F32)
    slot = [pair_bf(2 * C), pair_bf(C), pair_bf(C), pair_bf(C), pair_f32(C), pair_bf(2 * C),
            pair_f32(8), pair_f32(C)]
    col = lambda j: pl.BlockSpec((TB, COL_BLOCK), lambda c: (c, j))
    whole = lambda shape: pl.BlockSpec(shape, lambda c: (0,) * len(shape))
    return pl.pallas_call(
        _rwkv_kernel,
        grid=(T // TB,),
        in_specs=[col(0), col(1), col(2), col(3),
                  pl.BlockSpec((TB, LANES), lambda c: (c, 0)),
                  whole((N_VEC, RWKV_WIDTH)), whole((1, LANES))]
                 + [whole((LANES, RWKV_WIDTH))] * 4,
        out_specs=pl.BlockSpec((TB, RWKV_WIDTH), lambda c: (c, 0)),
        out_shape=jax.ShapeDtypeStruct((T, RWKV_WIDTH), BF16),
        scratch_shapes=[
            pltpu.VMEM((N_PAIRS, LANES, LANES), F32),
            pltpu.VMEM((8, RWKV_WIDTH), F32),
            pltpu.VMEM((8, LANES), F32),
            pltpu.VMEM((TB, RWKV_WIDTH), F32),
            pltpu.VMEM((TB, RWKV_WIDTH), F32),
        ] + slot + slot,
        compiler_params=pltpu.CompilerParams(
            dimension_semantics=("arbitrary",), vmem_limit_bytes=VMEM_LIMIT),
        name="rwkv",
    )(proj_main, proj_main, proj_main, proj_main, proj_lora, vec_tab, mu_l, *w_lora_split)


def _moba_prep_kernel(q_ref, k_ref, v_ref, invf_ref, qs_ref, kr_ref, vt_ref, margin_ref,
                      km_ref, kmax_ref, cs_ref):
    b = pl.program_id(0)
    BS = MOBA_BLOCK
    NR = SHIFT_LANE
    lane = lax.broadcasted_iota(jnp.int32, (BS, LANES), 1)

    @pl.when(b == 0)
    def _():
        km_ref[...] = jnp.zeros_like(km_ref)
        kmax_ref[...] = jnp.zeros_like(kmax_ref)
        local = lax.broadcasted_iota(jnp.int32, (BS, LANES), 0).astype(F32) * invf_ref[...]
        cs_ref[0] = jnp.cos(local)
        cs_ref[1] = jnp.sin(local)

    base = (b * BS).astype(F32) * invf_ref[...]
    cb, sb = jnp.cos(base), jnp.sin(base)
    cos = cb * cs_ref[0] - sb * cs_ref[1]
    sin = sb * cs_ref[0] + cb * cs_ref[1]
    cos_f = jnp.where(lane < 2 * ROT_HALF, cos, 1.0)
    sin_a = jnp.where(lane < ROT_HALF, -sin, 0.0)
    sin_b = jnp.where((lane >= ROT_HALF) & (lane < 2 * ROT_HALF), sin, 0.0)

    def rope(x):
        return (x * cos_f + pltpu.roll(x, LANES - ROT_HALF, axis=1) * sin_a
                + pltpu.roll(x, ROT_HALF, axis=1) * sin_b)

    jidx = lax.broadcasted_iota(jnp.int32, (NR, BS), 0)
    past = jidx < b
    past_bias = jnp.where(past, 0.0, NEG_INF)
    own_bias = jnp.where(jidx == b, 0.0, NEG_INF)
    fill = jnp.full((BS, LANES - NR), NEG_INF, F32)
    ones_sq = jnp.ones((LANES, LANES), BF16)
    scale = MOBA_HEAD_DIM ** -0.5
    ones_rows = jnp.where(lax.broadcasted_iota(jnp.int32, (V_AUG_ROWS - LANES, BS), 0) == 0, 1.0, 0.0)
    for h in range(MOBA_HEADS):
        sl = slice(LANES * h, LANES * (h + 1))
        qh = rope(q_ref[:, sl])
        kh = rope(k_ref[:, sl])
        qb = (qh * (scale * LOG2E)).astype(BF16)
        kb = kh.astype(BF16)
        kr_ref[h, 0] = kb
        vt_ref[h, 0] = jnp.concatenate([v_ref[:, sl].T, ones_rows], axis=0).astype(BF16)
        gate = _dot_nt(km_ref[h], qh, lax.Precision.HIGHEST)
        gate = jnp.where(past, gate, NEG_INF)
        bias = jnp.full((NR, BS), NEG_INF, F32)
        for _ in range(MOBA_TOPK):
            m = jnp.max(gate, axis=0, keepdims=True)
            first = jnp.min(jnp.where(gate == m, jidx, NR), axis=0, keepdims=True)
            hit = jidx == first
            bias = jnp.where(hit, past_bias, bias)
            gate = jnp.where(hit, -3e38, gate)
        bias_t = jnp.concatenate([jnp.maximum(bias, own_bias).T, fill], axis=1)
        km_ref[h, pl.ds(b, 1), :] = jnp.mean(kh, axis=0, keepdims=True)
        qf = qb.astype(F32)
        kf = kb.astype(F32)
        qn = jnp.sqrt(_dot((qf * qf).astype(BF16), ones_sq))
        kn2 = jnp.max(_dot((kf * kf).astype(BF16), ones_sq), axis=0, keepdims=True)
        kmax = jnp.maximum(kmax_ref[h], jnp.sqrt(kn2))
        kmax_ref[h] = kmax
        bound = qn * kmax[0:1, :] * 1.02 + 1e-3
        shift = bound - SHIFT_HEADROOM
        own = _dot((qf * kf).astype(BF16), ones_sq)
        margin_ref[0, h:h + 1, :] = jnp.max(bound - own, axis=0, keepdims=True)
        sh_hi = shift.astype(BF16).astype(F32)
        sh_lo = shift - sh_hi
        aux = jnp.where(lane == SHIFT_LANE, -sh_hi, jnp.where(lane == SHIFT_LANE + 1, -sh_lo, bias_t))
        qs_ref[h] = jnp.concatenate([qb, aux.astype(BF16)], axis=1)


def _moba_prep(proj_main, invf):
    T = proj_main.shape[0]
    nb = T // MOBA_BLOCK
    assert nb <= SHIFT_LANE, "key-block one-hot lanes must stay below the shift lanes"
    BS = MOBA_BLOCK
    H = MOBA_HEADS
    return pl.pallas_call(
        _moba_prep_kernel,
        grid=(nb,),
        in_specs=[
            pl.BlockSpec((BS, COL_BLOCK), lambda b: (b, 4)),
            pl.BlockSpec((BS, COL_BLOCK), lambda b: (b, 5)),
            pl.BlockSpec((BS, COL_BLOCK), lambda b: (b, 6)),
            pl.BlockSpec((1, LANES), lambda b: (0, 0)),
        ],
        out_specs=[
            pl.BlockSpec((H, BS, 2 * LANES), lambda b: (0, b, 0)),
            pl.BlockSpec((H, 1, BS, LANES), lambda b: (0, b, 0, 0)),
            pl.BlockSpec((H, 1, V_AUG_ROWS, BS), lambda b: (0, b, 0, 0)),
            pl.BlockSpec((1, H, LANES), lambda b: (b, 0, 0)),
        ],
        out_shape=[
            jax.ShapeDtypeStruct((H, T, 2 * LANES), BF16),
            jax.ShapeDtypeStruct((H, nb, BS, LANES), BF16),
            jax.ShapeDtypeStruct((H, nb, V_AUG_ROWS, BS), BF16),
            jax.ShapeDtypeStruct((nb, H, LANES), F32),
        ],
        scratch_shapes=[pltpu.VMEM((H, SHIFT_LANE, LANES), F32), pltpu.VMEM((H, 8, LANES), F32),
                        pltpu.VMEM((2, BS, LANES), F32)],
        compiler_params=pltpu.CompilerParams(
            dimension_semantics=("arbitrary",), vmem_limit_bytes=VMEM_LIMIT),
        name="moba_prep",
    )(proj_main, proj_main, proj_main, invf)


def _moba_attn_kernel(fast_ref, q_ref, k_ref, vt_ref, g_ref, o_ref, m_ref, l_ref, acc_ref):
    hg = pl.program_id(0)
    i = pl.program_id(1)
    BS = MOBA_BLOCK
    G = range(ATTN_HEADS)
    kidx = lax.broadcasted_iota(jnp.int32, (BS, BS), 0)
    qidx = lax.broadcasted_iota(jnp.int32, (BS, BS), 1)
    causal = kidx <= qidx
    lane16 = lax.broadcasted_iota(jnp.int32, (16, LANES), 1)
    shift_lanes = (lane16 == SHIFT_LANE) | (lane16 == SHIFT_LANE + 1)

    def aug_keys(g, j, with_shift):
        hot = (lane16 == j) | shift_lanes if with_shift else lane16 == j
        e = jnp.tile(jnp.where(hot, 1.0, 0.0).astype(BF16), (BS // 16, 1))
        return jnp.concatenate([k_ref[g, j], e], axis=1)

    def write_out(g, acc, denom):
        sl = slice(LANES * g, LANES * (g + 1))
        o_ref[:, sl] = ((acc / denom).T * _silu(g_ref[:, sl])).astype(BF16)

    @pl.when(fast_ref[hg, i] != 0)
    def _():
        def tiles(js, mask_causal):
            s = [[_dot_nt(aug_keys(g, j, True), q_ref[g]) for g in G] for j in js]
            if mask_causal:
                s = [[jnp.where(causal, sg, NEG_INF) for sg in sj] for sj in s]
            p = [[jnp.exp2(sg).astype(BF16) for sg in sj] for sj in s]
            pv = [[_dot(vt_ref[g, j], p[u][g]) for g in G] for u, j in enumerate(js)]
            return [functools.reduce(lambda a, b: a + b, [pv[u][g] for u in range(len(js))]) for g in G]

        own = tiles([i], True)
        for g in G:
            acc_ref[g] = own[g]

        def body(js):
            pv = tiles(js, False)
            for g in G:
                acc_ref[g] = acc_ref[g] + pv[g]

        n_main = i // ATTN_UNROLL

        def main_body(t, carry):
            body([t * ATTN_UNROLL + u for u in range(ATTN_UNROLL)])
            return carry

        def rest_body(j, carry):
            body([j])
            return carry

        lax.fori_loop(0, n_main, main_body, 0)
        lax.fori_loop(n_main * ATTN_UNROLL, i, rest_body, 0)
        for g in G:
            acc = acc_ref[g]
            write_out(g, acc[0:LANES], acc[LANES:LANES + 1])

    @pl.when(fast_ref[hg, i] == 0)
    def _():
        for g in G:
            s = jnp.where(causal, _dot_nt(aug_keys(g, i, False), q_ref[g]), NEG_INF)
            m = jnp.max(s, axis=0, keepdims=True)
            p = jnp.exp2(s - m)
            m_ref[g] = m
            l_ref[g] = jnp.sum(p, axis=0, keepdims=True)
            acc_ref[g] = _dot(vt_ref[g, i], p.astype(BF16))

        def body(j, carry):
            s = [_dot_nt(aug_keys(g, j, False), q_ref[g]) for g in G]
            pb, alpha = [], []
            for g in G:
                m = m_ref[g]
                m_new = jnp.maximum(m, jnp.max(s[g], axis=0, keepdims=True))
                a = jnp.exp2(m - m_new)
                p = jnp.exp2(s[g] - m_new)
                m_ref[g] = m_new
                l_ref[g] = a * l_ref[g] + jnp.sum(p, axis=0, keepdims=True)
                pb.append(p.astype(BF16))
                alpha.append(a)
            pv = [_dot(vt_ref[g, j], pb[g]) for g in G]
            for g in G:
                acc_ref[g] = alpha[g] * acc_ref[g] + pv[g]
            return carry

        lax.fori_loop(0, i, body, 0)
        for g in G:
            write_out(g, acc_ref[g][0:LANES], l_ref[g])


def _moba_attn(fast, qs, kr, vt, proj_main):
    H, T, _ = qs.shape
    nb = T // MOBA_BLOCK
    BS = MOBA_BLOCK
    G = ATTN_HEADS
    resident = pl.Buffered(1)
    grid_spec = pltpu.PrefetchScalarGridSpec(
        num_scalar_prefetch=1,
        grid=(H // G, nb),
        in_specs=[
            pl.BlockSpec((G, BS, 2 * LANES), lambda h, i, f: (h, i, 0)),
            pl.BlockSpec((G, nb, BS, LANES), lambda h, i, f: (h, 0, 0, 0), pipeline_mode=resident),
            pl.BlockSpec((G, nb, V_AUG_ROWS, BS), lambda h, i, f: (h, 0, 0, 0), pipeline_mode=resident),
            pl.BlockSpec((BS, G * LANES), lambda h, i, f: (i, GATE_M_COL * (H // G) + h)),
        ],
        out_specs=pl.BlockSpec((BS, G * LANES), lambda h, i, f: (i, h)),
        scratch_shapes=[
            pltpu.VMEM((G, 1, BS), F32), pltpu.VMEM((G, 1, BS), F32), pltpu.VMEM((G, V_AUG_ROWS, BS), F32),
        ],
    )
    return pl.pallas_call(
        _moba_attn_kernel,
        grid_spec=grid_spec,
        out_shape=jax.ShapeDtypeStruct((T, MOBA_WIDTH), BF16),
        compiler_params=pltpu.CompilerParams(
            dimension_semantics=("arbitrary", "arbitrary"), vmem_limit_bytes=VMEM_LIMIT),
        name="moba_attn",
    )(fast, qs, kr, vt, proj_main)


def _out_kernel(mr_ref, mm_ref, x_ref, wt_ref, wb_ref, fw_ref, o_ref):
    h = x_ref[...] + _dot(mr_ref[...], wt_ref[...]) + _dot(mm_ref[...], wb_ref[...])
    ms = jnp.mean(h * h, axis=-1, keepdims=True)
    o_ref[...] = h * lax.rsqrt(ms + RMS_EPS) * fw_ref[...]


def _out_proj(mixed_r, mixed_m, x2, w_out, fw, tm):
    T = x2.shape[0]
    return pl.pallas_call(
        _out_kernel,
        grid=(T // tm,),
        in_specs=[
            pl.BlockSpec((tm, RWKV_WIDTH), lambda i: (i, 0)),
            pl.BlockSpec((tm, MOBA_WIDTH), lambda i: (i, 0)),
            pl.BlockSpec((tm, D_MODEL), lambda i: (i, 0)),
            pl.BlockSpec((RWKV_WIDTH, D_MODEL), lambda i: (0, 0)),
            pl.BlockSpec((MOBA_WIDTH, D_MODEL), lambda i: (1, 0)),
            pl.BlockSpec((1, D_MODEL), lambda i: (0, 0)),
        ],
        out_specs=pl.BlockSpec((tm, D_MODEL), lambda i: (i, 0)),
        out_shape=jax.ShapeDtypeStruct((T, D_MODEL), F32),
        compiler_params=pltpu.CompilerParams(
            dimension_semantics=("arbitrary",), vmem_limit_bytes=VMEM_LIMIT),
        name="out_proj",
    )(mixed_r, mixed_m, x2, w_out, w_out, fw)


def kernel(x, norm_w, w_in, mu_rkv, mu_lora, w_decay_up, w_decay_bias, w_iclr_up, w_iclr_bias,
           k_k, k_a, r_k, lnx_w, lnx_b, w_out, final_norm_w):
    B, T, _ = x.shape
    assert B == 1 and T % MOBA_BLOCK == 0
    x2 = x.reshape(T, D_MODEL)

    w = w_in[0]
    lora0 = 3 * RWKV_WIDTH
    lora1 = lora0 + 2 * LORA_RANK
    w_main = jnp.concatenate([w[:, :lora0].astype(BF16), w[:, lora1:].astype(BF16)], axis=1)
    w_lora = w[:, lora0:lora1].astype(BF16)
    tm_proj = min(1024, T)
    proj_main, proj_lora = _project(x2, norm_w[0:1], w_main, w_lora, tm_proj)

    rows = [mu_rkv[0, 0], mu_rkv[0, 1], mu_rkv[0, 2], w_decay_bias[0], w_iclr_bias[0], k_k[0], k_a[0],
            r_k[0].reshape(-1), lnx_w[0], lnx_b[0]]
    vec_tab = jnp.stack(rows + [jnp.zeros_like(rows[0])] * (N_VEC - len(rows)), axis=0)
    mu_l = mu_lora[0].reshape(1, 2 * LORA_RANK)
    zeros_up = jnp.zeros_like(w_decay_up[0])
    wdec = jnp.concatenate([w_decay_up[0], zeros_up], axis=0)
    wicl = jnp.concatenate([zeros_up, w_iclr_up[0]], axis=0)
    w_lora_split = []
    for w_up in (wdec, wicl):
        w_hi = w_up.astype(BF16)
        w_lora_split += [w_hi, (w_up - w_hi.astype(F32)).astype(BF16)]
    mixed_r = _rwkv(proj_main, proj_lora, vec_tab, mu_l, w_lora_split)

    lane = jnp.arange(LANES)
    inv_freq = ROPE_THETA ** (-(lane % ROT_HALF).astype(F32) / ROT_HALF)
    qs, kr, vt, margin = _moba_prep(proj_main, inv_freq.reshape(1, LANES))
    ok = (margin[:, :, 0] <= FAST_WINDOW).reshape(-1, MOBA_HEADS // ATTN_HEADS, ATTN_HEADS)
    fast = jnp.all(ok, axis=-1).T.astype(jnp.int32)
    mixed_m = _moba_attn(fast, qs, kr, vt, proj_main)

    tm_out = min(512, T)
    out = _out_proj(mixed_r, mixed_m, x2, w_out[0].astype(BF16), final_norm_w.reshape(1, D_MODEL), tm_out)
    return out.reshape(B, T, D_MODEL)
```

```python
import functools

import jax
import jax.numpy as jnp
from jax import lax
from jax.experimental import pallas as pl
from jax.experimental.pallas import tpu as pltpu

F32 = jnp.float32
BF16 = jnp.bfloat16

D_MODEL = 2048
RWKV_WIDTH = 1024
RWKV_HEAD_DIM = 64
LORA_RANK = 64
MOBA_WIDTH = 1024
MOBA_HEAD_DIM = 128
MOBA_HEADS = 8
ROT_HALF = 16
ROPE_THETA = 500000.0
MOBA_BLOCK = 256
MOBA_TOPK = 3
RMS_EPS = 1e-6
LNX_EPS = 64e-5
NEG_INF = -1e30

LANES = 128
N_PAIRS = RWKV_WIDTH // LANES
CHUNK = 64
RWKV_CHUNKS_PER_STEP = 4
N_MAIN_BLOCKS = 8
GATE_M_COL = 7
COL_BLOCK = 1024
PROJ_SUB_ROWS = 256
VMEM_LIMIT = 56 * 1024 * 1024
ATTN_HEADS = 4
ATTN_UNROLL = 4
LOG2E = 1.4426950408889634
V_AUG_ROWS = 144
SHIFT_LANE = 64
SHIFT_HEADROOM = 20.0
FAST_WINDOW = 120.0


def _dot(a, b, precision=None):
    return jnp.dot(a, b, preferred_element_type=F32, precision=precision)


def _dot_nt(a, b, precision=None):
    return lax.dot_general(a, b, (((1,), (1,)), ((), ())), preferred_element_type=F32,
                           precision=precision)


def _silu(g):
    return g / (1.0 + jnp.exp(-g))


def _wprep_kernel(w_ref, o_ref):
    o_ref[...] = w_ref[0].astype(BF16)


def _weight_relayout(w_in):
    n_groups = N_MAIN_BLOCKS * COL_BLOCK // LANES
    lora_group = 3 * RWKV_WIDTH // LANES
    return pl.pallas_call(
        _wprep_kernel,
        grid=(n_groups,),
        in_specs=[pl.BlockSpec((1, D_MODEL, LANES), lambda j: (0, 0, jnp.where(j >= lora_group, j + 1, j)))],
        out_specs=pl.BlockSpec((D_MODEL, LANES), lambda j: (0, j)),
        out_shape=jax.ShapeDtypeStruct((D_MODEL, n_groups * LANES), BF16),
        compiler_params=pltpu.CompilerParams(dimension_semantics=("arbitrary",)),
        name="wprep",
    )(w_in)


def _proj_kernel(x_ref, nw_ref, wm_ref, wl_ref, om_ref, ol_ref, xn_ref):
    n = pl.program_id(1)

    @pl.when(n == 0)
    def _():
        for r0 in range(0, x_ref.shape[0], PROJ_SUB_ROWS):
            rows = slice(r0, r0 + PROJ_SUB_ROWS)
            x = x_ref[rows, :]
            ms = jnp.mean(x * x, axis=-1, keepdims=True)
            xn = (x * lax.rsqrt(ms + RMS_EPS) * nw_ref[...]).astype(BF16)
            xn_ref[rows, :] = xn
            ol_ref[rows, :] = _dot(xn, wl_ref[...])

    om_ref[...] = _dot(xn_ref[...], wm_ref[...])


def _project(x2, norm_w, w_main, w_lora, tm):
    T = x2.shape[0]
    assert tm % PROJ_SUB_ROWS == 0
    return pl.pallas_call(
        _proj_kernel,
        grid=(T // tm, N_MAIN_BLOCKS),
        in_specs=[
            pl.BlockSpec((tm, D_MODEL), lambda i, n: (i, 0)),
            pl.BlockSpec((1, D_MODEL), lambda i, n: (0, 0)),
            pl.BlockSpec((D_MODEL, COL_BLOCK), lambda i, n: (0, n)),
            pl.BlockSpec((D_MODEL, LANES), lambda i, n: (0, 0)),
        ],
        out_specs=[
            pl.BlockSpec((tm, COL_BLOCK), lambda i, n: (i, n)),
            pl.BlockSpec((tm, LANES), lambda i, n: (i, 0)),
        ],
        out_shape=[
            jax.ShapeDtypeStruct((T, N_MAIN_BLOCKS * COL_BLOCK), F32),
            jax.ShapeDtypeStruct((T, LANES), F32),
        ],
        scratch_shapes=[pltpu.VMEM((tm, D_MODEL), BF16)],
        compiler_params=pltpu.CompilerParams(
            dimension_semantics=("arbitrary", "arbitrary"), vmem_limit_bytes=VMEM_LIMIT),
        name="proj",
    )(x2, norm_w, w_main, w_lora)


_V_MU_R, _V_MU_K, _V_MU_V, _V_DEC_B, _V_ICL_B, _V_KK, _V_KA, _V_RK, _V_LNW, _V_LNB = range(10)
N_VEC = 16


def _shift_rows(z, prev_row):
    rolled = pltpu.roll(z, 1, axis=0)
    row = lax.broadcasted_iota(jnp.int32, z.shape, 0)
    return jnp.where(row == 0, prev_row, rolled)


def _split3(x):
    hi = x.astype(BF16)
    rem = x - hi.astype(F32)
    mid = rem.astype(BF16)
    return hi, mid, (rem - mid.astype(F32)).astype(BF16)


def _interleave(main, side, main_per_side):
    main_live = side_live = True
    while main_live or side_live:
        for _ in range(main_per_side):
            if main_live and next(main, StopIteration) is StopIteration:
                main_live = False
        if side_live and next(side, StopIteration) is StopIteration:
            side_live = False


def _rwkv_kernel(r_ref, k_ref, v_ref, g_ref, lo_ref, vec_ref, mul_ref, wdh_ref, wdl_ref, wih_ref, wil_ref,
                 o_ref, s_ref, prev_ref, prevl_ref, lw_ref, a_ref, tr_ref, tk_ref, t2_ref, ss_ref, sm_ref,
                 *slots):
    C = CHUNK
    NC = RWKV_CHUNKS_PER_STEP
    TB = NC * C
    n_slot = len(slots) // 2
    slots = (slots[:n_slot], slots[n_slot:])

    @pl.when(pl.program_id(0) == 0)
    def _():
        s_ref[...] = jnp.zeros_like(s_ref)
        prev_ref[...] = jnp.zeros_like(prev_ref)
        prevl_ref[...] = jnp.zeros_like(prevl_ref)

    def vec(i, sl=slice(None)):
        return vec_ref[i:i + 1, sl]

    lo = lo_ref[...]
    lop = _shift_rows(lo, prevl_ref[0:1, :])
    lo = lo + mul_ref[...] * (lop - lo)
    lane_t = lax.broadcasted_iota(jnp.int32, (TB, LANES), 1)
    lmix = jnp.where(lane_t < LORA_RANK, jnp.tanh(lo), lo)
    l_hi = lmix.astype(BF16)
    l_lo = (lmix - l_hi.astype(F32)).astype(BF16)

    def dot3(w_hi_ref, w_lo_ref):
        w_hi = w_hi_ref[...]
        return _dot(l_hi, w_hi) + (_dot(l_lo, w_hi) + _dot(l_hi, w_lo_ref[...]))

    z = -(vec(_V_DEC_B) + dot3(wdh_ref, wdl_ref))
    softplus = jnp.maximum(z, 0.0) + jnp.log(1.0 + jnp.exp(-jnp.abs(z)))
    lw_ref[...] = -jnp.exp(-softplus - 0.5)
    a_ref[...] = 1.0 / (1.0 + jnp.exp(-(vec(_V_ICL_B) + dot3(wih_ref, wil_ref))))

    lane = lax.broadcasted_iota(jnp.int32, (C, LANES), 1)
    ti = lax.broadcasted_iota(jnp.int32, (C, C), 0)
    si = lax.broadcasted_iota(jnp.int32, (C, C), 1)
    tril = jnp.where(si <= ti, 1.0, 0.0).astype(BF16)
    gi = lax.broadcasted_iota(jnp.int32, (LANES, LANES), 0)
    gj = lax.broadcasted_iota(jnp.int32, (LANES, LANES), 1)
    same_head = (gi < RWKV_HEAD_DIM) == (gj < RWKV_HEAD_DIM)
    ones_bd = jnp.where(same_head, 1.0, 0.0).astype(BF16)
    lo_half = lane < RWKV_HEAD_DIM
    t_idx = lax.broadcasted_iota(jnp.int32, (C, LANES), 0)
    s_idx = lane & (RWKV_HEAD_DIM - 1)
    strict = s_idx < t_idx
    incl = s_idx <= t_idx
    zero = jnp.zeros((C, LANES), F32)

    def bd(m):
        zeros = jnp.zeros_like(m)
        return jnp.concatenate([jnp.where(lo_half, m, zeros), jnp.where(lo_half, zeros, m)], axis=0)

    def prepare(c):
        xa_ref, yb_ref, yk_ref, vb_ref, vf_ref, wb_ref, pc_ref, bon_ref = slots[c % 2]
        rows = slice(c * C, (c + 1) * C)
        cols = [slice(LANES * p, LANES * (p + 1)) for p in range(N_PAIRS)]
        for p in range(N_PAIRS):
            sl = cols[p]

            def shifted(ref, slot, mu):
                zc = ref[rows, sl]
                before = prev_ref[slot:slot + 1, sl] if c == 0 else ref[c * C - 1:c * C, sl]
                return zc + mu * (_shift_rows(zc, before) - zc)

            r = shifted(r_ref, 0, vec(_V_MU_R, sl))
            k = shifted(k_ref, 1, vec(_V_MU_K, sl))
            v = shifted(v_ref, 2, vec(_V_MU_V, sl))
            kk = k * vec(_V_KK, sl)
            k2 = k * (1.0 + (a_ref[rows, sl] - 1.0) * vec(_V_KA, sl))
            tr_ref[p] = r
            tk_ref[p] = kk
            t2_ref[p] = k2
            vf_ref[p] = v
            vb_ref[p] = v.astype(BF16)
            ss_ref[p] = (kk * kk).astype(BF16)
            ss_ref[N_PAIRS + p] = (r * k2 * vec(_V_RK, sl)).astype(BF16)
            yield
        sums = _dot(ss_ref[...].reshape(2 * N_PAIRS * C, LANES), ones_bd).reshape(2 * N_PAIRS, C, LANES)
        for p in range(2 * N_PAIRS):
            sm_ref[p] = sums[p]
        yield
        for p in range(N_PAIRS):
            sl = cols[p]
            lw = lw_ref[rows, sl]
            hi, mid, low = _split3(lw)
            cl = _dot(tril, hi) + (_dot(tril, mid) + _dot(tril, low))
            r, k2, v = tr_ref[p], t2_ref[p], vf_ref[p]
            kk = tk_ref[p] / jnp.maximum(jnp.sqrt(sm_ref[p]), 1e-12)
            b = kk * a_ref[rows, sl]
            cl_end = cl[C - 1:C, :]
            p_inv = jnp.exp(-cl)
            p_tail = jnp.exp(cl_end - cl)
            xa_ref[p, 0:C, :] = (-kk * jnp.exp(cl - lw)).astype(BF16)
            xa_ref[p, C:2 * C, :] = (r * jnp.exp(cl)).astype(BF16)
            yb_ref[p] = (b * p_inv).astype(BF16)
            yk_ref[p] = (k2 * p_inv).astype(BF16)
            wb_ref[p, 0:C, :] = (b * p_tail).astype(BF16)
            wb_ref[p, C:2 * C, :] = (k2 * p_tail).astype(BF16)
            pc_ref[p] = jnp.broadcast_to(jnp.exp(cl_end), (8, LANES))
            bon_ref[p] = sm_ref[N_PAIRS + p] * v
            yield

    def recur(c):
        xa_ref, yb_ref, yk_ref, vb_ref, vf_ref, wb_ref, pc_ref, bon_ref = slots[c % 2]
        rows = slice(c * C, (c + 1) * C)
        P = range(N_PAIRS)
        x = [xa_ref[p] for p in P]
        sc = [_dot_nt(x[p], jnp.concatenate([bd(yb_ref[p]), bd(yk_ref[p])], axis=0)) for p in P]
        yield
        xs = [_dot_nt(x[p], s_ref[p].astype(BF16)) for p in P]
        a_ab = [jnp.where(strict, sc[p][0:C, 0:LANES], zero).astype(BF16) for p in P]
        b_rb = [jnp.where(incl, sc[p][C:2 * C, 0:LANES], zero).astype(BF16) for p in P]
        akrk = [jnp.concatenate([jnp.where(strict, sc[p][0:C, LANES:2 * LANES], zero),
                                 jnp.where(incl, sc[p][C:2 * C, LANES:2 * LANES], zero)],
                                axis=0).astype(BF16) for p in P]
        yield
        av = [_dot(akrk[p], bd(vb_ref[p])) for p in P]
        yield
        u = [xs[p][0:C] + av[p][0:C] for p in P]
        ap = a_ab
        n_steps = 6
        for step in range(n_steps - 1):
            both = [_dot(ap[p], jnp.concatenate([bd(u[p].astype(BF16)), bd(ap[p])], axis=1)) for p in P]
            u = [u[p] + both[p][:, 0:LANES] for p in P]
            ap = [both[p][:, LANES:2 * LANES].astype(BF16) for p in P]
            yield
        u = [u[p] + _dot(ap[p], bd(u[p].astype(BF16))) for p in P]
        yield
        y = [xs[p][C:2 * C] + av[p][C:2 * C] + _dot(b_rb[p], bd(u[p].astype(BF16))) for p in P]
        yield
        uvt = [jnp.concatenate([u[p], vf_ref[p]], axis=0).T.astype(BF16) for p in P]
        zmat = [_dot(uvt[p], wb_ref[p]) for p in P]
        for p in P:
            s_ref[p] = s_ref[p] * pc_ref[p][0:1, :] + jnp.where(same_head, zmat[p], jnp.zeros_like(zmat[p]))
        yield
        inv_n = 1.0 / RWKV_HEAD_DIM

        def head_mean(ts):
            flat = jnp.concatenate([t.astype(BF16) for t in ts], axis=0)
            m = _dot(flat, ones_bd) * inv_n
            return [m[C * p:C * (p + 1)] for p in P]

        mean = head_mean(y)
        d = [y[p] - mean[p] for p in P]
        yield
        var = head_mean([d[p] * d[p] for p in P])
        for p in P:
            sl = slice(LANES * p, LANES * (p + 1))
            yn = d[p] * lax.rsqrt(var[p] + LNX_EPS) * vec(_V_LNW, sl) + vec(_V_LNB, sl)
            o_ref[rows, sl] = ((yn + bon_ref[p]) * _silu(g_ref[rows, sl])).astype(BF16)
        yield

    for _ in prepare(0):
        pass
    for c in range(NC):
        side = prepare(c + 1) if c + 1 < NC else iter(())
        _interleave(recur(c), side, main_per_side=1)

    prev_ref[0:1, :] = r_ref[TB - 1:TB, :]
    prev_ref[1:2, :] = k_ref[TB - 1:TB, :]
    prev_ref[2:3, :] = v_ref[TB - 1:TB, :]
    prevl_ref[0:1, :] = lo_ref[TB - 1:TB, :]


def _rwkv(proj_main, proj_lora, vec_tab, mu_l, w_lora_split):
    T = proj_main.shape[0]
    C = CHUNK
    TB = RWKV_CHUNKS_PER_STEP * C
    assert T % TB == 0
    pair_bf = lambda rows: pltpu.VMEM((N_PAIRS, rows, LANES), BF16)
    pair_f32 = lambda rows: pltpu.VMEM((N_PAIRS, rows, LANES), F32)
    slot = [pair_bf(2 * C), pair_bf(C), pair_bf(C), pair_bf(C), pair_f32(C), pair_bf(2 * C),
            pair_f32(8), pair_f32(C)]
    col = lambda j: pl.BlockSpec((TB, COL_BLOCK), lambda c: (c, j))
    whole = lambda shape: pl.BlockSpec(shape, lambda c: (0,) * len(shape))
    return pl.pallas_call(
        _rwkv_kernel,
        grid=(T // TB,),
        in_specs=[col(0), col(1), col(2), col(3),
                  pl.BlockSpec((TB, LANES), lambda c: (c, 0)),
                  whole((N_VEC, RWKV_WIDTH)), whole((1, LANES))]
                 + [whole((LANES, RWKV_WIDTH))] * 4,
        out_specs=pl.BlockSpec((TB, RWKV_WIDTH), lambda c: (c, 0)),
        out_shape=jax.ShapeDtypeStruct((T, RWKV_WIDTH), BF16),
        scratch_shapes=[
            pltpu.VMEM((N_PAIRS, LANES, LANES), F32),
            pltpu.VMEM((8, RWKV_WIDTH), F32),
            pltpu.VMEM((8, LANES), F32),
            pltpu.VMEM((TB, RWKV_WIDTH), F32),
            pltpu.VMEM((TB, RWKV_WIDTH), F32),
            pair_f32(C), pair_f32(C), pair_f32(C),
            pltpu.VMEM((2 * N_PAIRS, C, LANES), BF16),
            pltpu.VMEM((2 * N_PAIRS, C, LANES), F32),
        ] + slot + slot,
        compiler_params=pltpu.CompilerParams(
            dimension_semantics=("arbitrary",), vmem_limit_bytes=VMEM_LIMIT),
        name="rwkv",
    )(proj_main, proj_main, proj_main, proj_main, proj_lora, vec_tab, mu_l, *w_lora_split)


def _moba_prep_kernel(q_ref, k_ref, v_ref, invf_ref, qs_ref, kr_ref, vt_ref, margin_ref,
                      km_ref, kmax_ref, cs_ref):
    b = pl.program_id(0)
    BS = MOBA_BLOCK
    NR = SHIFT_LANE
    lane = lax.broadcasted_iota(jnp.int32, (BS, LANES), 1)

    @pl.when(b == 0)
    def _():
        km_ref[...] = jnp.zeros_like(km_ref)
        kmax_ref[...] = jnp.zeros_like(kmax_ref)
        local = lax.broadcasted_iota(jnp.int32, (BS, LANES), 0).astype(F32) * invf_ref[...]
        cs_ref[0] = jnp.cos(local)
        cs_ref[1] = jnp.sin(local)

    base = (b * BS).astype(F32) * invf_ref[...]
    cb, sb = jnp.cos(base), jnp.sin(base)
    cos = cb * cs_ref[0] - sb * cs_ref[1]
    sin = sb * cs_ref[0] + cb * cs_ref[1]
    cos_f = jnp.where(lane < 2 * ROT_HALF, cos, 1.0)
    sin_a = jnp.where(lane < ROT_HALF, -sin, 0.0)
    sin_b = jnp.where((lane >= ROT_HALF) & (lane < 2 * ROT_HALF), sin, 0.0)

    def rope(x):
        return (x * cos_f + pltpu.roll(x, LANES - ROT_HALF, axis=1) * sin_a
                + pltpu.roll(x, ROT_HALF, axis=1) * sin_b)

    jidx = lax.broadcasted_iota(jnp.int32, (NR, BS), 0)
    past = jidx < b
    past_bias = jnp.where(past, 0.0, NEG_INF)
    own_bias = jnp.where(jidx == b, 0.0, NEG_INF)
    fill = jnp.full((BS, LANES - NR), NEG_INF, F32)
    ones_sq = jnp.ones((LANES, LANES), BF16)
    scale = MOBA_HEAD_DIM ** -0.5
    ones_rows = jnp.where(lax.broadcasted_iota(jnp.int32, (V_AUG_ROWS - LANES, BS), 0) == 0, 1.0, 0.0)
    for h in range(MOBA_HEADS):
        sl = slice(LANES * h, LANES * (h + 1))
        qh = rope(q_ref[:, sl])
        kh = rope(k_ref[:, sl])
        qb = (qh * (scale * LOG2E)).astype(BF16)
        kb = kh.astype(BF16)
        kr_ref[h, 0] = kb
        vt_ref[h, 0] = jnp.concatenate([v_ref[:, sl].T, ones_rows], axis=0).astype(BF16)
        gate = _dot_nt(km_ref[h], qh, lax.Precision.HIGHEST)
        gate = jnp.where(past, gate, NEG_INF)
        bias = jnp.full((NR, BS), NEG_INF, F32)
        for _ in range(MOBA_TOPK):
            m = jnp.max(gate, axis=0, keepdims=True)
            first = jnp.min(jnp.where(gate == m, jidx, NR), axis=0, keepdims=True)
            hit = jidx == first
            bias = jnp.where(hit, past_bias, bias)
            gate = jnp.where(hit, -3e38, gate)
        bias_t = jnp.concatenate([jnp.maximum(bias, own_bias).T, fill], axis=1)
        km_ref[h, pl.ds(b, 1), :] = jnp.mean(kh, axis=0, keepdims=True)
        qf = qb.astype(F32)
        kf = kb.astype(F32)
        qn = jnp.sqrt(_dot((qf * qf).astype(BF16), ones_sq))
        kn2 = jnp.max(_dot((kf * kf).astype(BF16), ones_sq), axis=0, keepdims=True)
        kmax = jnp.maximum(kmax_ref[h], jnp.sqrt(kn2))
        kmax_ref[h] = kmax
        bound = qn * kmax[0:1, :] * 1.02 + 1e-3
        shift = bound - SHIFT_HEADROOM
        own = _dot((qf * kf).astype(BF16), ones_sq)
        margin_ref[0, h:h + 1, :] = jnp.max(bound - own, axis=0, keepdims=True)
        sh_hi = shift.astype(BF16).astype(F32)
        sh_lo = shift - sh_hi
        aux = jnp.where(lane == SHIFT_LANE, -sh_hi, jnp.where(lane == SHIFT_LANE + 1, -sh_lo, bias_t))
        qs_ref[h] = jnp.concatenate([qb, aux.astype(BF16)], axis=1)


def _moba_prep(proj_main, invf):
    T = proj_main.shape[0]
    nb = T // MOBA_BLOCK
    assert nb <= SHIFT_LANE, "key-block one-hot lanes must stay below the shift lanes"
    BS = MOBA_BLOCK
    H = MOBA_HEADS
    return pl.pallas_call(
        _moba_prep_kernel,
        grid=(nb,),
        in_specs=[
            pl.BlockSpec((BS, COL_BLOCK), lambda b: (b, 4)),
            pl.BlockSpec((BS, COL_BLOCK), lambda b: (b, 5)),
            pl.BlockSpec((BS, COL_BLOCK), lambda b: (b, 6)),
            pl.BlockSpec((1, LANES), lambda b: (0, 0)),
        ],
        out_specs=[
            pl.BlockSpec((H, BS, 2 * LANES), lambda b: (0, b, 0)),
            pl.BlockSpec((H, 1, BS, LANES), lambda b: (0, b, 0, 0)),
            pl.BlockSpec((H, 1, V_AUG_ROWS, BS), lambda b: (0, b, 0, 0)),
            pl.BlockSpec((1, H, LANES), lambda b: (b, 0, 0)),
        ],
        out_shape=[
            jax.ShapeDtypeStruct((H, T, 2 * LANES), BF16),
            jax.ShapeDtypeStruct((H, nb, BS, LANES), BF16),
            jax.ShapeDtypeStruct((H, nb, V_AUG_ROWS, BS), BF16),
            jax.ShapeDtypeStruct((nb, H, LANES), F32),
        ],
        scratch_shapes=[pltpu.VMEM((H, SHIFT_LANE, LANES), F32), pltpu.VMEM((H, 8, LANES), F32),
                        pltpu.VMEM((2, BS, LANES), F32)],
        compiler_params=pltpu.CompilerParams(
            dimension_semantics=("arbitrary",), vmem_limit_bytes=VMEM_LIMIT),
        name="moba_prep",
    )(proj_main, proj_main, proj_main, invf)


def _moba_attn_kernel(fast_ref, q_ref, k_ref, vt_ref, g_ref, o_ref, m_ref, l_ref, acc_ref):
    hg = pl.program_id(0)
    i = pl.program_id(1)
    BS = MOBA_BLOCK
    G = range(ATTN_HEADS)
    kidx = lax.broadcasted_iota(jnp.int32, (BS, BS), 0)
    qidx = lax.broadcasted_iota(jnp.int32, (BS, BS), 1)
    causal = kidx <= qidx
    lane16 = lax.broadcasted_iota(jnp.int32, (16, LANES), 1)
    shift_lanes = (lane16 == SHIFT_LANE) | (lane16 == SHIFT_LANE + 1)

    def aug_keys(g, j, with_shift):
        hot = (lane16 == j) | shift_lanes if with_shift else lane16 == j
        e = jnp.tile(jnp.where(hot, 1.0, 0.0).astype(BF16), (BS // 16, 1))
        return jnp.concatenate([k_ref[g, j], e], axis=1)

    def write_out(g, acc, denom):
        sl = slice(LANES * g, LANES * (g + 1))
        o_ref[:, sl] = ((acc / denom).T * _silu(g_ref[:, sl])).astype(BF16)

    @pl.when(fast_ref[hg, i] != 0)
    def _():
        def tiles(js, causal_flags):
            s = [[_dot_nt(aug_keys(g, j, True), q_ref[g]) for g in G] for j in js]
            s = [[jnp.where(causal, sg, NEG_INF) if flag else sg for sg in sj]
                 for sj, flag in zip(s, causal_flags)]
            p = [[jnp.exp2(sg).astype(BF16) for sg in sj] for sj in s]
            pv = [[_dot(vt_ref[g, j], p[u][g]) for g in G] for u, j in enumerate(js)]
            return [functools.reduce(lambda a, b: a + b, [pv[u][g] for u in range(len(js))]) for g in G]

        n_main = i // ATTN_UNROLL
        left = i - n_main * ATTN_UNROLL
        for r in range(ATTN_UNROLL):
            @pl.when(left == r)
            def _(r=r):
                first = tiles([i] + [n_main * ATTN_UNROLL + u for u in range(r)], [True] + [False] * r)
                for g in G:
                    acc_ref[g] = first[g]

        def main_body(t, carry):
            pv = tiles([t * ATTN_UNROLL + u for u in range(ATTN_UNROLL)], [False] * ATTN_UNROLL)
            for g in G:
                acc_ref[g] = acc_ref[g] + pv[g]
            return carry

        lax.fori_loop(0, n_main, main_body, 0)
        for g in G:
            acc = acc_ref[g]
            write_out(g, acc[0:LANES], acc[LANES:LANES + 1])

    @pl.when(fast_ref[hg, i] == 0)
    def _():
        for g in G:
            s = jnp.where(causal, _dot_nt(aug_keys(g, i, False), q_ref[g]), NEG_INF)
            m = jnp.max(s, axis=0, keepdims=True)
            p = jnp.exp2(s - m)
            m_ref[g] = m
            l_ref[g] = jnp.sum(p, axis=0, keepdims=True)
            acc_ref[g] = _dot(vt_ref[g, i], p.astype(BF16))

        def body(j, carry):
            s = [_dot_nt(aug_keys(g, j, False), q_ref[g]) for g in G]
            pb, alpha = [], []
            for g in G:
                m = m_ref[g]
                m_new = jnp.maximum(m, jnp.max(s[g], axis=0, keepdims=True))
                a = jnp.exp2(m - m_new)
                p = jnp.exp2(s[g] - m_new)
                m_ref[g] = m_new
                l_ref[g] = a * l_ref[g] + jnp.sum(p, axis=0, keepdims=True)
                pb.append(p.astype(BF16))
                alpha.append(a)
            pv = [_dot(vt_ref[g, j], pb[g]) for g in G]
            for g in G:
                acc_ref[g] = alpha[g] * acc_ref[g] + pv[g]
            return carry

        lax.fori_loop(0, i, body, 0)
        for g in G:
            write_out(g, acc_ref[g][0:LANES], l_ref[g])


def _moba_attn(fast, qs, kr, vt, proj_main):
    H, T, _ = qs.shape
    nb = T // MOBA_BLOCK
    BS = MOBA_BLOCK
    G = ATTN_HEADS
    resident = pl.Buffered(1)
    grid_spec = pltpu.PrefetchScalarGridSpec(
        num_scalar_prefetch=1,
        grid=(H // G, nb),
        in_specs=[
            pl.BlockSpec((G, BS, 2 * LANES), lambda h, i, f: (h, i, 0)),
            pl.BlockSpec((G, nb, BS, LANES), lambda h, i, f: (h, 0, 0, 0), pipeline_mode=resident),
            pl.BlockSpec((G, nb, V_AUG_ROWS, BS), lambda h, i, f: (h, 0, 0, 0), pipeline_mode=resident),
            pl.BlockSpec((BS, G * LANES), lambda h, i, f: (i, GATE_M_COL * (H // G) + h)),
        ],
        out_specs=pl.BlockSpec((BS, G * LANES), lambda h, i, f: (i, h)),
        scratch_shapes=[
            pltpu.VMEM((G, 1, BS), F32), pltpu.VMEM((G, 1, BS), F32), pltpu.VMEM((G, V_AUG_ROWS, BS), F32),
        ],
    )
    return pl.pallas_call(
        _moba_attn_kernel,
        grid_spec=grid_spec,
        out_shape=jax.ShapeDtypeStruct((T, MOBA_WIDTH), BF16),
        compiler_params=pltpu.CompilerParams(
            dimension_semantics=("arbitrary", "arbitrary"), vmem_limit_bytes=VMEM_LIMIT),
        name="moba_attn",
    )(fast, qs, kr, vt, proj_main)


def _out_kernel(mr_ref, mm_ref, x_ref, wt_ref, wb_ref, fw_ref, o_ref):
    h = x_ref[...] + _dot(mr_ref[...], wt_ref[...]) + _dot(mm_ref[...], wb_ref[...])
    ms = jnp.mean(h * h, axis=-1, keepdims=True)
    o_ref[...] = h * lax.rsqrt(ms + RMS_EPS) * fw_ref[...]


def _out_proj(mixed_r, mixed_m, x2, w_out, fw, tm):
    T = x2.shape[0]
    return pl.pallas_call(
        _out_kernel,
        grid=(T // tm,),
        in_specs=[
            pl.BlockSpec((tm, RWKV_WIDTH), lambda i: (i, 0)),
            pl.BlockSpec((tm, MOBA_WIDTH), lambda i: (i, 0)),
            pl.BlockSpec((tm, D_MODEL), lambda i: (i, 0)),
            pl.BlockSpec((RWKV_WIDTH, D_MODEL), lambda i: (0, 0)),
            pl.BlockSpec((MOBA_WIDTH, D_MODEL), lambda i: (1, 0)),
            pl.BlockSpec((1, D_MODEL), lambda i: (0, 0)),
        ],
        out_specs=pl.BlockSpec((tm, D_MODEL), lambda i: (i, 0)),
        out_shape=jax.ShapeDtypeStruct((T, D_MODEL), F32),
        compiler_params=pltpu.CompilerParams(
            dimension_semantics=("arbitrary",), vmem_limit_bytes=VMEM_LIMIT),
        name="out_proj",
    )(mixed_r, mixed_m, x2, w_out, w_out, fw)


def kernel(x, norm_w, w_in, mu_rkv, mu_lora, w_decay_up, w_decay_bias, w_iclr_up, w_iclr_bias,
           k_k, k_a, r_k, lnx_w, lnx_b, w_out, final_norm_w):
    B, T, _ = x.shape
    assert B == 1 and T % MOBA_BLOCK == 0
    x2 = x.reshape(T, D_MODEL)

    w = w_in[0]
    lora0 = 3 * RWKV_WIDTH
    lora1 = lora0 + 2 * LORA_RANK
    w_main = _weight_relayout(w_in)
    w_lora = w[:, lora0:lora1].astype(BF16)
    tm_proj = min(1024, T)
    proj_main, proj_lora = _project(x2, norm_w[0:1], w_main, w_lora, tm_proj)

    rows = [mu_rkv[0, 0], mu_rkv[0, 1], mu_rkv[0, 2], w_decay_bias[0], w_iclr_bias[0], k_k[0], k_a[0],
            r_k[0].reshape(-1), lnx_w[0], lnx_b[0]]
    vec_tab = jnp.stack(rows + [jnp.zeros_like(rows[0])] * (N_VEC - len(rows)), axis=0)
    mu_l = mu_lora[0].reshape(1, 2 * LORA_RANK)
    zeros_up = jnp.zeros_like(w_decay_up[0])
    wdec = jnp.concatenate([w_decay_up[0], zeros_up], axis=0)
    wicl = jnp.concatenate([zeros_up, w_iclr_up[0]], axis=0)
    w_lora_split = []
    for w_up in (wdec, wicl):
        w_hi = w_up.astype(BF16)
        w_lora_split += [w_hi, (w_up - w_hi.astype(F32)).astype(BF16)]
    mixed_r = _rwkv(proj_main, proj_lora, vec_tab, mu_l, w_lora_split)

    lane = jnp.arange(LANES)
    inv_freq = ROPE_THETA ** (-(lane % ROT_HALF).astype(F32) / ROT_HALF)
    qs, kr, vt, margin = _moba_prep(proj_main, inv_freq.reshape(1, LANES))
    ok = (margin[:, :, 0] <= FAST_WINDOW).reshape(-1, MOBA_HEADS // ATTN_HEADS, ATTN_HEADS)
    fast = jnp.all(ok, axis=-1).T.astype(jnp.int32)
    mixed_m = _moba_attn(fast, qs, kr, vt, proj_main)

    tm_out = min(512, T)
    out = _out_proj(mixed_r, mixed_m, x2, w_out[0].astype(BF16), final_norm_w.reshape(1, D_MODEL), tm_out)
    return out.reshape(B, T, D_MODEL)
```

```python
import functools

import jax
import jax.numpy as jnp
from jax import lax
from jax.experimental import pallas as pl
from jax.experimental.pallas import tpu as pltpu

F32 = jnp.float32
BF16 = jnp.bfloat16

D_MODEL = 2048
RWKV_WIDTH = 1024
RWKV_HEAD_DIM = 64
LORA_RANK = 64
MOBA_WIDTH = 1024
MOBA_HEAD_DIM = 128
MOBA_HEADS = 8
ROT_HALF = 16
ROPE_THETA = 500000.0
MOBA_BLOCK = 256
MOBA_TOPK = 3
RMS_EPS = 1e-6
LNX_EPS = 64e-5
NEG_INF = -1e30

LANES = 128
N_PAIRS = RWKV_WIDTH // LANES
CHUNK = 64
RWKV_CHUNKS_PER_STEP = 4
N_MAIN_BLOCKS = 8
GATE_M_COL = 7
COL_BLOCK = 1024
PROJ_SUB_ROWS = 256
WPREP_ROWS = 256
VMEM_LIMIT = 56 * 1024 * 1024
ATTN_HEADS = 4
ATTN_UNROLL = 4
LOG2E = 1.4426950408889634
V_AUG_ROWS = 144
SHIFT_HEADROOM = 20.0
FAST_WINDOW = 120.0


def _dot(a, b, precision=None):
    return jnp.dot(a, b, preferred_element_type=F32, precision=precision)


def _dot_nt(a, b, precision=None):
    return lax.dot_general(a, b, (((1,), (1,)), ((), ())), preferred_element_type=F32,
                           precision=precision)


def _silu(g):
    return g / (1.0 + jnp.exp(-g))


def _wprep_kernel(w_ref, om_ref, ol_ref):
    lora0 = 3 * RWKV_WIDTH
    lora1 = lora0 + 2 * LORA_RANK
    om_ref[:, 0:lora0] = w_ref[0, :, 0:lora0].astype(BF16)
    om_ref[:, lora0:] = w_ref[0, :, lora1:].astype(BF16)
    ol_ref[...] = w_ref[0, :, lora0:lora1].astype(BF16)


def _weight_relayout(w_in):
    d_in = w_in.shape[2]
    n_main = N_MAIN_BLOCKS * COL_BLOCK
    assert d_in == n_main + 2 * LORA_RANK
    return pl.pallas_call(
        _wprep_kernel,
        grid=(D_MODEL // WPREP_ROWS,),
        in_specs=[pl.BlockSpec((1, WPREP_ROWS, d_in), lambda i: (0, i, 0))],
        out_specs=[pl.BlockSpec((WPREP_ROWS, n_main), lambda i: (i, 0)),
                   pl.BlockSpec((WPREP_ROWS, 2 * LORA_RANK), lambda i: (i, 0))],
        out_shape=[jax.ShapeDtypeStruct((D_MODEL, n_main), BF16),
                   jax.ShapeDtypeStruct((D_MODEL, 2 * LORA_RANK), BF16)],
        compiler_params=pltpu.CompilerParams(
            dimension_semantics=("arbitrary",), vmem_limit_bytes=VMEM_LIMIT),
        name="wprep",
    )(w_in)


def _proj_kernel(x_ref, nw_ref, wm_ref, wl_ref, om_ref, ol_ref, xn_ref):
    n = pl.program_id(1)

    @pl.when(n == 0)
    def _():
        for r0 in range(0, x_ref.shape[0], PROJ_SUB_ROWS):
            rows = slice(r0, r0 + PROJ_SUB_ROWS)
            x = x_ref[rows, :]
            ms = jnp.mean(x * x, axis=-1, keepdims=True)
            xn = (x * lax.rsqrt(ms + RMS_EPS) * nw_ref[...]).astype(BF16)
            xn_ref[rows, :] = xn
            ol_ref[rows, :] = _dot(xn, wl_ref[...])

    om_ref[...] = _dot(xn_ref[...], wm_ref[...])


def _project(x2, norm_w, w_main, w_lora, tm):
    T = x2.shape[0]
    assert tm % PROJ_SUB_ROWS == 0
    return pl.pallas_call(
        _proj_kernel,
        grid=(T // tm, N_MAIN_BLOCKS),
        in_specs=[
            pl.BlockSpec((tm, D_MODEL), lambda i, n: (i, 0)),
            pl.BlockSpec((1, D_MODEL), lambda i, n: (0, 0)),
            pl.BlockSpec((D_MODEL, COL_BLOCK), lambda i, n: (0, n)),
            pl.BlockSpec((D_MODEL, LANES), lambda i, n: (0, 0)),
        ],
        out_specs=[
            pl.BlockSpec((tm, COL_BLOCK), lambda i, n: (i, n)),
            pl.BlockSpec((tm, LANES), lambda i, n: (i, 0)),
        ],
        out_shape=[
            jax.ShapeDtypeStruct((T, N_MAIN_BLOCKS * COL_BLOCK), F32),
            jax.ShapeDtypeStruct((T, LANES), F32),
        ],
        scratch_shapes=[pltpu.VMEM((tm, D_MODEL), BF16)],
        compiler_params=pltpu.CompilerParams(
            dimension_semantics=("arbitrary", "arbitrary"), vmem_limit_bytes=VMEM_LIMIT),
        name="proj",
    )(x2, norm_w, w_main, w_lora)


_V_MU_R, _V_MU_K, _V_MU_V, _V_DEC_B, _V_ICL_B, _V_KK, _V_KA, _V_RK, _V_LNW, _V_LNB = range(10)
N_VEC = 16


def _shift_rows(z, prev_row):
    rolled = pltpu.roll(z, 1, axis=0)
    row = lax.broadcasted_iota(jnp.int32, z.shape, 0)
    return jnp.where(row == 0, prev_row, rolled)


def _split3(x):
    hi = x.astype(BF16)
    rem = x - hi.astype(F32)
    mid = rem.astype(BF16)
    return hi, mid, (rem - mid.astype(F32)).astype(BF16)


def _interleave(main, side, main_per_side):
    main_live = side_live = True
    while main_live or side_live:
        for _ in range(main_per_side):
            if main_live and next(main, StopIteration) is StopIteration:
                main_live = False
        if side_live and next(side, StopIteration) is StopIteration:
            side_live = False


def _rwkv_kernel(r_ref, k_ref, v_ref, g_ref, lo_ref, vec_ref, mul_ref, wdh_ref, wdl_ref, wih_ref, wil_ref,
                 o_ref, s_ref, prev_ref, prevl_ref, lw_ref, a_ref, tr_ref, tk_ref, t2_ref, ss_ref, sm_ref,
                 *slots):
    C = CHUNK
    NC = RWKV_CHUNKS_PER_STEP
    TB = NC * C
    n_slot = len(slots) // 2
    slots = (slots[:n_slot], slots[n_slot:])

    @pl.when(pl.program_id(0) == 0)
    def _():
        s_ref[...] = jnp.zeros_like(s_ref)
        prev_ref[...] = jnp.zeros_like(prev_ref)
        prevl_ref[...] = jnp.zeros_like(prevl_ref)

    def vec(i, sl=slice(None)):
        return vec_ref[i:i + 1, sl]

    lo = lo_ref[...]
    lop = _shift_rows(lo, prevl_ref[0:1, :])
    lo = lo + mul_ref[...] * (lop - lo)
    lane_t = lax.broadcasted_iota(jnp.int32, (TB, LANES), 1)
    lmix = jnp.where(lane_t < LORA_RANK, jnp.tanh(lo), lo)
    l_hi = lmix.astype(BF16)
    l_lo = (lmix - l_hi.astype(F32)).astype(BF16)

    def dot3(w_hi_ref, w_lo_ref):
        w_hi = w_hi_ref[...]
        return _dot(l_hi, w_hi) + (_dot(l_lo, w_hi) + _dot(l_hi, w_lo_ref[...]))

    z = -(vec(_V_DEC_B) + dot3(wdh_ref, wdl_ref))
    softplus = jnp.maximum(z, 0.0) + jnp.log(1.0 + jnp.exp(-jnp.abs(z)))
    lw_ref[...] = -jnp.exp(-softplus - 0.5)
    a_ref[...] = 1.0 / (1.0 + jnp.exp(-(vec(_V_ICL_B) + dot3(wih_ref, wil_ref))))

    lane = lax.broadcasted_iota(jnp.int32, (C, LANES), 1)
    ti = lax.broadcasted_iota(jnp.int32, (C, C), 0)
    si = lax.broadcasted_iota(jnp.int32, (C, C), 1)
    tril = jnp.where(si <= ti, 1.0, 0.0).astype(BF16)
    gi = lax.broadcasted_iota(jnp.int32, (LANES, LANES), 0)
    gj = lax.broadcasted_iota(jnp.int32, (LANES, LANES), 1)
    same_head = (gi < RWKV_HEAD_DIM) == (gj < RWKV_HEAD_DIM)
    ones_bd = jnp.where(same_head, 1.0, 0.0).astype(BF16)
    lo_half = lane < RWKV_HEAD_DIM
    t_idx = lax.broadcasted_iota(jnp.int32, (C, LANES), 0)
    s_idx = lane & (RWKV_HEAD_DIM - 1)
    strict = s_idx < t_idx
    incl = s_idx <= t_idx
    zero = jnp.zeros((C, LANES), F32)

    def bd(m):
        zeros = jnp.zeros_like(m)
        return jnp.concatenate([jnp.where(lo_half, m, zeros), jnp.where(lo_half, zeros, m)], axis=0)

    def prepare(c):
        xa_ref, yb_ref, yk_ref, vb_ref, vf_ref, wb_ref, pc_ref, bon_ref = slots[c % 2]
        rows = slice(c * C, (c + 1) * C)
        cols = [slice(LANES * p, LANES * (p + 1)) for p in range(N_PAIRS)]
        for p in range(N_PAIRS):
            sl = cols[p]

            def shifted(ref, slot, mu):
                zc = ref[rows, sl]
                before = prev_ref[slot:slot + 1, sl] if c == 0 else ref[c * C - 1:c * C, sl]
                return zc + mu * (_shift_rows(zc, before) - zc)

            r = shifted(r_ref, 0, vec(_V_MU_R, sl))
            k = shifted(k_ref, 1, vec(_V_MU_K, sl))
            v = shifted(v_ref, 2, vec(_V_MU_V, sl))
            kk = k * vec(_V_KK, sl)
            k2 = k * (1.0 + (a_ref[rows, sl] - 1.0) * vec(_V_KA, sl))
            tr_ref[p] = r
            tk_ref[p] = kk
            t2_ref[p] = k2
            vf_ref[p] = v
            vb_ref[p] = v.astype(BF16)
            ss_ref[p] = (kk * kk).astype(BF16)
            ss_ref[N_PAIRS + p] = (r * k2 * vec(_V_RK, sl)).astype(BF16)
            yield
        sums = _dot(ss_ref[...].reshape(2 * N_PAIRS * C, LANES), ones_bd).reshape(2 * N_PAIRS, C, LANES)
        for p in range(2 * N_PAIRS):
            sm_ref[p] = sums[p]
        yield
        for p in range(N_PAIRS):
            sl = cols[p]
            lw = lw_ref[rows, sl]
            hi, mid, low = _split3(lw)
            cl = _dot(tril, hi) + (_dot(tril, mid) + _dot(tril, low))
            r, k2, v = tr_ref[p], t2_ref[p], vf_ref[p]
            kk = tk_ref[p] / jnp.maximum(jnp.sqrt(sm_ref[p]), 1e-12)
            b = kk * a_ref[rows, sl]
            cl_end = cl[C - 1:C, :]
            p_inv = jnp.exp(-cl)
            p_tail = jnp.exp(cl_end - cl)
            xa_ref[p, 0:C, :] = (-kk * jnp.exp(cl - lw)).astype(BF16)
            xa_ref[p, C:2 * C, :] = (r * jnp.exp(cl)).astype(BF16)
            yb_ref[p] = (b * p_inv).astype(BF16)
            yk_ref[p] = (k2 * p_inv).astype(BF16)
            wb_ref[p, 0:C, :] = (b * p_tail).astype(BF16)
            wb_ref[p, C:2 * C, :] = (k2 * p_tail).astype(BF16)
            pc_ref[p] = jnp.broadcast_to(jnp.exp(cl_end), (8, LANES))
            bon_ref[p] = sm_ref[N_PAIRS + p] * v
            yield

    def recur(c):
        xa_ref, yb_ref, yk_ref, vb_ref, vf_ref, wb_ref, pc_ref, bon_ref = slots[c % 2]
        rows = slice(c * C, (c + 1) * C)
        P = range(N_PAIRS)
        x = [xa_ref[p] for p in P]
        sc = [_dot_nt(x[p], jnp.concatenate([bd(yb_ref[p]), bd(yk_ref[p])], axis=0)) for p in P]
        yield
        xs = [_dot_nt(x[p], s_ref[p].astype(BF16)) for p in P]
        a_ab = [jnp.where(strict, sc[p][0:C, 0:LANES], zero).astype(BF16) for p in P]
        b_rb = [jnp.where(incl, sc[p][C:2 * C, 0:LANES], zero).astype(BF16) for p in P]
        akrk = [jnp.concatenate([jnp.where(strict, sc[p][0:C, LANES:2 * LANES], zero),
                                 jnp.where(incl, sc[p][C:2 * C, LANES:2 * LANES], zero)],
                                axis=0).astype(BF16) for p in P]
        yield
        av = [_dot(akrk[p], bd(vb_ref[p])) for p in P]
        yield
        u = [xs[p][0:C] + av[p][0:C] for p in P]
        ap = a_ab
        n_steps = 6
        for step in range(n_steps - 1):
            both = [_dot(ap[p], jnp.concatenate([bd(u[p].astype(BF16)), bd(ap[p])], axis=1)) for p in P]
            u = [u[p] + both[p][:, 0:LANES] for p in P]
            ap = [both[p][:, LANES:2 * LANES].astype(BF16) for p in P]
            yield
        u = [u[p] + _dot(ap[p], bd(u[p].astype(BF16))) for p in P]
        yield
        y = [xs[p][C:2 * C] + av[p][C:2 * C] + _dot(b_rb[p], bd(u[p].astype(BF16))) for p in P]
        yield
        uvt = [jnp.concatenate([u[p], vf_ref[p]], axis=0).T.astype(BF16) for p in P]
        zmat = [_dot(uvt[p], wb_ref[p]) for p in P]
        for p in P:
            s_ref[p] = s_ref[p] * pc_ref[p][0:1, :] + jnp.where(same_head, zmat[p], jnp.zeros_like(zmat[p]))
        yield
        inv_n = 1.0 / RWKV_HEAD_DIM

        def head_mean(ts):
            flat = jnp.concatenate([t.astype(BF16) for t in ts], axis=0)
            m = _dot(flat, ones_bd) * inv_n
            return [m[C * p:C * (p + 1)] for p in P]

        mean = head_mean(y)
        d = [y[p] - mean[p] for p in P]
        yield
        var = head_mean([d[p] * d[p] for p in P])
        for p in P:
            sl = slice(LANES * p, LANES * (p + 1))
            yn = d[p] * lax.rsqrt(var[p] + LNX_EPS) * vec(_V_LNW, sl) + vec(_V_LNB, sl)
            o_ref[rows, sl] = ((yn + bon_ref[p]) * _silu(g_ref[rows, sl])).astype(BF16)
        yield

    for _ in prepare(0):
        pass
    for c in range(NC):
        side = prepare(c + 1) if c + 1 < NC else iter(())
        _interleave(recur(c), side, main_per_side=1)

    prev_ref[0:1, :] = r_ref[TB - 1:TB, :]
    prev_ref[1:2, :] = k_ref[TB - 1:TB, :]
    prev_ref[2:3, :] = v_ref[TB - 1:TB, :]
    prevl_ref[0:1, :] = lo_ref[TB - 1:TB, :]


def _rwkv(proj_main, proj_lora, vec_tab, mu_l, w_lora_split):
    T = proj_main.shape[0]
    C = CHUNK
    TB = RWKV_CHUNKS_PER_STEP * C
    assert T % TB == 0
    pair_bf = lambda rows: pltpu.VMEM((N_PAIRS, rows, LANES), BF16)
    pair_f32 = lambda rows: pltpu.VMEM((N_PAIRS, rows, LANES), F32)
    slot = [pair_bf(2 * C), pair_bf(C), pair_bf(C), pair_bf(C), pair_f32(C), pair_bf(2 * C),
            pair_f32(8), pair_f32(C)]
    col = lambda j: pl.BlockSpec((TB, COL_BLOCK), lambda c: (c, j))
    whole = lambda shape: pl.BlockSpec(shape, lambda c: (0,) * len(shape))
    return pl.pallas_call(
        _rwkv_kernel,
        grid=(T // TB,),
        in_specs=[col(0), col(1), col(2), col(3),
                  pl.BlockSpec((TB, LANES), lambda c: (c, 0)),
                  whole((N_VEC, RWKV_WIDTH)), whole((1, LANES))]
                 + [whole((LANES, RWKV_WIDTH))] * 4,
        out_specs=pl.BlockSpec((TB, RWKV_WIDTH), lambda c: (c, 0)),
        out_shape=jax.ShapeDtypeStruct((T, RWKV_WIDTH), BF16),
        scratch_shapes=[
            pltpu.VMEM((N_PAIRS, LANES, LANES), F32),
            pltpu.VMEM((8, RWKV_WIDTH), F32),
            pltpu.VMEM((8, LANES), F32),
            pltpu.VMEM((TB, RWKV_WIDTH), F32),
            pltpu.VMEM((TB, RWKV_WIDTH), F32),
            pair_f32(C), pair_f32(C), pair_f32(C),
            pltpu.VMEM((2 * N_PAIRS, C, LANES), BF16),
            pltpu.VMEM((2 * N_PAIRS, C, LANES), F32),
        ] + slot + slot,
        compiler_params=pltpu.CompilerParams(
            dimension_semantics=("arbitrary",), vmem_limit_bytes=VMEM_LIMIT),
        name="rwkv",
    )(proj_main, proj_main, proj_main, proj_main, proj_lora, vec_tab, mu_l, *w_lora_split)


def _moba_prep_kernel(q_ref, k_ref, v_ref, invf_ref, qs_ref, kr_ref, vt_ref, bias_ref, shift_ref, margin_ref,
                      km_ref, kmax_ref, cs_ref):
    b = pl.program_id(0)
    BS = MOBA_BLOCK
    NR = bias_ref.shape[2]
    lane = lax.broadcasted_iota(jnp.int32, (BS, LANES), 1)

    @pl.when(b == 0)
    def _():
        km_ref[...] = jnp.zeros_like(km_ref)
        kmax_ref[...] = jnp.zeros_like(kmax_ref)
        local = lax.broadcasted_iota(jnp.int32, (BS, LANES), 0).astype(F32) * invf_ref[...]
        cs_ref[0] = jnp.cos(local)
        cs_ref[1] = jnp.sin(local)

    base = (b * BS).astype(F32) * invf_ref[...]
    cb, sb = jnp.cos(base), jnp.sin(base)
    cos = cb * cs_ref[0] - sb * cs_ref[1]
    sin = sb * cs_ref[0] + cb * cs_ref[1]
    cos_f = jnp.where(lane < 2 * ROT_HALF, cos, 1.0)
    sin_a = jnp.where(lane < ROT_HALF, -sin, 0.0)
    sin_b = jnp.where((lane >= ROT_HALF) & (lane < 2 * ROT_HALF), sin, 0.0)

    def rope(x):
        return (x * cos_f + pltpu.roll(x, LANES - ROT_HALF, axis=1) * sin_a
                + pltpu.roll(x, ROT_HALF, axis=1) * sin_b)

    jidx = lax.broadcasted_iota(jnp.int32, (NR, BS), 0)
    past = jidx < b
    past_bias = jnp.where(past, 0.0, NEG_INF)
    own_bias = jnp.where(jidx == b, 0.0, NEG_INF)
    ones8 = jnp.ones((8, LANES), BF16)
    scale = MOBA_HEAD_DIM ** -0.5
    ones_rows = jnp.where(lax.broadcasted_iota(jnp.int32, (V_AUG_ROWS - LANES, BS), 0) == 0, 1.0, 0.0)
    for h in range(MOBA_HEADS):
        sl = slice(LANES * h, LANES * (h + 1))
        qh = rope(q_ref[:, sl])
        kh = rope(k_ref[:, sl])
        qb = (qh * (scale * LOG2E)).astype(BF16)
        kb = kh.astype(BF16)
        qs_ref[h] = qb
        kr_ref[h, 0] = kb
        vt_ref[h, 0] = jnp.concatenate([v_ref[:, sl].T, ones_rows], axis=0).astype(BF16)
        gate = _dot_nt(km_ref[h], qh, lax.Precision.HIGHEST)
        gate = jnp.where(past, gate, NEG_INF)
        bias = jnp.full((NR, BS), NEG_INF, F32)
        for _ in range(MOBA_TOPK):
            m = jnp.max(gate, axis=0, keepdims=True)
            first = jnp.min(jnp.where(gate == m, jidx, NR), axis=0, keepdims=True)
            hit = jidx == first
            bias = jnp.where(hit, past_bias, bias)
            gate = jnp.where(hit, -3e38, gate)
        bias_ref[h, 0] = jnp.maximum(bias, own_bias)
        km_ref[h, pl.ds(b, 1), :] = jnp.mean(kh, axis=0, keepdims=True)
        qf = qb.astype(F32)
        kf = kb.astype(F32)
        qn = jnp.sqrt(_dot_nt(ones8, (qf * qf).astype(BF16)))
        kn2 = jnp.max(_dot_nt(ones8, (kf * kf).astype(BF16)), axis=1, keepdims=True)
        kmax = jnp.maximum(kmax_ref[h], jnp.sqrt(kn2))
        kmax_ref[h] = kmax
        bound = qn * kmax[:, 0:1] * 1.02 + 1e-3
        shift_ref[h, 0] = bound - SHIFT_HEADROOM
        own = _dot_nt(ones8, (qf * kf).astype(BF16))
        margin_ref[0, h:h + 1, :] = jnp.broadcast_to(jnp.max(bound - own, axis=1, keepdims=True)[0:1], (1, LANES))


def _moba_prep(proj_main, invf):
    T = proj_main.shape[0]
    nb = T // MOBA_BLOCK
    NR = -(-nb // 8) * 8
    BS = MOBA_BLOCK
    H = MOBA_HEADS
    return pl.pallas_call(
        _moba_prep_kernel,
        grid=(nb,),
        in_specs=[
            pl.BlockSpec((BS, COL_BLOCK), lambda b: (b, 4)),
            pl.BlockSpec((BS, COL_BLOCK), lambda b: (b, 5)),
            pl.BlockSpec((BS, COL_BLOCK), lambda b: (b, 6)),
            pl.BlockSpec((1, LANES), lambda b: (0, 0)),
        ],
        out_specs=[
            pl.BlockSpec((H, BS, LANES), lambda b: (0, b, 0)),
            pl.BlockSpec((H, 1, BS, LANES), lambda b: (0, b, 0, 0)),
            pl.BlockSpec((H, 1, V_AUG_ROWS, BS), lambda b: (0, b, 0, 0)),
            pl.BlockSpec((H, 1, NR, BS), lambda b: (0, b, 0, 0)),
            pl.BlockSpec((H, 1, 8, BS), lambda b: (0, b, 0, 0)),
            pl.BlockSpec((1, H, LANES), lambda b: (b, 0, 0)),
        ],
        out_shape=[
            jax.ShapeDtypeStruct((H, T, LANES), BF16),
            jax.ShapeDtypeStruct((H, nb, BS, LANES), BF16),
            jax.ShapeDtypeStruct((H, nb, V_AUG_ROWS, BS), BF16),
            jax.ShapeDtypeStruct((H, nb, NR, BS), F32),
            jax.ShapeDtypeStruct((H, nb, 8, BS), F32),
            jax.ShapeDtypeStruct((nb, H, LANES), F32),
        ],
        scratch_shapes=[pltpu.VMEM((H, NR, LANES), F32), pltpu.VMEM((H, 8, LANES), F32),
                        pltpu.VMEM((2, BS, LANES), F32)],
        compiler_params=pltpu.CompilerParams(
            dimension_semantics=("arbitrary",), vmem_limit_bytes=VMEM_LIMIT),
        name="moba_prep",
    )(proj_main, proj_main, proj_main, invf)


def _moba_attn_kernel(fast_ref, q_ref, k_ref, vt_ref, bias_ref, shift_ref, g_ref, o_ref, m_ref, l_ref, acc_ref):
    hg = pl.program_id(0)
    i = pl.program_id(1)
    BS = MOBA_BLOCK
    G = range(ATTN_HEADS)
    kidx = lax.broadcasted_iota(jnp.int32, (BS, BS), 0)
    qidx = lax.broadcasted_iota(jnp.int32, (BS, BS), 1)
    causal = kidx <= qidx

    def scores(g, j, shifted):
        row = bias_ref[g, 0, pl.ds(j, 1), :]
        if shifted:
            row = row - shift_ref[g, 0, 0:1, :]
        return _dot_nt(k_ref[g, j], q_ref[g]) + row

    def write_out(g, acc, denom):
        sl = slice(LANES * g, LANES * (g + 1))
        o_ref[:, sl] = ((acc / denom).T * _silu(g_ref[:, sl])).astype(BF16)

    @pl.when(fast_ref[hg, i] != 0)
    def _():
        def tiles(js, causal_flags):
            s = [[scores(g, j, True) for g in G] for j in js]
            s = [[jnp.where(causal, sg, NEG_INF) if flag else sg for sg in sj]
                 for sj, flag in zip(s, causal_flags)]
            p = [[jnp.exp2(sg).astype(BF16) for sg in sj] for sj in s]
            pv = [[_dot(vt_ref[g, j], p[u][g]) for g in G] for u, j in enumerate(js)]
            return [functools.reduce(lambda a, b: a + b, [pv[u][g] for u in range(len(js))]) for g in G]

        n_main = i // ATTN_UNROLL
        left = i - n_main * ATTN_UNROLL
        for r in range(ATTN_UNROLL):
            @pl.when(left == r)
            def _(r=r):
                first = tiles([i] + [n_main * ATTN_UNROLL + u for u in range(r)], [True] + [False] * r)
                for g in G:
                    acc_ref[g] = first[g]

        def main_body(t, carry):
            pv = tiles([t * ATTN_UNROLL + u for u in range(ATTN_UNROLL)], [False] * ATTN_UNROLL)
            for g in G:
                acc_ref[g] = acc_ref[g] + pv[g]
            return carry

        lax.fori_loop(0, n_main, main_body, 0)
        for g in G:
            acc = acc_ref[g]
            write_out(g, acc[0:LANES], acc[LANES:LANES + 1])

    @pl.when(fast_ref[hg, i] == 0)
    def _():
        for g in G:
            s = jnp.where(causal, scores(g, i, False), NEG_INF)
            m = jnp.max(s, axis=0, keepdims=True)
            p = jnp.exp2(s - m)
            m_ref[g] = m
            l_ref[g] = jnp.sum(p, axis=0, keepdims=True)
            acc_ref[g] = _dot(vt_ref[g, i], p.astype(BF16))

        def body(j, carry):
            s = [scores(g, j, False) for g in G]
            pb, alpha = [], []
            for g in G:
                m = m_ref[g]
                m_new = jnp.maximum(m, jnp.max(s[g], axis=0, keepdims=True))
                a = jnp.exp2(m - m_new)
                p = jnp.exp2(s[g] - m_new)
                m_ref[g] = m_new
                l_ref[g] = a * l_ref[g] + jnp.sum(p, axis=0, keepdims=True)
                pb.append(p.astype(BF16))
                alpha.append(a)
            pv = [_dot(vt_ref[g, j], pb[g]) for g in G]
            for g in G:
                acc_ref[g] = alpha[g] * acc_ref[g] + pv[g]
            return carry

        lax.fori_loop(0, i, body, 0)
        for g in G:
            write_out(g, acc_ref[g][0:LANES], l_ref[g])


def _moba_attn(fast, qs, kr, vt, bias, shift, proj_main):
    H, T, _ = qs.shape
    nb = T // MOBA_BLOCK
    BS = MOBA_BLOCK
    G = ATTN_HEADS
    NR = bias.shape[2]
    resident = pl.Buffered(1)
    grid_spec = pltpu.PrefetchScalarGridSpec(
        num_scalar_prefetch=1,
        grid=(H // G, nb),
        in_specs=[
            pl.BlockSpec((G, BS, LANES), lambda h, i, f: (h, i, 0)),
            pl.BlockSpec((G, nb, BS, LANES), lambda h, i, f: (h, 0, 0, 0), pipeline_mode=resident),
            pl.BlockSpec((G, nb, V_AUG_ROWS, BS), lambda h, i, f: (h, 0, 0, 0), pipeline_mode=resident),
            pl.BlockSpec((G, 1, NR, BS), lambda h, i, f: (h, i, 0, 0)),
            pl.BlockSpec((G, 1, 8, BS), lambda h, i, f: (h, i, 0, 0)),
            pl.BlockSpec((BS, G * LANES), lambda h, i, f: (i, GATE_M_COL * (H // G) + h)),
        ],
        out_specs=pl.BlockSpec((BS, G * LANES), lambda h, i, f: (i, h)),
        scratch_shapes=[
            pltpu.VMEM((G, 1, BS), F32), pltpu.VMEM((G, 1, BS), F32), pltpu.VMEM((G, V_AUG_ROWS, BS), F32),
        ],
    )
    return pl.pallas_call(
        _moba_attn_kernel,
        grid_spec=grid_spec,
        out_shape=jax.ShapeDtypeStruct((T, MOBA_WIDTH), BF16),
        compiler_params=pltpu.CompilerParams(
            dimension_semantics=("arbitrary", "arbitrary"), vmem_limit_bytes=VMEM_LIMIT),
        name="moba_attn",
    )(fast, qs, kr, vt, bias, shift, proj_main)


def _out_kernel(mr_ref, mm_ref, x_ref, wt_ref, wb_ref, fw_ref, o_ref):
    h = x_ref[...] + _dot(mr_ref[...], wt_ref[...]) + _dot(mm_ref[...], wb_ref[...])
    ms = jnp.mean(h * h, axis=-1, keepdims=True)
    o_ref[...] = h * lax.rsqrt(ms + RMS_EPS) * fw_ref[...]


def _out_proj(mixed_r, mixed_m, x2, w_out, fw, tm):
    T = x2.shape[0]
    return pl.pallas_call(
        _out_kernel,
        grid=(T // tm,),
        in_specs=[
            pl.BlockSpec((tm, RWKV_WIDTH), lambda i: (i, 0)),
            pl.BlockSpec((tm, MOBA_WIDTH), lambda i: (i, 0)),
            pl.BlockSpec((tm, D_MODEL), lambda i: (i, 0)),
            pl.BlockSpec((RWKV_WIDTH, D_MODEL), lambda i: (0, 0)),
            pl.BlockSpec((MOBA_WIDTH, D_MODEL), lambda i: (1, 0)),
            pl.BlockSpec((1, D_MODEL), lambda i: (0, 0)),
        ],
        out_specs=pl.BlockSpec((tm, D_MODEL), lambda i: (i, 0)),
        out_shape=jax.ShapeDtypeStruct((T, D_MODEL), F32),
        compiler_params=pltpu.CompilerParams(
            dimension_semantics=("arbitrary",), vmem_limit_bytes=VMEM_LIMIT),
        name="out_proj",
    )(mixed_r, mixed_m, x2, w_out, w_out, fw)


def kernel(x, norm_w, w_in, mu_rkv, mu_lora, w_decay_up, w_decay_bias, w_iclr_up, w_iclr_bias,
           k_k, k_a, r_k, lnx_w, lnx_b, w_out, final_norm_w):
    B, T, _ = x.shape
    assert B == 1 and T % MOBA_BLOCK == 0
    x2 = x.reshape(T, D_MODEL)

    w_main, w_lora = _weight_relayout(w_in)
    tm_proj = min(1024, T)
    proj_main, proj_lora = _project(x2, norm_w[0:1], w_main, w_lora, tm_proj)

    rows = [mu_rkv[0, 0], mu_rkv[0, 1], mu_rkv[0, 2], w_decay_bias[0], w_iclr_bias[0], k_k[0], k_a[0],
            r_k[0].reshape(-1), lnx_w[0], lnx_b[0]]
    vec_tab = jnp.stack(rows + [jnp.zeros_like(rows[0])] * (N_VEC - len(rows)), axis=0)
    mu_l = mu_lora[0].reshape(1, 2 * LORA_RANK)
    zeros_up = jnp.zeros_like(w_decay_up[0])
    wdec = jnp.concatenate([w_decay_up[0], zeros_up], axis=0)
    wicl = jnp.concatenate([zeros_up, w_iclr_up[0]], axis=0)
    w_lora_split = []
    for w_up in (wdec, wicl):
        w_hi = w_up.astype(BF16)
        w_lora_split += [w_hi, (w_up - w_hi.astype(F32)).astype(BF16)]
    mixed_r = _rwkv(proj_main, proj_lora, vec_tab, mu_l, w_lora_split)

    lane = jnp.arange(LANES)
    inv_freq = ROPE_THETA ** (-(lane % ROT_HALF).astype(F32) / ROT_HALF)
    qs, kr, vt, bias, shift, margin = _moba_prep(proj_main, inv_freq.reshape(1, LANES))
    ok = (margin[:, :, 0] <= FAST_WINDOW).reshape(-1, MOBA_HEADS // ATTN_HEADS, ATTN_HEADS)
    fast = jnp.all(ok, axis=-1).T.astype(jnp.int32)
    mixed_m = _moba_attn(fast, qs, kr, vt, bias, shift, proj_main)

    tm_out = min(512, T)
    out = _out_proj(mixed_r, mixed_m, x2, w_out[0].astype(BF16), final_norm_w.reshape(1, D_MODEL), tm_out)
    return out.reshape(B, T, D_MODEL)
```

```python
import functools

import jax
import jax.numpy as jnp
from jax import lax
from jax.experimental import pallas as pl
from jax.experimental.pallas import tpu as pltpu

F32 = jnp.float32
BF16 = jnp.bfloat16

D_MODEL = 2048
RWKV_WIDTH = 1024
RWKV_HEAD_DIM = 64
LORA_RANK = 64
MOBA_WIDTH = 1024
MOBA_HEAD_DIM = 128
MOBA_HEADS = 8
ROT_HALF = 16
ROPE_THETA = 500000.0
MOBA_BLOCK = 256
MOBA_TOPK = 3
RMS_EPS = 1e-6
LNX_EPS = 64e-5
NEG_INF = -1e30

LANES = 128
N_PAIRS = RWKV_WIDTH // LANES
CHUNK = 64
RWKV_CHUNKS_PER_STEP = 4
N_MAIN_BLOCKS = 8
GATE_M_COL = 7
COL_BLOCK = 1024
PROJ_SUB_ROWS = 256
WPREP_ROWS = 256
VMEM_LIMIT = 56 * 1024 * 1024
ATTN_HEADS = 4
ATTN_UNROLL = 8
LOG2E = 1.4426950408889634
V_AUG_ROWS = 144
SHIFT_HEADROOM = 20.0
FAST_WINDOW = 120.0


def _dot(a, b, precision=None):
    return jnp.dot(a, b, preferred_element_type=F32, precision=precision)


def _dot_nt(a, b, precision=None):
    return lax.dot_general(a, b, (((1,), (1,)), ((), ())), preferred_element_type=F32,
                           precision=precision)


def _silu(g):
    return g / (1.0 + jnp.exp(-g))


def _wprep_kernel(w_ref, om_ref, ol_ref):
    lora0 = 3 * RWKV_WIDTH
    lora1 = lora0 + 2 * LORA_RANK
    om_ref[:, 0:lora0] = w_ref[0, :, 0:lora0].astype(BF16)
    om_ref[:, lora0:] = w_ref[0, :, lora1:].astype(BF16)
    ol_ref[...] = w_ref[0, :, lora0:lora1].astype(BF16)


def _weight_relayout(w_in):
    d_in = w_in.shape[2]
    n_main = N_MAIN_BLOCKS * COL_BLOCK
    assert d_in == n_main + 2 * LORA_RANK
    return pl.pallas_call(
        _wprep_kernel,
        grid=(D_MODEL // WPREP_ROWS,),
        in_specs=[pl.BlockSpec((1, WPREP_ROWS, d_in), lambda i: (0, i, 0))],
        out_specs=[pl.BlockSpec((WPREP_ROWS, n_main), lambda i: (i, 0)),
                   pl.BlockSpec((WPREP_ROWS, 2 * LORA_RANK), lambda i: (i, 0))],
        out_shape=[jax.ShapeDtypeStruct((D_MODEL, n_main), BF16),
                   jax.ShapeDtypeStruct((D_MODEL, 2 * LORA_RANK), BF16)],
        compiler_params=pltpu.CompilerParams(
            dimension_semantics=("arbitrary",), vmem_limit_bytes=VMEM_LIMIT),
        name="wprep",
    )(w_in)


def _proj_kernel(x_ref, nw_ref, wm_ref, wl_ref, om_ref, ol_ref, xn_ref):
    n = pl.program_id(1)

    @pl.when(n == 0)
    def _():
        for r0 in range(0, x_ref.shape[0], PROJ_SUB_ROWS):
            rows = slice(r0, r0 + PROJ_SUB_ROWS)
            x = x_ref[rows, :]
            ms = jnp.mean(x * x, axis=-1, keepdims=True)
            xn = (x * lax.rsqrt(ms + RMS_EPS) * nw_ref[...]).astype(BF16)
            xn_ref[rows, :] = xn
            ol_ref[rows, :] = _dot(xn, wl_ref[...])

    om_ref[...] = _dot(xn_ref[...], wm_ref[...])


def _project(x2, norm_w, w_main, w_lora, tm):
    T = x2.shape[0]
    assert tm % PROJ_SUB_ROWS == 0
    return pl.pallas_call(
        _proj_kernel,
        grid=(T // tm, N_MAIN_BLOCKS),
        in_specs=[
            pl.BlockSpec((tm, D_MODEL), lambda i, n: (i, 0)),
            pl.BlockSpec((1, D_MODEL), lambda i, n: (0, 0)),
            pl.BlockSpec((D_MODEL, COL_BLOCK), lambda i, n: (0, n)),
            pl.BlockSpec((D_MODEL, LANES), lambda i, n: (0, 0)),
        ],
        out_specs=[
            pl.BlockSpec((tm, COL_BLOCK), lambda i, n: (i, n)),
            pl.BlockSpec((tm, LANES), lambda i, n: (i, 0)),
        ],
        out_shape=[
            jax.ShapeDtypeStruct((T, N_MAIN_BLOCKS * COL_BLOCK), F32),
            jax.ShapeDtypeStruct((T, LANES), F32),
        ],
        scratch_shapes=[pltpu.VMEM((tm, D_MODEL), BF16)],
        compiler_params=pltpu.CompilerParams(
            dimension_semantics=("arbitrary", "arbitrary"), vmem_limit_bytes=VMEM_LIMIT),
        name="proj",
    )(x2, norm_w, w_main, w_lora)


_V_MU_R, _V_MU_K, _V_MU_V, _V_DEC_B, _V_ICL_B, _V_KK, _V_KA, _V_RK, _V_LNW, _V_LNB = range(10)
N_VEC = 16


def _shift_rows(z, prev_row):
    rolled = pltpu.roll(z, 1, axis=0)
    row = lax.broadcasted_iota(jnp.int32, z.shape, 0)
    return jnp.where(row == 0, prev_row, rolled)


def _split3(x):
    hi = x.astype(BF16)
    rem = x - hi.astype(F32)
    mid = rem.astype(BF16)
    return hi, mid, (rem - mid.astype(F32)).astype(BF16)


def _interleave(main, side, main_per_side):
    main_live = side_live = True
    while main_live or side_live:
        for _ in range(main_per_side):
            if main_live and next(main, StopIteration) is StopIteration:
                main_live = False
        if side_live and next(side, StopIteration) is StopIteration:
            side_live = False


def _rwkv_kernel(r_ref, k_ref, v_ref, g_ref, lo_ref, vec_ref, mul_ref, wdh_ref, wdl_ref, wih_ref, wil_ref,
                 o_ref, s_ref, prev_ref, prevl_ref, lw_ref, a_ref, tr_ref, tk_ref, t2_ref, ss_ref, sm_ref,
                 *slots):
    C = CHUNK
    NC = RWKV_CHUNKS_PER_STEP
    TB = NC * C
    n_slot = len(slots) // 2
    slots = (slots[:n_slot], slots[n_slot:])

    @pl.when(pl.program_id(0) == 0)
    def _():
        s_ref[...] = jnp.zeros_like(s_ref)
        prev_ref[...] = jnp.zeros_like(prev_ref)
        prevl_ref[...] = jnp.zeros_like(prevl_ref)

    def vec(i, sl=slice(None)):
        return vec_ref[i:i + 1, sl]

    lo = lo_ref[...]
    lop = _shift_rows(lo, prevl_ref[0:1, :])
    lo = lo + mul_ref[...] * (lop - lo)
    lane_t = lax.broadcasted_iota(jnp.int32, (TB, LANES), 1)
    lmix = jnp.where(lane_t < LORA_RANK, jnp.tanh(lo), lo)
    l_hi = lmix.astype(BF16)
    l_lo = (lmix - l_hi.astype(F32)).astype(BF16)

    def dot3(w_hi_ref, w_lo_ref):
        w_hi = w_hi_ref[...]
        return _dot(l_hi, w_hi) + (_dot(l_lo, w_hi) + _dot(l_hi, w_lo_ref[...]))

    z = -(vec(_V_DEC_B) + dot3(wdh_ref, wdl_ref))
    softplus = jnp.maximum(z, 0.0) + jnp.log(1.0 + jnp.exp(-jnp.abs(z)))
    lw_ref[...] = -jnp.exp(-softplus - 0.5)
    a_ref[...] = 1.0 / (1.0 + jnp.exp(-(vec(_V_ICL_B) + dot3(wih_ref, wil_ref))))

    lane = lax.broadcasted_iota(jnp.int32, (C, LANES), 1)
    ti = lax.broadcasted_iota(jnp.int32, (C, C), 0)
    si = lax.broadcasted_iota(jnp.int32, (C, C), 1)
    tril = jnp.where(si <= ti, 1.0, 0.0).astype(BF16)
    tril3 = jnp.concatenate([tril, tril, tril], axis=1)
    gi = lax.broadcasted_iota(jnp.int32, (LANES, LANES), 0)
    gj = lax.broadcasted_iota(jnp.int32, (LANES, LANES), 1)
    same_head = (gi < RWKV_HEAD_DIM) == (gj < RWKV_HEAD_DIM)
    ones_bd = jnp.where(same_head, 1.0, 0.0).astype(BF16)
    lo_half = lane < RWKV_HEAD_DIM
    t_idx = lax.broadcasted_iota(jnp.int32, (C, LANES), 0)
    s_idx = lane & (RWKV_HEAD_DIM - 1)
    strict = s_idx < t_idx
    incl = s_idx <= t_idx
    zero = jnp.zeros((C, LANES), F32)

    def bd(m):
        zeros = jnp.zeros_like(m)
        return jnp.concatenate([jnp.where(lo_half, m, zeros), jnp.where(lo_half, zeros, m)], axis=0)

    def prepare(c):
        xa_ref, yb_ref, yk_ref, vb_ref, vf_ref, wb_ref, pc_ref, bon_ref = slots[c % 2]
        rows = slice(c * C, (c + 1) * C)
        cols = [slice(LANES * p, LANES * (p + 1)) for p in range(N_PAIRS)]
        for p in range(N_PAIRS):
            sl = cols[p]

            def shifted(ref, slot, mu):
                zc = ref[rows, sl]
                before = prev_ref[slot:slot + 1, sl] if c == 0 else ref[c * C - 1:c * C, sl]
                return zc + mu * (_shift_rows(zc, before) - zc)

            r = shifted(r_ref, 0, vec(_V_MU_R, sl))
            k = shifted(k_ref, 1, vec(_V_MU_K, sl))
            v = shifted(v_ref, 2, vec(_V_MU_V, sl))
            kk = k * vec(_V_KK, sl)
            k2 = k * (1.0 + (a_ref[rows, sl] - 1.0) * vec(_V_KA, sl))
            tr_ref[p] = r
            tk_ref[p] = kk
            t2_ref[p] = k2
            vf_ref[p] = v
            vb_ref[p] = v.astype(BF16)
            ss_ref[p] = (kk * kk).astype(BF16)
            ss_ref[N_PAIRS + p] = (r * k2 * vec(_V_RK, sl)).astype(BF16)
            yield
        sums = _dot(ss_ref[...].reshape(2 * N_PAIRS * C, LANES), ones_bd).reshape(2 * N_PAIRS, C, LANES)
        for p in range(2 * N_PAIRS):
            sm_ref[p] = sums[p]
        yield
        for p in range(N_PAIRS):
            sl = cols[p]
            lw = lw_ref[rows, sl]
            cl = _dot(tril3, jnp.concatenate(_split3(lw), axis=0))
            r, k2, v = tr_ref[p], t2_ref[p], vf_ref[p]
            kk = tk_ref[p] / jnp.maximum(jnp.sqrt(sm_ref[p]), 1e-12)
            b = kk * a_ref[rows, sl]
            cl_end = cl[C - 1:C, :]
            p_inv = jnp.exp(-cl)
            p_tail = jnp.exp(cl_end - cl)
            xa_ref[p, 0:C, :] = (-kk * jnp.exp(cl - lw)).astype(BF16)
            xa_ref[p, C:2 * C, :] = (r * jnp.exp(cl)).astype(BF16)
            yb_ref[p] = (b * p_inv).astype(BF16)
            yk_ref[p] = (k2 * p_inv).astype(BF16)
            wb_ref[p, 0:C, :] = (b * p_tail).astype(BF16)
            wb_ref[p, C:2 * C, :] = (k2 * p_tail).astype(BF16)
            pc_ref[p] = jnp.broadcast_to(jnp.exp(cl_end), (8, LANES))
            bon_ref[p] = sm_ref[N_PAIRS + p] * v
            yield

    def recur(c):
        xa_ref, yb_ref, yk_ref, vb_ref, vf_ref, wb_ref, pc_ref, bon_ref = slots[c % 2]
        rows = slice(c * C, (c + 1) * C)
        P = range(N_PAIRS)
        x = [xa_ref[p] for p in P]
        sc = [_dot_nt(x[p], jnp.concatenate([bd(yb_ref[p]), bd(yk_ref[p])], axis=0)) for p in P]
        yield
        st = [s_ref[p].T.astype(BF16) for p in P]
        a_ab = [jnp.where(strict, sc[p][0:C, 0:LANES], zero).astype(BF16) for p in P]
        b_rb = [jnp.where(incl, sc[p][C:2 * C, 0:LANES], zero).astype(BF16) for p in P]
        akrk = [jnp.concatenate([jnp.where(strict, sc[p][0:C, LANES:2 * LANES], zero),
                                 jnp.where(incl, sc[p][C:2 * C, LANES:2 * LANES], zero)],
                                axis=0).astype(BF16) for p in P]
        yield
        xsav = [_dot(jnp.concatenate([x[p], akrk[p]], axis=1),
                     jnp.concatenate([st[p], bd(vb_ref[p])], axis=0)) for p in P]
        yield
        u = [xsav[p][0:C] for p in P]
        ap = a_ab
        n_steps = 6
        for step in range(n_steps - 1):
            both = [_dot(ap[p], jnp.concatenate([bd(u[p].astype(BF16)), bd(ap[p])], axis=1)) for p in P]
            u = [u[p] + both[p][:, 0:LANES] for p in P]
            ap = [both[p][:, LANES:2 * LANES].astype(BF16) for p in P]
            yield
        u = [u[p] + _dot(ap[p], bd(u[p].astype(BF16))) for p in P]
        yield
        y = [xsav[p][C:2 * C] + _dot(b_rb[p], bd(u[p].astype(BF16))) for p in P]
        yield
        uvt = [jnp.concatenate([u[p], vf_ref[p]], axis=0).T.astype(BF16) for p in P]
        zmat = [_dot(uvt[p], wb_ref[p]) for p in P]
        for p in P:
            s_ref[p] = s_ref[p] * pc_ref[p][0:1, :] + jnp.where(same_head, zmat[p], jnp.zeros_like(zmat[p]))
        yield
        inv_n = 1.0 / RWKV_HEAD_DIM

        def head_mean(ts):
            flat = jnp.concatenate([t.astype(BF16) for t in ts], axis=0)
            m = _dot(flat, ones_bd) * inv_n
            return [m[C * p:C * (p + 1)] for p in P]

        mean = head_mean(y)
        d = [y[p] - mean[p] for p in P]
        yield
        var = head_mean([d[p] * d[p] for p in P])
        for p in P:
            sl = slice(LANES * p, LANES * (p + 1))
            yn = d[p] * lax.rsqrt(var[p] + LNX_EPS) * vec(_V_LNW, sl) + vec(_V_LNB, sl)
            o_ref[rows, sl] = ((yn + bon_ref[p]) * _silu(g_ref[rows, sl])).astype(BF16)
        yield

    for _ in prepare(0):
        pass
    for c in range(NC):
        side = prepare(c + 1) if c + 1 < NC else iter(())
        _interleave(recur(c), side, main_per_side=1)

    prev_ref[0:1, :] = r_ref[TB - 1:TB, :]
    prev_ref[1:2, :] = k_ref[TB - 1:TB, :]
    prev_ref[2:3, :] = v_ref[TB - 1:TB, :]
    prevl_ref[0:1, :] = lo_ref[TB - 1:TB, :]


def _rwkv(proj_main, proj_lora, vec_tab, mu_l, w_lora_split):
    T = proj_main.shape[0]
    C = CHUNK
    TB = RWKV_CHUNKS_PER_STEP * C
    assert T % TB == 0
    pair_bf = lambda rows: pltpu.VMEM((N_PAIRS, rows, LANES), BF16)
    pair_f32 = lambda rows: pltpu.VMEM((N_PAIRS, rows, LANES), F32)
    slot = [pair_bf(2 * C), pair_bf(C), pair_bf(C), pair_bf(C), pair_f32(C), pair_bf(2 * C),
            pair_f32(8), pair_f32(C)]
    col = lambda j: pl.BlockSpec((TB, COL_BLOCK), lambda c: (c, j))
    whole = lambda shape: pl.BlockSpec(shape, lambda c: (0,) * len(shape))
    return pl.pallas_call(
        _rwkv_kernel,
        grid=(T // TB,),
        in_specs=[col(0), col(1), col(2), col(3),
                  pl.BlockSpec((TB, LANES), lambda c: (c, 0)),
                  whole((N_VEC, RWKV_WIDTH)), whole((1, LANES))]
                 + [whole((LANES, RWKV_WIDTH))] * 4,
        out_specs=pl.BlockSpec((TB, RWKV_WIDTH), lambda c: (c, 0)),
        out_shape=jax.ShapeDtypeStruct((T, RWKV_WIDTH), BF16),
        scratch_shapes=[
            pltpu.VMEM((N_PAIRS, LANES, LANES), F32),
            pltpu.VMEM((8, RWKV_WIDTH), F32),
            pltpu.VMEM((8, LANES), F32),
            pltpu.VMEM((TB, RWKV_WIDTH), F32),
            pltpu.VMEM((TB, RWKV_WIDTH), F32),
            pair_f32(C), pair_f32(C), pair_f32(C),
            pltpu.VMEM((2 * N_PAIRS, C, LANES), BF16),
            pltpu.VMEM((2 * N_PAIRS, C, LANES), F32),
        ] + slot + slot,
        compiler_params=pltpu.CompilerParams(
            dimension_semantics=("arbitrary",), vmem_limit_bytes=VMEM_LIMIT),
        name="rwkv",
    )(proj_main, proj_main, proj_main, proj_main, proj_lora, vec_tab, mu_l, *w_lora_split)


def _moba_prep_kernel(q_ref, k_ref, v_ref, invf_ref, qs_ref, kr_ref, vt_ref, bias_ref, shift_ref, margin_ref,
                      km_ref, kmax_ref, cs_ref):
    b = pl.program_id(0)
    BS = MOBA_BLOCK
    NR = bias_ref.shape[2]
    lane = lax.broadcasted_iota(jnp.int32, (BS, LANES), 1)

    @pl.when(b == 0)
    def _():
        km_ref[...] = jnp.zeros_like(km_ref)
        kmax_ref[...] = jnp.zeros_like(kmax_ref)
        local = lax.broadcasted_iota(jnp.int32, (BS, LANES), 0).astype(F32) * invf_ref[...]
        cs_ref[0] = jnp.cos(local)
        cs_ref[1] = jnp.sin(local)

    base = (b * BS).astype(F32) * invf_ref[...]
    cb, sb = jnp.cos(base), jnp.sin(base)
    cos = cb * cs_ref[0] - sb * cs_ref[1]
    sin = sb * cs_ref[0] + cb * cs_ref[1]
    cos_f = jnp.where(lane < 2 * ROT_HALF, cos, 1.0)
    sin_a = jnp.where(lane < ROT_HALF, -sin, 0.0)
    sin_b = jnp.where((lane >= ROT_HALF) & (lane < 2 * ROT_HALF), sin, 0.0)

    def rope(x):
        return (x * cos_f + pltpu.roll(x, LANES - ROT_HALF, axis=1) * sin_a
                + pltpu.roll(x, ROT_HALF, axis=1) * sin_b)

    jidx = lax.broadcasted_iota(jnp.int32, (NR, BS), 0)
    past = jidx < b
    past_bias = jnp.where(past, 0.0, NEG_INF)
    own_bias = jnp.where(jidx == b, 0.0, NEG_INF)
    ones8 = jnp.ones((8, LANES), BF16)
    scale = MOBA_HEAD_DIM ** -0.5
    ones_rows = jnp.where(lax.broadcasted_iota(jnp.int32, (V_AUG_ROWS - LANES, BS), 0) == 0, 1.0, 0.0)
    for h in range(MOBA_HEADS):
        sl = slice(LANES * h, LANES * (h + 1))
        qh = rope(q_ref[:, sl])
        kh = rope(k_ref[:, sl])
        qb = (qh * (scale * LOG2E)).astype(BF16)
        kb = kh.astype(BF16)
        qs_ref[h] = qb
        kr_ref[h, 0] = kb
        vt_ref[h, 0] = jnp.concatenate([v_ref[:, sl].T, ones_rows], axis=0).astype(BF16)
        gate = _dot_nt(km_ref[h], qh, lax.Precision.HIGHEST)
        gate = jnp.where(past, gate, NEG_INF)
        bias = jnp.full((NR, BS), NEG_INF, F32)
        for _ in range(MOBA_TOPK):
            m = jnp.max(gate, axis=0, keepdims=True)
            first = jnp.min(jnp.where(gate == m, jidx, NR), axis=0, keepdims=True)
            hit = jidx == first
            bias = jnp.where(hit, past_bias, bias)
            gate = jnp.where(hit, -3e38, gate)
        bias_ref[h, 0] = jnp.maximum(bias, own_bias)
        km_ref[h, pl.ds(b, 1), :] = jnp.mean(kh, axis=0, keepdims=True)
        qf = qb.astype(F32)
        kf = kb.astype(F32)
        qn = jnp.sqrt(_dot_nt(ones8, (qf * qf).astype(BF16)))
        kn2 = jnp.max(_dot_nt(ones8, (kf * kf).astype(BF16)), axis=1, keepdims=True)
        kmax = jnp.maximum(kmax_ref[h], jnp.sqrt(kn2))
        kmax_ref[h] = kmax
        bound = qn * kmax[:, 0:1] * 1.02 + 1e-3
        shift_ref[h, 0] = bound - SHIFT_HEADROOM
        own = _dot_nt(ones8, (qf * kf).astype(BF16))
        margin_ref[0, h:h + 1, :] = jnp.broadcast_to(jnp.max(bound - own, axis=1, keepdims=True)[0:1], (1, LANES))


def _moba_prep(proj_main, invf):
    T = proj_main.shape[0]
    nb = T // MOBA_BLOCK
    NR = -(-nb // 8) * 8
    BS = MOBA_BLOCK
    H = MOBA_HEADS
    return pl.pallas_call(
        _moba_prep_kernel,
        grid=(nb,),
        in_specs=[
            pl.BlockSpec((BS, COL_BLOCK), lambda b: (b, 4)),
            pl.BlockSpec((BS, COL_BLOCK), lambda b: (b, 5)),
            pl.BlockSpec((BS, COL_BLOCK), lambda b: (b, 6)),
            pl.BlockSpec((1, LANES), lambda b: (0, 0)),
        ],
        out_specs=[
            pl.BlockSpec((H, BS, LANES), lambda b: (0, b, 0)),
            pl.BlockSpec((H, 1, BS, LANES), lambda b: (0, b, 0, 0)),
            pl.BlockSpec((H, 1, V_AUG_ROWS, BS), lambda b: (0, b, 0, 0)),
            pl.BlockSpec((H, 1, NR, BS), lambda b: (0, b, 0, 0)),
            pl.BlockSpec((H, 1, 8, BS), lambda b: (0, b, 0, 0)),
            pl.BlockSpec((1, H, LANES), lambda b: (b, 0, 0)),
        ],
        out_shape=[
            jax.ShapeDtypeStruct((H, T, LANES), BF16),
            jax.ShapeDtypeStruct((H, nb, BS, LANES), BF16),
            jax.ShapeDtypeStruct((H, nb, V_AUG_ROWS, BS), BF16),
            jax.ShapeDtypeStruct((H, nb, NR, BS), F32),
            jax.ShapeDtypeStruct((H, nb, 8, BS), F32),
            jax.ShapeDtypeStruct((nb, H, LANES), F32),
        ],
        scratch_shapes=[pltpu.VMEM((H, NR, LANES), F32), pltpu.VMEM((H, 8, LANES), F32),
                        pltpu.VMEM((2, BS, LANES), F32)],
        compiler_params=pltpu.CompilerParams(
            dimension_semantics=("arbitrary",), vmem_limit_bytes=VMEM_LIMIT),
        name="moba_prep",
    )(proj_main, proj_main, proj_main, invf)


def _moba_attn_kernel(fast_ref, q_ref, k_ref, vt_ref, bias_ref, shift_ref, g_ref, o_ref, m_ref, l_ref, acc_ref):
    hg = pl.program_id(0)
    i = pl.program_id(1)
    BS = MOBA_BLOCK
    G = range(ATTN_HEADS)
    kidx = lax.broadcasted_iota(jnp.int32, (BS, BS), 0)
    qidx = lax.broadcasted_iota(jnp.int32, (BS, BS), 1)
    causal = kidx <= qidx

    def scores(g, j, shifted):
        row = bias_ref[g, 0, pl.ds(j, 1), :]
        if shifted:
            row = row - shift_ref[g, 0, 0:1, :]
        return _dot_nt(k_ref[g, j], q_ref[g]) + row

    def write_out(g, acc, denom):
        sl = slice(LANES * g, LANES * (g + 1))
        o_ref[:, sl] = ((acc / denom).T * _silu(g_ref[:, sl])).astype(BF16)

    @pl.when(fast_ref[hg, i] != 0)
    def _():
        def tiles(js, causal_flags):
            s = [[scores(g, j, True) for g in G] for j in js]
            s = [[jnp.where(causal, sg, NEG_INF) if flag else sg for sg in sj]
                 for sj, flag in zip(s, causal_flags)]
            p = [[jnp.exp2(sg).astype(BF16) for sg in sj] for sj in s]
            pv = [[_dot(vt_ref[g, j], p[u][g]) for g in G] for u, j in enumerate(js)]
            return [functools.reduce(lambda a, b: a + b, [pv[u][g] for u in range(len(js))]) for g in G]

        n_main = i // ATTN_UNROLL
        left = i - n_main * ATTN_UNROLL
        for r in range(ATTN_UNROLL):
            @pl.when(left == r)
            def _(r=r):
                first = tiles([i] + [n_main * ATTN_UNROLL + u for u in range(r)], [True] + [False] * r)
                for g in G:
                    acc_ref[g] = first[g]

        def main_body(t, carry):
            pv = tiles([t * ATTN_UNROLL + u for u in range(ATTN_UNROLL)], [False] * ATTN_UNROLL)
            for g in G:
                acc_ref[g] = acc_ref[g] + pv[g]
            return carry

        lax.fori_loop(0, n_main, main_body, 0)
        for g in G:
            acc = acc_ref[g]
            write_out(g, acc[0:LANES], acc[LANES:LANES + 1])

    @pl.when(fast_ref[hg, i] == 0)
    def _():
        for g in G:
            s = jnp.where(causal, scores(g, i, False), NEG_INF)
            m = jnp.max(s, axis=0, keepdims=True)
            p = jnp.exp2(s - m)
            m_ref[g] = m
            l_ref[g] = jnp.sum(p, axis=0, keepdims=True)
            acc_ref[g] = _dot(vt_ref[g, i], p.astype(BF16))

        def body(j, carry):
            s = [scores(g, j, False) for g in G]
            pb, alpha = [], []
            for g in G:
                m = m_ref[g]
                m_new = jnp.maximum(m, jnp.max(s[g], axis=0, keepdims=True))
                a = jnp.exp2(m - m_new)
                p = jnp.exp2(s[g] - m_new)
                m_ref[g] = m_new
                l_ref[g] = a * l_ref[g] + jnp.sum(p, axis=0, keepdims=True)
                pb.append(p.astype(BF16))
                alpha.append(a)
            pv = [_dot(vt_ref[g, j], pb[g]) for g in G]
            for g in G:
                acc_ref[g] = alpha[g] * acc_ref[g] + pv[g]
            return carry

        lax.fori_loop(0, i, body, 0)
        for g in G:
            write_out(g, acc_ref[g][0:LANES], l_ref[g])


def _moba_attn(fast, qs, kr, vt, bias, shift, proj_main):
    H, T, _ = qs.shape
    nb = T // MOBA_BLOCK
    BS = MOBA_BLOCK
    G = ATTN_HEADS
    NR = bias.shape[2]
    resident = pl.Buffered(1)
    grid_spec = pltpu.PrefetchScalarGridSpec(
        num_scalar_prefetch=1,
        grid=(H // G, nb),
        in_specs=[
            pl.BlockSpec((G, BS, LANES), lambda h, i, f: (h, i, 0)),
            pl.BlockSpec((G, nb, BS, LANES), lambda h, i, f: (h, 0, 0, 0), pipeline_mode=resident),
            pl.BlockSpec((G, nb, V_AUG_ROWS, BS), lambda h, i, f: (h, 0, 0, 0), pipeline_mode=resident),
            pl.BlockSpec((G, 1, NR, BS), lambda h, i, f: (h, i, 0, 0)),
            pl.BlockSpec((G, 1, 8, BS), lambda h, i, f: (h, i, 0, 0)),
            pl.BlockSpec((BS, G * LANES), lambda h, i, f: (i, GATE_M_COL * (H // G) + h)),
        ],
        out_specs=pl.BlockSpec((BS, G * LANES), lambda h, i, f: (i, h)),
        scratch_shapes=[
            pltpu.VMEM((G, 1, BS), F32), pltpu.VMEM((G, 1, BS), F32), pltpu.VMEM((G, V_AUG_ROWS, BS), F32),
        ],
    )
    return pl.pallas_call(
        _moba_attn_kernel,
        grid_spec=grid_spec,
        out_shape=jax.ShapeDtypeStruct((T, MOBA_WIDTH), BF16),
        compiler_params=pltpu.CompilerParams(
            dimension_semantics=("arbitrary", "arbitrary"), vmem_limit_bytes=VMEM_LIMIT),
        name="moba_attn",
    )(fast, qs, kr, vt, bias, shift, proj_main)


def _out_kernel(mr_ref, mm_ref, x_ref, wt_ref, wb_ref, fw_ref, o_ref):
    h = x_ref[...] + _dot(mr_ref[...], wt_ref[...]) + _dot(mm_ref[...], wb_ref[...])
    ms = jnp.mean(h * h, axis=-1, keepdims=True)
    o_ref[...] = h * lax.rsqrt(ms + RMS_EPS) * fw_ref[...]


def _out_proj(mixed_r, mixed_m, x2, w_out, fw, tm):
    T = x2.shape[0]
    return pl.pallas_call(
        _out_kernel,
        grid=(T // tm,),
        in_specs=[
            pl.BlockSpec((tm, RWKV_WIDTH), lambda i: (i, 0)),
            pl.BlockSpec((tm, MOBA_WIDTH), lambda i: (i, 0)),
            pl.BlockSpec((tm, D_MODEL), lambda i: (i, 0)),
            pl.BlockSpec((RWKV_WIDTH, D_MODEL), lambda i: (0, 0)),
            pl.BlockSpec((MOBA_WIDTH, D_MODEL), lambda i: (1, 0)),
            pl.BlockSpec((1, D_MODEL), lambda i: (0, 0)),
        ],
        out_specs=pl.BlockSpec((tm, D_MODEL), lambda i: (i, 0)),
        out_shape=jax.ShapeDtypeStruct((T, D_MODEL), F32),
        compiler_params=pltpu.CompilerParams(
            dimension_semantics=("arbitrary",), vmem_limit_bytes=VMEM_LIMIT),
        name="out_proj",
    )(mixed_r, mixed_m, x2, w_out, w_out, fw)


def kernel(x, norm_w, w_in, mu_rkv, mu_lora, w_decay_up, w_decay_bias, w_iclr_up, w_iclr_bias,
           k_k, k_a, r_k, lnx_w, lnx_b, w_out, final_norm_w):
    B, T, _ = x.shape
    assert B == 1 and T % MOBA_BLOCK == 0
    x2 = x.reshape(T, D_MODEL)

    w_main, w_lora = _weight_relayout(w_in)
    tm_proj = min(1024, T)
    proj_main, proj_lora = _project(x2, norm_w[0:1], w_main, w_lora, tm_proj)

    rows = [mu_rkv[0, 0], mu_rkv[0, 1], mu_rkv[0, 2], w_decay_bias[0], w_iclr_bias[0], k_k[0], k_a[0],
            r_k[0].reshape(-1), lnx_w[0], lnx_b[0]]
    vec_tab = jnp.stack(rows + [jnp.zeros_like(rows[0])] * (N_VEC - len(rows)), axis=0)
    mu_l = mu_lora[0].reshape(1, 2 * LORA_RANK)
    zeros_up = jnp.zeros_like(w_decay_up[0])
    wdec = jnp.concatenate([w_decay_up[0], zeros_up], axis=0)
    wicl = jnp.concatenate([zeros_up, w_iclr_up[0]], axis=0)
    w_lora_split = []
    for w_up in (wdec, wicl):
        w_hi = w_up.astype(BF16)
        w_lora_split += [w_hi, (w_up - w_hi.astype(F32)).astype(BF16)]
    mixed_r = _rwkv(proj_main, proj_lora, vec_tab, mu_l, w_lora_split)

    lane = jnp.arange(LANES)
    inv_freq = ROPE_THETA ** (-(lane % ROT_HALF).astype(F32) / ROT_HALF)
    qs, kr, vt, bias, shift, margin = _moba_prep(proj_main, inv_freq.reshape(1, LANES))
    ok = (margin[:, :, 0] <= FAST_WINDOW).reshape(-1, MOBA_HEADS // ATTN_HEADS, ATTN_HEADS)
    fast = jnp.all(ok, axis=-1).T.astype(jnp.int32)
    mixed_m = _moba_attn(fast, qs, kr, vt, bias, shift, proj_main)

    tm_out = min(512, T)
    out = _out_proj(mixed_r, mixed_m, x2, w_out[0].astype(BF16), final_norm_w.reshape(1, D_MODEL), tm_out)
    return out.reshape(B, T, D_MODEL)
```

```python
import functools

import jax
import jax.numpy as jnp
from jax import lax
from jax.experimental import pallas as pl
from jax.experimental.pallas import tpu as pltpu

F32 = jnp.float32
BF16 = jnp.bfloat16

D_MODEL = 2048
RWKV_WIDTH = 1024
RWKV_HEAD_DIM = 64
LORA_RANK = 64
MOBA_WIDTH = 1024
MOBA_HEAD_DIM = 128
MOBA_HEADS = 8
ROT_HALF = 16
ROT_PAIR_LANE = 64
ROPE_THETA = 500000.0
MOBA_BLOCK = 256
MOBA_TOPK = 3
RMS_EPS = 1e-6
LNX_EPS = 64e-5
NEG_INF = -1e30

LANES = 128
N_PAIRS = RWKV_WIDTH // LANES
CHUNK = 64
RWKV_CHUNKS_PER_STEP = 4
N_MAIN_BLOCKS = 8
GATE_M_COL = 7
COL_BLOCK = 1024
PROJ_SUB_ROWS = 256
WPREP_ROWS = 256
VMEM_LIMIT = 56 * 1024 * 1024
ATTN_HEADS = 4
ATTN_UNROLL = 16
LOG2E = 1.4426950408889634
V_AUG_ROWS = 144
SHIFT_HEADROOM = 20.0
FAST_WINDOW = 120.0


def _dot(a, b, precision=None):
    return jnp.dot(a, b, preferred_element_type=F32, precision=precision)


def _dot_nt(a, b, precision=None):
    return lax.dot_general(a, b, (((1,), (1,)), ((), ())), preferred_element_type=F32,
                           precision=precision)


def _silu(g):
    return g / (1.0 + jnp.exp(-g))


def _wprep_kernel(w_ref, om_ref, ol_ref):
    lora0 = 3 * RWKV_WIDTH
    lora1 = lora0 + 2 * LORA_RANK
    skip = lora1 - lora0
    qk0, qk1 = 4 * COL_BLOCK, 6 * COL_BLOCK
    om_ref[:, 0:lora0] = w_ref[0, :, 0:lora0].astype(BF16)
    om_ref[:, lora0:qk0] = w_ref[0, :, lora1:qk0 + skip].astype(BF16)
    om_ref[:, qk1:] = w_ref[0, :, qk1 + skip:].astype(BF16)
    ol_ref[...] = w_ref[0, :, lora0:lora1].astype(BF16)
    lane = lax.broadcasted_iota(jnp.int32, (w_ref.shape[1], LANES), 1)
    up = (lane >= ROT_HALF) & (lane < 2 * ROT_HALF)
    down = (lane >= ROT_PAIR_LANE) & (lane < ROT_PAIR_LANE + ROT_HALF)
    for c0 in range(qk0, qk1, LANES):
        x = w_ref[0, :, c0 + skip:c0 + skip + LANES]
        moved = jnp.where(up, pltpu.roll(x, LANES - (ROT_PAIR_LANE - ROT_HALF), axis=1),
                          jnp.where(down, pltpu.roll(x, ROT_PAIR_LANE - ROT_HALF, axis=1), x))
        om_ref[:, c0:c0 + LANES] = moved.astype(BF16)


def _weight_relayout(w_in):
    d_in = w_in.shape[2]
    n_main = N_MAIN_BLOCKS * COL_BLOCK
    assert d_in == n_main + 2 * LORA_RANK
    return pl.pallas_call(
        _wprep_kernel,
        grid=(D_MODEL // WPREP_ROWS,),
        in_specs=[pl.BlockSpec((1, WPREP_ROWS, d_in), lambda i: (0, i, 0))],
        out_specs=[pl.BlockSpec((WPREP_ROWS, n_main), lambda i: (i, 0)),
                   pl.BlockSpec((WPREP_ROWS, 2 * LORA_RANK), lambda i: (i, 0))],
        out_shape=[jax.ShapeDtypeStruct((D_MODEL, n_main), BF16),
                   jax.ShapeDtypeStruct((D_MODEL, 2 * LORA_RANK), BF16)],
        compiler_params=pltpu.CompilerParams(
            dimension_semantics=("arbitrary",), vmem_limit_bytes=VMEM_LIMIT),
        name="wprep",
    )(w_in)


def _proj_kernel(x_ref, nw_ref, wm_ref, wl_ref, om_ref, ol_ref, xn_ref):
    n = pl.program_id(1)

    @pl.when(n == 0)
    def _():
        for r0 in range(0, x_ref.shape[0], PROJ_SUB_ROWS):
            rows = slice(r0, r0 + PROJ_SUB_ROWS)
            x = x_ref[rows, :]
            ms = jnp.mean(x * x, axis=-1, keepdims=True)
            xn = (x * lax.rsqrt(ms + RMS_EPS) * nw_ref[...]).astype(BF16)
            xn_ref[rows, :] = xn
            ol_ref[rows, :] = _dot(xn, wl_ref[...])

    om_ref[...] = _dot(xn_ref[...], wm_ref[...])


def _project(x2, norm_w, w_main, w_lora, tm):
    T = x2.shape[0]
    assert T % tm == 0 and tm % PROJ_SUB_ROWS == 0
    return pl.pallas_call(
        _proj_kernel,
        grid=(T // tm, N_MAIN_BLOCKS),
        in_specs=[
            pl.BlockSpec((tm, D_MODEL), lambda i, n: (i, 0)),
            pl.BlockSpec((1, D_MODEL), lambda i, n: (0, 0)),
            pl.BlockSpec((D_MODEL, COL_BLOCK), lambda i, n: (0, n)),
            pl.BlockSpec((D_MODEL, LANES), lambda i, n: (0, 0)),
        ],
        out_specs=[
            pl.BlockSpec((tm, COL_BLOCK), lambda i, n: (i, n)),
            pl.BlockSpec((tm, LANES), lambda i, n: (i, 0)),
        ],
        out_shape=[
            jax.ShapeDtypeStruct((T, N_MAIN_BLOCKS * COL_BLOCK), F32),
            jax.ShapeDtypeStruct((T, LANES), F32),
        ],
        scratch_shapes=[pltpu.VMEM((tm, D_MODEL), BF16)],
        compiler_params=pltpu.CompilerParams(
            dimension_semantics=("arbitrary", "arbitrary"), vmem_limit_bytes=VMEM_LIMIT),
        name="proj",
    )(x2, norm_w, w_main, w_lora)


_V_MU_R, _V_MU_K, _V_MU_V, _V_DEC_B, _V_ICL_B, _V_KK, _V_KA, _V_RK, _V_LNW, _V_LNB = range(10)
N_VEC = 16


def _shift_rows(z, prev_row):
    rolled = pltpu.roll(z, 1, axis=0)
    row = lax.broadcasted_iota(jnp.int32, z.shape, 0)
    return jnp.where(row == 0, prev_row, rolled)


def _split3(x):
    hi = x.astype(BF16)
    rem = x - hi.astype(F32)
    mid = rem.astype(BF16)
    return hi, mid, (rem - mid.astype(F32)).astype(BF16)


def _interleave(main, side, main_per_side):
    main_live = side_live = True
    while main_live or side_live:
        for _ in range(main_per_side):
            if main_live and next(main, StopIteration) is StopIteration:
                main_live = False
        if side_live and next(side, StopIteration) is StopIteration:
            side_live = False


def _rwkv_kernel(r_ref, k_ref, v_ref, g_ref, lo_ref, vec_ref, mul_ref, wdh_ref, wdl_ref, wih_ref, wil_ref,
                 o_ref, s_ref, prev_ref, prevl_ref, lw_ref, a_ref, tr_ref, tk_ref, t2_ref, ss_ref, sm_ref,
                 *slots):
    C = CHUNK
    NC = RWKV_CHUNKS_PER_STEP
    TB = NC * C
    n_slot = len(slots) // 2
    slots = (slots[:n_slot], slots[n_slot:])

    @pl.when(pl.program_id(0) == 0)
    def _():
        s_ref[...] = jnp.zeros_like(s_ref)
        prev_ref[...] = jnp.zeros_like(prev_ref)
        prevl_ref[...] = jnp.zeros_like(prevl_ref)

    def vec(i, sl=slice(None)):
        return vec_ref[i:i + 1, sl]

    lo = lo_ref[...]
    lop = _shift_rows(lo, prevl_ref[0:1, :])
    lo = lo + mul_ref[...] * (lop - lo)
    lane_t = lax.broadcasted_iota(jnp.int32, (TB, LANES), 1)
    lmix = jnp.where(lane_t < LORA_RANK, jnp.tanh(lo), lo)
    l_hi = lmix.astype(BF16)
    l_lo = (lmix - l_hi.astype(F32)).astype(BF16)

    def dot3(w_hi_ref, w_lo_ref):
        w_hi = w_hi_ref[...]
        return _dot(l_hi, w_hi) + (_dot(l_lo, w_hi) + _dot(l_hi, w_lo_ref[...]))

    z = -(vec(_V_DEC_B) + dot3(wdh_ref, wdl_ref))
    softplus = jnp.maximum(z, 0.0) + jnp.log(1.0 + jnp.exp(-jnp.abs(z)))
    lw_ref[...] = -jnp.exp(-softplus - 0.5)
    a_ref[...] = 1.0 / (1.0 + jnp.exp(-(vec(_V_ICL_B) + dot3(wih_ref, wil_ref))))

    lane = lax.broadcasted_iota(jnp.int32, (C, LANES), 1)
    ti = lax.broadcasted_iota(jnp.int32, (C, C), 0)
    si = lax.broadcasted_iota(jnp.int32, (C, C), 1)
    tril = jnp.where(si <= ti, 1.0, 0.0).astype(BF16)
    tril3 = jnp.concatenate([tril, tril, tril], axis=1)
    gi = lax.broadcasted_iota(jnp.int32, (LANES, LANES), 0)
    gj = lax.broadcasted_iota(jnp.int32, (LANES, LANES), 1)
    same_head = (gi < RWKV_HEAD_DIM) == (gj < RWKV_HEAD_DIM)
    ones_bd = jnp.where(same_head, 1.0, 0.0).astype(BF16)
    lo_half = lane < RWKV_HEAD_DIM
    t_idx = lax.broadcasted_iota(jnp.int32, (C, LANES), 0)
    s_idx = lane & (RWKV_HEAD_DIM - 1)
    strict = s_idx < t_idx
    incl = s_idx <= t_idx
    zero = jnp.zeros((C, LANES), F32)

    def bd(m):
        zeros = jnp.zeros_like(m)
        return jnp.concatenate([jnp.where(lo_half, m, zeros), jnp.where(lo_half, zeros, m)], axis=0)

    def prepare(c):
        xa_ref, yb_ref, yk_ref, vb_ref, vf_ref, wb_ref, pc_ref, bon_ref = slots[c % 2]
        rows = slice(c * C, (c + 1) * C)
        cols = [slice(LANES * p, LANES * (p + 1)) for p in range(N_PAIRS)]
        for p in range(N_PAIRS):
            sl = cols[p]

            def shifted(ref, slot, mu):
                zc = ref[rows, sl]
                before = prev_ref[slot:slot + 1, sl] if c == 0 else ref[c * C - 1:c * C, sl]
                return zc + mu * (_shift_rows(zc, before) - zc)

            r = shifted(r_ref, 0, vec(_V_MU_R, sl))
            k = shifted(k_ref, 1, vec(_V_MU_K, sl))
            v = shifted(v_ref, 2, vec(_V_MU_V, sl))
            kk = k * vec(_V_KK, sl)
            k2 = k * (1.0 + (a_ref[rows, sl] - 1.0) * vec(_V_KA, sl))
            tr_ref[p] = r
            tk_ref[p] = kk
            t2_ref[p] = k2
            vf_ref[p] = v
            vb_ref[p] = v.astype(BF16)
            ss_ref[p] = (kk * kk).astype(BF16)
            ss_ref[N_PAIRS + p] = (r * k2 * vec(_V_RK, sl)).astype(BF16)
            yield
        sums = _dot(ss_ref[...].reshape(2 * N_PAIRS * C, LANES), ones_bd).reshape(2 * N_PAIRS, C, LANES)
        for p in range(2 * N_PAIRS):
            sm_ref[p] = sums[p]
        yield
        for p in range(N_PAIRS):
            sl = cols[p]
            lw = lw_ref[rows, sl]
            cl = _dot(tril3, jnp.concatenate(_split3(lw), axis=0))
            r, k2, v = tr_ref[p], t2_ref[p], vf_ref[p]
            kk = tk_ref[p] / jnp.maximum(jnp.sqrt(sm_ref[p]), 1e-12)
            b = kk * a_ref[rows, sl]
            cl_end = cl[C - 1:C, :]
            p_inv = jnp.exp(-cl)
            p_tail = jnp.exp(cl_end - cl)
            xa_ref[p, 0:C, :] = (-kk * jnp.exp(cl - lw)).astype(BF16)
            xa_ref[p, C:2 * C, :] = (r * jnp.exp(cl)).astype(BF16)
            yb_ref[p] = (b * p_inv).astype(BF16)
            yk_ref[p] = (k2 * p_inv).astype(BF16)
            wb_ref[p, 0:C, :] = (b * p_tail).astype(BF16)
            wb_ref[p, C:2 * C, :] = (k2 * p_tail).astype(BF16)
            pc_ref[p] = jnp.broadcast_to(jnp.exp(cl_end), (8, LANES))
            bon_ref[p] = sm_ref[N_PAIRS + p] * v
            yield

    def recur(c):
        xa_ref, yb_ref, yk_ref, vb_ref, vf_ref, wb_ref, pc_ref, bon_ref = slots[c % 2]
        rows = slice(c * C, (c + 1) * C)
        P = range(N_PAIRS)
        x = [xa_ref[p] for p in P]
        sc = [_dot_nt(x[p], jnp.concatenate([bd(yb_ref[p]), bd(yk_ref[p])], axis=0)) for p in P]
        yield
        st = [s_ref[p].T.astype(BF16) for p in P]
        a_ab = [jnp.where(strict, sc[p][0:C, 0:LANES], zero).astype(BF16) for p in P]
        b_rb = [jnp.where(incl, sc[p][C:2 * C, 0:LANES], zero).astype(BF16) for p in P]
        akrk = [jnp.concatenate([jnp.where(strict, sc[p][0:C, LANES:2 * LANES], zero),
                                 jnp.where(incl, sc[p][C:2 * C, LANES:2 * LANES], zero)],
                                axis=0).astype(BF16) for p in P]
        yield
        xsav = [_dot(jnp.concatenate([x[p], akrk[p]], axis=1),
                     jnp.concatenate([st[p], bd(vb_ref[p])], axis=0)) for p in P]
        yield
        u = [xsav[p][0:C] for p in P]
        ap = a_ab
        n_steps = 6
        for step in range(n_steps - 1):
            both = [_dot(ap[p], jnp.concatenate([bd(u[p].astype(BF16)), bd(ap[p])], axis=1)) for p in P]
            u = [u[p] + both[p][:, 0:LANES] for p in P]
            ap = [both[p][:, LANES:2 * LANES].astype(BF16) for p in P]
            yield
        u = [u[p] + _dot(ap[p], bd(u[p].astype(BF16))) for p in P]
        yield
        y = [xsav[p][C:2 * C] + _dot(b_rb[p], bd(u[p].astype(BF16))) for p in P]
        yield
        uvt = [jnp.concatenate([u[p], vf_ref[p]], axis=0).T.astype(BF16) for p in P]
        zmat = [_dot(uvt[p], wb_ref[p]) for p in P]
        for p in P:
            s_ref[p] = s_ref[p] * pc_ref[p][0:1, :] + jnp.where(same_head, zmat[p], jnp.zeros_like(zmat[p]))
        yield
        inv_n = 1.0 / RWKV_HEAD_DIM

        def head_mean(ts):
            flat = jnp.concatenate([t.astype(BF16) for t in ts], axis=0)
            m = _dot(flat, ones_bd) * inv_n
            return [m[C * p:C * (p + 1)] for p in P]

        mean = head_mean(y)
        d = [y[p] - mean[p] for p in P]
        yield
        var = head_mean([d[p] * d[p] for p in P])
        for p in P:
            sl = slice(LANES * p, LANES * (p + 1))
            yn = d[p] * lax.rsqrt(var[p] + LNX_EPS) * vec(_V_LNW, sl) + vec(_V_LNB, sl)
            o_ref[rows, sl] = ((yn + bon_ref[p]) * _silu(g_ref[rows, sl])).astype(BF16)
        yield

    for _ in prepare(0):
        pass
    for c in range(NC):
        side = prepare(c + 1) if c + 1 < NC else iter(())
        _interleave(recur(c), side, main_per_side=1)

    prev_ref[0:1, :] = r_ref[TB - 1:TB, :]
    prev_ref[1:2, :] = k_ref[TB - 1:TB, :]
    prev_ref[2:3, :] = v_ref[TB - 1:TB, :]
    prevl_ref[0:1, :] = lo_ref[TB - 1:TB, :]


def _rwkv(proj_main, proj_lora, vec_tab, mu_l, w_lora_split):
    T = proj_main.shape[0]
    C = CHUNK
    TB = RWKV_CHUNKS_PER_STEP * C
    assert T % TB == 0
    pair_bf = lambda rows: pltpu.VMEM((N_PAIRS, rows, LANES), BF16)
    pair_f32 = lambda rows: pltpu.VMEM((N_PAIRS, rows, LANES), F32)
    slot = [pair_bf(2 * C), pair_bf(C), pair_bf(C), pair_bf(C), pair_f32(C), pair_bf(2 * C),
            pair_f32(8), pair_f32(C)]
    col = lambda j: pl.BlockSpec((TB, COL_BLOCK), lambda c: (c, j))
    whole = lambda shape: pl.BlockSpec(shape, lambda c: (0,) * len(shape))
    return pl.pallas_call(
        _rwkv_kernel,
        grid=(T // TB,),
        in_specs=[col(0), col(1), col(2), col(3),
                  pl.BlockSpec((TB, LANES), lambda c: (c, 0)),
                  whole((N_VEC, RWKV_WIDTH)), whole((1, LANES))]
                 + [whole((LANES, RWKV_WIDTH))] * 4,
        out_specs=pl.BlockSpec((TB, RWKV_WIDTH), lambda c: (c, 0)),
        out_shape=jax.ShapeDtypeStruct((T, RWKV_WIDTH), BF16),
        scratch_shapes=[
            pltpu.VMEM((N_PAIRS, LANES, LANES), F32),
            pltpu.VMEM((8, RWKV_WIDTH), F32),
            pltpu.VMEM((8, LANES), F32),
            pltpu.VMEM((TB, RWKV_WIDTH), F32),
            pltpu.VMEM((TB, RWKV_WIDTH), F32),
            pair_f32(C), pair_f32(C), pair_f32(C),
            pltpu.VMEM((2 * N_PAIRS, C, LANES), BF16),
            pltpu.VMEM((2 * N_PAIRS, C, LANES), F32),
        ] + slot + slot,
        compiler_params=pltpu.CompilerParams(
            dimension_semantics=("arbitrary",), vmem_limit_bytes=VMEM_LIMIT),
        name="rwkv",
    )(proj_main, proj_main, proj_main, proj_main, proj_lora, vec_tab, mu_l, *w_lora_split)


def _moba_prep_kernel(q_ref, k_ref, v_ref, invf_ref, qs_ref, kr_ref, vt_ref, bias_ref, shift_ref, margin_ref,
                      km_ref, kmax_ref, cs_ref):
    b = pl.program_id(0)
    BS = MOBA_BLOCK
    NR = bias_ref.shape[2]
    lane = lax.broadcasted_iota(jnp.int32, (BS, LANES), 1)

    @pl.when(b == 0)
    def _():
        km_ref[...] = jnp.zeros_like(km_ref)
        kmax_ref[...] = jnp.zeros_like(kmax_ref)
        local = lax.broadcasted_iota(jnp.int32, (BS, LANES), 0).astype(F32) * invf_ref[...]
        cs_ref[0] = jnp.cos(local)
        cs_ref[1] = jnp.sin(local)

    base = (b * BS).astype(F32) * invf_ref[...]
    cb, sb = jnp.cos(base), jnp.sin(base)
    cos = cb * cs_ref[0] - sb * cs_ref[1]
    sin = sb * cs_ref[0] + cb * cs_ref[1]
    first = lane < ROT_HALF
    second = (lane >= ROT_PAIR_LANE) & (lane < ROT_PAIR_LANE + ROT_HALF)
    cos_f = jnp.where(first | second, cos, 1.0)
    sin_t = jnp.where(first, sin, jnp.where(second, -sin, 0.0))

    def rope(x):
        return x * cos_f + pltpu.roll(x * sin_t, ROT_PAIR_LANE, axis=1)

    jidx = lax.broadcasted_iota(jnp.int32, (NR, BS), 0)
    past = jidx < b
    past_bias = jnp.where(past, 0.0, NEG_INF)
    own_bias = jnp.where(jidx == b, 0.0, NEG_INF)
    ones8 = jnp.ones((8, LANES), BF16)
    scale = MOBA_HEAD_DIM ** -0.5
    ones_rows = jnp.where(lax.broadcasted_iota(jnp.int32, (V_AUG_ROWS - LANES, BS), 0) == 0, 1.0, 0.0)
    for h in range(MOBA_HEADS):
        sl = slice(LANES * h, LANES * (h + 1))
        qh = rope(q_ref[:, sl])
        kh = rope(k_ref[:, sl])
        qb = (qh * (scale * LOG2E)).astype(BF16)
        kb = kh.astype(BF16)
        qs_ref[h] = qb
        kr_ref[h, 0] = kb
        vt_ref[h, 0] = jnp.concatenate([v_ref[:, sl].T, ones_rows], axis=0).astype(BF16)
        gate = _dot_nt(km_ref[h], qh, lax.Precision.HIGHEST)
        gate = jnp.where(past, gate, NEG_INF)
        bias = jnp.full((NR, BS), NEG_INF, F32)
        for _ in range(MOBA_TOPK):
            m = jnp.max(gate, axis=0, keepdims=True)
            first = jnp.min(jnp.where(gate == m, jidx, NR), axis=0, keepdims=True)
            hit = jidx == first
            bias = jnp.where(hit, past_bias, bias)
            gate = jnp.where(hit, -3e38, gate)
        bias_ref[h, 0] = jnp.maximum(bias, own_bias)
        km_ref[h, pl.ds(b, 1), :] = jnp.mean(kh, axis=0, keepdims=True)
        qf = qb.astype(F32)
        kf = kb.astype(F32)
        qn = jnp.sqrt(_dot_nt(ones8, (qf * qf).astype(BF16)))
        kn2 = jnp.max(_dot_nt(ones8, (kf * kf).astype(BF16)), axis=1, keepdims=True)
        kmax = jnp.maximum(kmax_ref[h], jnp.sqrt(kn2))
        kmax_ref[h] = kmax
        bound = qn * kmax[:, 0:1] * 1.02 + 1e-3
        shift_ref[h, 0] = bound - SHIFT_HEADROOM
        own = _dot_nt(ones8, (qf * kf).astype(BF16))
        margin_ref[0, h:h + 1, :] = jnp.broadcast_to(jnp.max(bound - own, axis=1, keepdims=True)[0:1], (1, LANES))


def _moba_prep(proj_main, invf):
    T = proj_main.shape[0]
    nb = T // MOBA_BLOCK
    NR = -(-nb // 8) * 8
    BS = MOBA_BLOCK
    H = MOBA_HEADS
    return pl.pallas_call(
        _moba_prep_kernel,
        grid=(nb,),
        in_specs=[
            pl.BlockSpec((BS, COL_BLOCK), lambda b: (b, 4)),
            pl.BlockSpec((BS, COL_BLOCK), lambda b: (b, 5)),
            pl.BlockSpec((BS, COL_BLOCK), lambda b: (b, 6)),
            pl.BlockSpec((1, LANES), lambda b: (0, 0)),
        ],
        out_specs=[
            pl.BlockSpec((H, BS, LANES), lambda b: (0, b, 0)),
            pl.BlockSpec((H, 1, BS, LANES), lambda b: (0, b, 0, 0)),
            pl.BlockSpec((H, 1, V_AUG_ROWS, BS), lambda b: (0, b, 0, 0)),
            pl.BlockSpec((H, 1, NR, BS), lambda b: (0, b, 0, 0)),
            pl.BlockSpec((H, 1, 8, BS), lambda b: (0, b, 0, 0)),
            pl.BlockSpec((1, H, LANES), lambda b: (b, 0, 0)),
        ],
        out_shape=[
            jax.ShapeDtypeStruct((H, T, LANES), BF16),
            jax.ShapeDtypeStruct((H, nb, BS, LANES), BF16),
            jax.ShapeDtypeStruct((H, nb, V_AUG_ROWS, BS), BF16),
            jax.ShapeDtypeStruct((H, nb, NR, BS), F32),
            jax.ShapeDtypeStruct((H, nb, 8, BS), F32),
            jax.ShapeDtypeStruct((nb, H, LANES), F32),
        ],
        scratch_shapes=[pltpu.VMEM((H, NR, LANES), F32), pltpu.VMEM((H, 8, LANES), F32),
                        pltpu.VMEM((2, BS, LANES), F32)],
        compiler_params=pltpu.CompilerParams(
            dimension_semantics=("arbitrary",), vmem_limit_bytes=VMEM_LIMIT),
        name="moba_prep",
    )(proj_main, proj_main, proj_main, invf)


def _moba_attn_kernel(fast_ref, q_ref, k_ref, vt_ref, bias_ref, shift_ref, g_ref, o_ref, m_ref, l_ref, acc_ref):
    hg = pl.program_id(0)
    i = pl.program_id(1)
    BS = MOBA_BLOCK
    G = range(ATTN_HEADS)
    kidx = lax.broadcasted_iota(jnp.int32, (BS, BS), 0)
    qidx = lax.broadcasted_iota(jnp.int32, (BS, BS), 1)
    causal = kidx <= qidx

    def scores(g, j, shifted):
        row = bias_ref[g, 0, pl.ds(j, 1), :]
        if shifted:
            row = row - shift_ref[g, 0, 0:1, :]
        return _dot_nt(k_ref[g, j], q_ref[g]) + row

    def write_out(g, acc, denom):
        sl = slice(LANES * g, LANES * (g + 1))
        o_ref[:, sl] = ((acc / denom).T * _silu(g_ref[:, sl])).astype(BF16)

    @pl.when(fast_ref[hg, i] != 0)
    def _():
        def tiles(js, causal_flags):
            s = [[scores(g, j, True) for g in G] for j in js]
            s = [[jnp.where(causal, sg, NEG_INF) if flag else sg for sg in sj]
                 for sj, flag in zip(s, causal_flags)]
            p = [[jnp.exp2(sg).astype(BF16) for sg in sj] for sj in s]
            pv = [[_dot(vt_ref[g, j], p[u][g]) for g in G] for u, j in enumerate(js)]
            return [functools.reduce(lambda a, b: a + b, [pv[u][g] for u in range(len(js))]) for g in G]

        n_main = i // ATTN_UNROLL
        left = i - n_main * ATTN_UNROLL
        for r in range(ATTN_UNROLL):
            @pl.when(left == r)
            def _(r=r):
                first = tiles([i] + [n_main * ATTN_UNROLL + u for u in range(r)], [True] + [False] * r)
                for g in G:
                    acc_ref[g] = first[g]

        def main_body(t, carry):
            pv = tiles([t * ATTN_UNROLL + u for u in range(ATTN_UNROLL)], [False] * ATTN_UNROLL)
            for g in G:
                acc_ref[g] = acc_ref[g] + pv[g]
            return carry

        lax.fori_loop(0, n_main, main_body, 0)
        for g in G:
            acc = acc_ref[g]
            write_out(g, acc[0:LANES], acc[LANES:LANES + 1])

    @pl.when(fast_ref[hg, i] == 0)
    def _():
        for g in G:
            s = jnp.where(causal, scores(g, i, False), NEG_INF)
            m = jnp.max(s, axis=0, keepdims=True)
            p = jnp.exp2(s - m)
            m_ref[g] = m
            l_ref[g] = jnp.sum(p, axis=0, keepdims=True)
            acc_ref[g] = _dot(vt_ref[g, i], p.astype(BF16))

        def body(j, carry):
            s = [scores(g, j, False) for g in G]
            pb, alpha = [], []
            for g in G:
                m = m_ref[g]
                m_new = jnp.maximum(m, jnp.max(s[g], axis=0, keepdims=True))
                a = jnp.exp2(m - m_new)
                p = jnp.exp2(s[g] - m_new)
                m_ref[g] = m_new
                l_ref[g] = a * l_ref[g] + jnp.sum(p, axis=0, keepdims=True)
                pb.append(p.astype(BF16))
                alpha.append(a)
            pv = [_dot(vt_ref[g, j], pb[g]) for g in G]
            for g in G:
                acc_ref[g] = alpha[g] * acc_ref[g] + pv[g]
            return carry

        lax.fori_loop(0, i, body, 0)
        for g in G:
            write_out(g, acc_ref[g][0:LANES], l_ref[g])


def _moba_attn(fast, qs, kr, vt, bias, shift, proj_main):
    H, T, _ = qs.shape
    nb = T // MOBA_BLOCK
    BS = MOBA_BLOCK
    G = ATTN_HEADS
    NR = bias.shape[2]
    resident = pl.Buffered(1)
    grid_spec = pltpu.PrefetchScalarGridSpec(
        num_scalar_prefetch=1,
        grid=(H // G, nb),
        in_specs=[
            pl.BlockSpec((G, BS, LANES), lambda h, i, f: (h, i, 0)),
            pl.BlockSpec((G, nb, BS, LANES), lambda h, i, f: (h, 0, 0, 0), pipeline_mode=resident),
            pl.BlockSpec((G, nb, V_AUG_ROWS, BS), lambda h, i, f: (h, 0, 0, 0), pipeline_mode=resident),
            pl.BlockSpec((G, 1, NR, BS), lambda h, i, f: (h, i, 0, 0)),
            pl.BlockSpec((G, 1, 8, BS), lambda h, i, f: (h, i, 0, 0)),
            pl.BlockSpec((BS, G * LANES), lambda h, i, f: (i, GATE_M_COL * (H // G) + h)),
        ],
        out_specs=pl.BlockSpec((BS, G * LANES), lambda h, i, f: (i, h)),
        scratch_shapes=[
            pltpu.VMEM((G, 1, BS), F32), pltpu.VMEM((G, 1, BS), F32), pltpu.VMEM((G, V_AUG_ROWS, BS), F32),
        ],
    )
    return pl.pallas_call(
        _moba_attn_kernel,
        grid_spec=grid_spec,
        out_shape=jax.ShapeDtypeStruct((T, MOBA_WIDTH), BF16),
        compiler_params=pltpu.CompilerParams(
            dimension_semantics=("arbitrary", "arbitrary"), vmem_limit_bytes=VMEM_LIMIT),
        name="moba_attn",
    )(fast, qs, kr, vt, bias, shift, proj_main)


def _out_kernel(mr_ref, mm_ref, x_ref, wt_ref, wb_ref, fw_ref, o_ref):
    h = x_ref[...] + _dot(mr_ref[...], wt_ref[...]) + _dot(mm_ref[...], wb_ref[...])
    ms = jnp.mean(h * h, axis=-1, keepdims=True)
    o_ref[...] = h * lax.rsqrt(ms + RMS_EPS) * fw_ref[...]


def _out_proj(mixed_r, mixed_m, x2, w_out, fw, tm):
    T = x2.shape[0]
    assert T % tm == 0
    return pl.pallas_call(
        _out_kernel,
        grid=(T // tm,),
        in_specs=[
            pl.BlockSpec((tm, RWKV_WIDTH), lambda i: (i, 0)),
            pl.BlockSpec((tm, MOBA_WIDTH), lambda i: (i, 0)),
            pl.BlockSpec((tm, D_MODEL), lambda i: (i, 0)),
            pl.BlockSpec((RWKV_WIDTH, D_MODEL), lambda i: (0, 0)),
            pl.BlockSpec((MOBA_WIDTH, D_MODEL), lambda i: (1, 0)),
            pl.BlockSpec((1, D_MODEL), lambda i: (0, 0)),
        ],
        out_specs=pl.BlockSpec((tm, D_MODEL), lambda i: (i, 0)),
        out_shape=jax.ShapeDtypeStruct((T, D_MODEL), F32),
        compiler_params=pltpu.CompilerParams(
            dimension_semantics=("arbitrary",), vmem_limit_bytes=VMEM_LIMIT),
        name="out_proj",
    )(mixed_r, mixed_m, x2, w_out, w_out, fw)


def kernel(x, norm_w, w_in, mu_rkv, mu_lora, w_decay_up, w_decay_bias, w_iclr_up, w_iclr_bias,
           k_k, k_a, r_k, lnx_w, lnx_b, w_out, final_norm_w):
    B, T, _ = x.shape
    assert B == 1 and T % MOBA_BLOCK == 0
    x2 = x.reshape(T, D_MODEL)

    w_main, w_lora = _weight_relayout(w_in)
    tm_proj = min(1024, T)
    proj_main, proj_lora = _project(x2, norm_w[0:1], w_main, w_lora, tm_proj)

    rows = [mu_rkv[0, 0], mu_rkv[0, 1], mu_rkv[0, 2], w_decay_bias[0], w_iclr_bias[0], k_k[0], k_a[0],
            r_k[0].reshape(-1), lnx_w[0], lnx_b[0]]
    vec_tab = jnp.stack(rows + [jnp.zeros_like(rows[0])] * (N_VEC - len(rows)), axis=0)
    mu_l = mu_lora[0].reshape(1, 2 * LORA_RANK)
    zeros_up = jnp.zeros_like(w_decay_up[0])
    wdec = jnp.concatenate([w_decay_up[0], zeros_up], axis=0)
    wicl = jnp.concatenate([zeros_up, w_iclr_up[0]], axis=0)
    w_lora_split = []
    for w_up in (wdec, wicl):
        w_hi = w_up.astype(BF16)
        w_lora_split += [w_hi, (w_up - w_hi.astype(F32)).astype(BF16)]
    mixed_r = _rwkv(proj_main, proj_lora, vec_tab, mu_l, w_lora_split)

    lane = jnp.arange(LANES)
    inv_freq = ROPE_THETA ** (-(lane % ROT_HALF).astype(F32) / ROT_HALF)
    qs, kr, vt, bias, shift, margin = _moba_prep(proj_main, inv_freq.reshape(1, LANES))
    ok = (margin[:, :, 0] <= FAST_WINDOW).reshape(-1, MOBA_HEADS // ATTN_HEADS, ATTN_HEADS)
    fast = jnp.all(ok, axis=-1).T.astype(jnp.int32)
    mixed_m = _moba_attn(fast, qs, kr, vt, bias, shift, proj_main)

    tm_out = min(512, T)
    out = _out_proj(mixed_r, mixed_m, x2, w_out[0].astype(BF16), final_norm_w.reshape(1, D_MODEL), tm_out)
    return out.reshape(B, T, D_MODEL)
```

```python
import functools

import jax
import jax.numpy as jnp
from jax import lax
from jax.experimental import pallas as pl
from jax.experimental.pallas import tpu as pltpu

F32 = jnp.float32
BF16 = jnp.bfloat16

D_MODEL = 2048
RWKV_WIDTH = 1024
RWKV_HEAD_DIM = 64
LORA_RANK = 64
MOBA_WIDTH = 1024
MOBA_HEAD_DIM = 128
MOBA_HEADS = 8
ROT_HALF = 16
ROT_PAIR_LANE = 64
ROPE_THETA = 500000.0
MOBA_BLOCK = 256
MOBA_TOPK = 3
RMS_EPS = 1e-6
LNX_EPS = 64e-5
NEG_INF = -1e30

LANES = 128
N_PAIRS = RWKV_WIDTH // LANES
CHUNK = 64
RWKV_CHUNKS_PER_STEP = 4
RECUR_SIDE_SCHEDULE = (0, 0, 0, 2, 3, 3, 3, 3, 2, 1, 0, 0, 0)
N_MAIN_BLOCKS = 8
R_COL, K_COL, V_COL, GATE_R_COL, Q_M_COL, K_M_COL, V_M_COL, GATE_M_COL = range(N_MAIN_BLOCKS)
COL_BLOCK = 1024
PROJ_SUB_ROWS = 256
WPREP_ROWS = 256
VMEM_LIMIT = 56 * 1024 * 1024
ATTN_HEADS = 4
ATTN_UNROLL = 16
LOG2E = 1.4426950408889634
V_AUG_ROWS = 144
SHIFT_HEADROOM = 20.0
FAST_WINDOW = 120.0


def _dot(a, b, precision=None):
    return jnp.dot(a, b, preferred_element_type=F32, precision=precision)


def _dot_nt(a, b, precision=None):
    return lax.dot_general(a, b, (((1,), (1,)), ((), ())), preferred_element_type=F32,
                           precision=precision)


def _silu(g):
    return g / (1.0 + jnp.exp(-g))


def _wprep_kernel(w_ref, om_ref, ol_ref):
    lora0 = 3 * RWKV_WIDTH
    lora1 = lora0 + 2 * LORA_RANK
    skip = lora1 - lora0
    qk0, qk1 = Q_M_COL * COL_BLOCK, V_M_COL * COL_BLOCK
    om_ref[:, 0:lora0] = w_ref[0, :, 0:lora0].astype(BF16)
    om_ref[:, lora0:qk0] = w_ref[0, :, lora1:qk0 + skip].astype(BF16)
    om_ref[:, qk1:] = w_ref[0, :, qk1 + skip:].astype(BF16)
    ol_ref[...] = w_ref[0, :, lora0:lora1].astype(BF16)
    lane = lax.broadcasted_iota(jnp.int32, (w_ref.shape[1], LANES), 1)
    up = (lane >= ROT_HALF) & (lane < 2 * ROT_HALF)
    down = (lane >= ROT_PAIR_LANE) & (lane < ROT_PAIR_LANE + ROT_HALF)
    for c0 in range(qk0, qk1, LANES):
        x = w_ref[0, :, c0 + skip:c0 + skip + LANES]
        moved = jnp.where(up, pltpu.roll(x, LANES - (ROT_PAIR_LANE - ROT_HALF), axis=1),
                          jnp.where(down, pltpu.roll(x, ROT_PAIR_LANE - ROT_HALF, axis=1), x))
        om_ref[:, c0:c0 + LANES] = moved.astype(BF16)


def _weight_relayout(w_in):
    d_in = w_in.shape[2]
    n_main = N_MAIN_BLOCKS * COL_BLOCK
    assert d_in == n_main + 2 * LORA_RANK
    return pl.pallas_call(
        _wprep_kernel,
        grid=(D_MODEL // WPREP_ROWS,),
        in_specs=[pl.BlockSpec((1, WPREP_ROWS, d_in), lambda i: (0, i, 0))],
        out_specs=[pl.BlockSpec((WPREP_ROWS, n_main), lambda i: (i, 0)),
                   pl.BlockSpec((WPREP_ROWS, 2 * LORA_RANK), lambda i: (i, 0))],
        out_shape=[jax.ShapeDtypeStruct((D_MODEL, n_main), BF16),
                   jax.ShapeDtypeStruct((D_MODEL, 2 * LORA_RANK), BF16)],
        compiler_params=pltpu.CompilerParams(
            dimension_semantics=("arbitrary",), vmem_limit_bytes=VMEM_LIMIT),
        name="wprep",
    )(w_in)


def _proj_kernel(x_ref, nw_ref, wm_ref, wl_ref, om_ref, ol_ref, xn_ref):
    n = pl.program_id(1)

    @pl.when(n == 0)
    def _():
        for r0 in range(0, x_ref.shape[0], PROJ_SUB_ROWS):
            rows = slice(r0, r0 + PROJ_SUB_ROWS)
            x = x_ref[rows, :]
            ms = jnp.mean(x * x, axis=-1, keepdims=True)
            xn = (x * lax.rsqrt(ms + RMS_EPS) * nw_ref[...]).astype(BF16)
            xn_ref[rows, :] = xn
            ol_ref[rows, :] = _dot(xn, wl_ref[...])

    om_ref[...] = _dot(xn_ref[...], wm_ref[...])


def _project(x2, norm_w, w_main, w_lora, tm):
    T = x2.shape[0]
    assert T % tm == 0 and tm % PROJ_SUB_ROWS == 0
    return pl.pallas_call(
        _proj_kernel,
        grid=(T // tm, N_MAIN_BLOCKS),
        in_specs=[
            pl.BlockSpec((tm, D_MODEL), lambda i, n: (i, 0)),
            pl.BlockSpec((1, D_MODEL), lambda i, n: (0, 0)),
            pl.BlockSpec((D_MODEL, COL_BLOCK), lambda i, n: (0, n)),
            pl.BlockSpec((D_MODEL, LANES), lambda i, n: (0, 0)),
        ],
        out_specs=[
            pl.BlockSpec((tm, COL_BLOCK), lambda i, n: (i, n)),
            pl.BlockSpec((tm, LANES), lambda i, n: (i, 0)),
        ],
        out_shape=[
            jax.ShapeDtypeStruct((T, N_MAIN_BLOCKS * COL_BLOCK), F32),
            jax.ShapeDtypeStruct((T, LANES), F32),
        ],
        scratch_shapes=[pltpu.VMEM((tm, D_MODEL), BF16)],
        compiler_params=pltpu.CompilerParams(
            dimension_semantics=("arbitrary", "arbitrary"), vmem_limit_bytes=VMEM_LIMIT),
        name="proj",
    )(x2, norm_w, w_main, w_lora)


_V_MU_R, _V_MU_K, _V_MU_V, _V_DEC_B, _V_ICL_B, _V_KK, _V_KA, _V_RK, _V_LNW, _V_LNB = range(10)
N_VEC = 16


def _shift_rows(z, prev_row):
    rolled = pltpu.roll(z, 1, axis=0)
    row = lax.broadcasted_iota(jnp.int32, z.shape, 0)
    return jnp.where(row == 0, prev_row, rolled)


def _split3(x):
    hi = x.astype(BF16)
    rem = x - hi.astype(F32)
    mid = rem.astype(BF16)
    return hi, mid, (rem - mid.astype(F32)).astype(BF16)


def _interleave(main, side, side_per_main):
    for n_side in side_per_main:
        next(main, None)
        for _ in range(n_side):
            next(side, None)
    for _ in main:
        pass
    for _ in side:
        pass


def _rwkv_kernel(r_ref, k_ref, v_ref, g_ref, lo_ref, vec_ref, mul_ref, wdh_ref, wdl_ref, wih_ref, wil_ref,
                 o_ref, s_ref, prev_ref, prevl_ref, lw_ref, a_ref, tr_ref, tk_ref, t2_ref, ss_ref, sm_ref,
                 *slots):
    C = CHUNK
    NC = RWKV_CHUNKS_PER_STEP
    TB = NC * C
    n_slot = len(slots) // 2
    slots = (slots[:n_slot], slots[n_slot:])

    @pl.when(pl.program_id(0) == 0)
    def _():
        s_ref[...] = jnp.zeros_like(s_ref)
        prev_ref[...] = jnp.zeros_like(prev_ref)
        prevl_ref[...] = jnp.zeros_like(prevl_ref)

    def vec(i, sl=slice(None)):
        return vec_ref[i:i + 1, sl]

    lo = lo_ref[...]
    lop = _shift_rows(lo, prevl_ref[0:1, :])
    lo = lo + mul_ref[...] * (lop - lo)
    lane_t = lax.broadcasted_iota(jnp.int32, (TB, LANES), 1)
    lmix = jnp.where(lane_t < LORA_RANK, jnp.tanh(lo), lo)
    l_hi = lmix.astype(BF16)
    l_lo = (lmix - l_hi.astype(F32)).astype(BF16)

    def dot3(w_hi_ref, w_lo_ref):
        w_hi = w_hi_ref[...]
        return _dot(l_hi, w_hi) + (_dot(l_lo, w_hi) + _dot(l_hi, w_lo_ref[...]))

    z = -(vec(_V_DEC_B) + dot3(wdh_ref, wdl_ref))
    softplus = jnp.maximum(z, 0.0) + jnp.log(1.0 + jnp.exp(-jnp.abs(z)))
    lw_ref[...] = -jnp.exp(-softplus - 0.5)
    a_ref[...] = 1.0 / (1.0 + jnp.exp(-(vec(_V_ICL_B) + dot3(wih_ref, wil_ref))))

    lane = lax.broadcasted_iota(jnp.int32, (C, LANES), 1)
    ti = lax.broadcasted_iota(jnp.int32, (C, C), 0)
    si = lax.broadcasted_iota(jnp.int32, (C, C), 1)
    tril = jnp.where(si <= ti, 1.0, 0.0).astype(BF16)
    tril3 = jnp.concatenate([tril, tril, tril], axis=1)
    gi = lax.broadcasted_iota(jnp.int32, (LANES, LANES), 0)
    gj = lax.broadcasted_iota(jnp.int32, (LANES, LANES), 1)
    same_head = (gi < RWKV_HEAD_DIM) == (gj < RWKV_HEAD_DIM)
    ones_bd = jnp.where(same_head, 1.0, 0.0).astype(BF16)
    lo_half = lane < RWKV_HEAD_DIM
    t_idx = lax.broadcasted_iota(jnp.int32, (C, LANES), 0)
    s_idx = lane & (RWKV_HEAD_DIM - 1)
    strict = s_idx < t_idx
    incl = s_idx <= t_idx
    zero = jnp.zeros((C, LANES), F32)

    def bd(m):
        zeros = jnp.zeros_like(m)
        return jnp.concatenate([jnp.where(lo_half, m, zeros), jnp.where(lo_half, zeros, m)], axis=0)

    def prepare(c):
        xa_ref, yb_ref, yk_ref, vb_ref, vf_ref, wb_ref, pc_ref, bon_ref = slots[c % 2]
        rows = slice(c * C, (c + 1) * C)
        cols = [slice(LANES * p, LANES * (p + 1)) for p in range(N_PAIRS)]
        for p in range(N_PAIRS):
            sl = cols[p]

            def shifted(ref, slot, mu):
                zc = ref[rows, sl]
                before = prev_ref[slot:slot + 1, sl] if c == 0 else ref[c * C - 1:c * C, sl]
                return zc + mu * (_shift_rows(zc, before) - zc)

            r = shifted(r_ref, 0, vec(_V_MU_R, sl))
            k = shifted(k_ref, 1, vec(_V_MU_K, sl))
            v = shifted(v_ref, 2, vec(_V_MU_V, sl))
            kk = k * vec(_V_KK, sl)
            k2 = k * (1.0 + (a_ref[rows, sl] - 1.0) * vec(_V_KA, sl))
            tr_ref[p] = r
            tk_ref[p] = kk
            t2_ref[p] = k2
            vf_ref[p] = v
            vb_ref[p] = v.astype(BF16)
            ss_ref[p] = (kk * kk).astype(BF16)
            ss_ref[N_PAIRS + p] = (r * k2 * vec(_V_RK, sl)).astype(BF16)
            yield
        sums = _dot(ss_ref[...].reshape(2 * N_PAIRS * C, LANES), ones_bd).reshape(2 * N_PAIRS, C, LANES)
        for p in range(2 * N_PAIRS):
            sm_ref[p] = sums[p]
        yield
        for p in range(N_PAIRS):
            sl = cols[p]
            lw = lw_ref[rows, sl]
            cl = _dot(tril3, jnp.concatenate(_split3(lw), axis=0))
            r, k2, v = tr_ref[p], t2_ref[p], vf_ref[p]
            kk = tk_ref[p] / jnp.maximum(jnp.sqrt(sm_ref[p]), 1e-12)
            b = kk * a_ref[rows, sl]
            cl_end = cl[C - 1:C, :]
            p_inv = jnp.exp(-cl)
            p_tail = jnp.exp(cl_end - cl)
            xa_ref[p, 0:C, :] = (-kk * jnp.exp(cl - lw)).astype(BF16)
            xa_ref[p, C:2 * C, :] = (r * jnp.exp(cl)).astype(BF16)
            yb_ref[p] = (b * p_inv).astype(BF16)
            yk_ref[p] = (k2 * p_inv).astype(BF16)
            wb_ref[p, 0:C, :] = (b * p_tail).astype(BF16)
            wb_ref[p, C:2 * C, :] = (k2 * p_tail).astype(BF16)
            pc_ref[p] = jnp.broadcast_to(jnp.exp(cl_end), (8, LANES))
            bon_ref[p] = sm_ref[N_PAIRS + p] * v
            yield

    def recur(c):
        xa_ref, yb_ref, yk_ref, vb_ref, vf_ref, wb_ref, pc_ref, bon_ref = slots[c % 2]
        rows = slice(c * C, (c + 1) * C)
        P = range(N_PAIRS)
        x = [xa_ref[p] for p in P]
        sc = [_dot_nt(x[p], jnp.concatenate([bd(yb_ref[p]), bd(yk_ref[p])], axis=0)) for p in P]
        yield
        st = [s_ref[p].T.astype(BF16) for p in P]
        a_ab = [jnp.where(strict, sc[p][0:C, 0:LANES], zero).astype(BF16) for p in P]
        b_rb = [jnp.where(incl, sc[p][C:2 * C, 0:LANES], zero).astype(BF16) for p in P]
        akrk = [jnp.concatenate([jnp.where(strict, sc[p][0:C, LANES:2 * LANES], zero),
                                 jnp.where(incl, sc[p][C:2 * C, LANES:2 * LANES], zero)],
                                axis=0).astype(BF16) for p in P]
        yield
        xsav = [_dot(jnp.concatenate([x[p], akrk[p]], axis=1),
                     jnp.concatenate([st[p], bd(vb_ref[p])], axis=0)) for p in P]
        yield
        u = [xsav[p][0:C] for p in P]
        ap = a_ab
        n_steps = 6
        for step in range(n_steps - 1):
            both = [_dot(ap[p], jnp.concatenate([bd(u[p].astype(BF16)), bd(ap[p])], axis=1)) for p in P]
            u = [u[p] + both[p][:, 0:LANES] for p in P]
            ap = [both[p][:, LANES:2 * LANES].astype(BF16) for p in P]
            yield
        u = [u[p] + _dot(ap[p], bd(u[p].astype(BF16))) for p in P]
        yield
        y = [xsav[p][C:2 * C] + _dot(b_rb[p], bd(u[p].astype(BF16))) for p in P]
        yield
        uvt = [jnp.concatenate([u[p], vf_ref[p]], axis=0).T.astype(BF16) for p in P]
        zmat = [_dot(uvt[p], wb_ref[p]) for p in P]
        for p in P:
            s_ref[p] = s_ref[p] * pc_ref[p][0:1, :] + jnp.where(same_head, zmat[p], jnp.zeros_like(zmat[p]))
        yield
        inv_n = 1.0 / RWKV_HEAD_DIM

        def head_mean(ts):
            flat = jnp.concatenate([t.astype(BF16) for t in ts], axis=0)
            m = _dot(flat, ones_bd) * inv_n
            return [m[C * p:C * (p + 1)] for p in P]

        mean = head_mean(y)
        d = [y[p] - mean[p] for p in P]
        yield
        var = head_mean([d[p] * d[p] for p in P])
        for p in P:
            sl = slice(LANES * p, LANES * (p + 1))
            yn = d[p] * lax.rsqrt(var[p] + LNX_EPS) * vec(_V_LNW, sl) + vec(_V_LNB, sl)
            o_ref[rows, sl] = ((yn + bon_ref[p]) * _silu(g_ref[rows, sl])).astype(BF16)
        yield

    for _ in prepare(0):
        pass
    for c in range(NC):
        side = prepare(c + 1) if c + 1 < NC else iter(())
        _interleave(recur(c), side, RECUR_SIDE_SCHEDULE)

    prev_ref[0:1, :] = r_ref[TB - 1:TB, :]
    prev_ref[1:2, :] = k_ref[TB - 1:TB, :]
    prev_ref[2:3, :] = v_ref[TB - 1:TB, :]
    prevl_ref[0:1, :] = lo_ref[TB - 1:TB, :]


def _rwkv(proj_main, proj_lora, vec_tab, mu_l, w_lora_split):
    T = proj_main.shape[0]
    C = CHUNK
    TB = RWKV_CHUNKS_PER_STEP * C
    assert T % TB == 0
    pair_bf = lambda rows: pltpu.VMEM((N_PAIRS, rows, LANES), BF16)
    pair_f32 = lambda rows: pltpu.VMEM((N_PAIRS, rows, LANES), F32)
    slot = [pair_bf(2 * C), pair_bf(C), pair_bf(C), pair_bf(C), pair_f32(C), pair_bf(2 * C),
            pair_f32(8), pair_f32(C)]
    col = lambda j: pl.BlockSpec((TB, COL_BLOCK), lambda c: (c, j))
    whole = lambda shape: pl.BlockSpec(shape, lambda c: (0,) * len(shape))
    return pl.pallas_call(
        _rwkv_kernel,
        grid=(T // TB,),
        in_specs=[col(R_COL), col(K_COL), col(V_COL), col(GATE_R_COL),
                  pl.BlockSpec((TB, LANES), lambda c: (c, 0)),
                  whole((N_VEC, RWKV_WIDTH)), whole((1, LANES))]
                 + [whole((LANES, RWKV_WIDTH))] * 4,
        out_specs=pl.BlockSpec((TB, RWKV_WIDTH), lambda c: (c, 0)),
        out_shape=jax.ShapeDtypeStruct((T, RWKV_WIDTH), BF16),
        scratch_shapes=[
            pltpu.VMEM((N_PAIRS, LANES, LANES), F32),
            pltpu.VMEM((8, RWKV_WIDTH), F32),
            pltpu.VMEM((8, LANES), F32),
            pltpu.VMEM((TB, RWKV_WIDTH), F32),
            pltpu.VMEM((TB, RWKV_WIDTH), F32),
            pair_f32(C), pair_f32(C), pair_f32(C),
            pltpu.VMEM((2 * N_PAIRS, C, LANES), BF16),
            pltpu.VMEM((2 * N_PAIRS, C, LANES), F32),
        ] + slot + slot,
        compiler_params=pltpu.CompilerParams(
            dimension_semantics=("arbitrary",), vmem_limit_bytes=VMEM_LIMIT),
        name="rwkv",
    )(proj_main, proj_main, proj_main, proj_main, proj_lora, vec_tab, mu_l, *w_lora_split)


def _moba_prep_kernel(q_ref, k_ref, v_ref, invf_ref, qs_ref, kr_ref, vt_ref, bias_ref, shift_ref, margin_ref,
                      km_ref, kmax_ref, cs_ref):
    b = pl.program_id(0)
    BS = MOBA_BLOCK
    NR = bias_ref.shape[2]
    lane = lax.broadcasted_iota(jnp.int32, (BS, LANES), 1)

    @pl.when(b == 0)
    def _():
        km_ref[...] = jnp.zeros_like(km_ref)
        kmax_ref[...] = jnp.zeros_like(kmax_ref)
        local = lax.broadcasted_iota(jnp.int32, (BS, LANES), 0).astype(F32) * invf_ref[...]
        cs_ref[0] = jnp.cos(local)
        cs_ref[1] = jnp.sin(local)

    base = (b * BS).astype(F32) * invf_ref[...]
    cb, sb = jnp.cos(base), jnp.sin(base)
    cos = cb * cs_ref[0] - sb * cs_ref[1]
    sin = sb * cs_ref[0] + cb * cs_ref[1]
    first = lane < ROT_HALF
    second = (lane >= ROT_PAIR_LANE) & (lane < ROT_PAIR_LANE + ROT_HALF)
    cos_f = jnp.where(first | second, cos, 1.0)
    sin_t = jnp.where(first, sin, jnp.where(second, -sin, 0.0))

    def rope(x):
        return x * cos_f + pltpu.roll(x * sin_t, ROT_PAIR_LANE, axis=1)

    jidx = lax.broadcasted_iota(jnp.int32, (NR, BS), 0)
    past = jidx < b
    past_bias = jnp.where(past, 0.0, NEG_INF)
    own_bias = jnp.where(jidx == b, 0.0, NEG_INF)
    ones8 = jnp.ones((8, LANES), BF16)
    scale = MOBA_HEAD_DIM ** -0.5
    ones_rows = jnp.where(lax.broadcasted_iota(jnp.int32, (V_AUG_ROWS - LANES, BS), 0) == 0, 1.0, 0.0)
    for h in range(MOBA_HEADS):
        sl = slice(LANES * h, LANES * (h + 1))
        qh = rope(q_ref[:, sl])
        kh = rope(k_ref[:, sl])
        qb = (qh * (scale * LOG2E)).astype(BF16)
        kb = kh.astype(BF16)
        qs_ref[h] = qb
        kr_ref[h, 0] = kb
        vt_ref[h, 0] = jnp.concatenate([v_ref[:, sl].T, ones_rows], axis=0).astype(BF16)
        gate = _dot_nt(km_ref[h], qh, lax.Precision.HIGHEST)
        gate = jnp.where(past, gate, NEG_INF)
        bias = jnp.full((NR, BS), NEG_INF, F32)
        for _ in range(MOBA_TOPK):
            m = jnp.max(gate, axis=0, keepdims=True)
            first = jnp.min(jnp.where(gate == m, jidx, NR), axis=0, keepdims=True)
            hit = jidx == first
            bias = jnp.where(hit, past_bias, bias)
            gate = jnp.where(hit, -3e38, gate)
        bias_ref[h, 0] = jnp.maximum(bias, own_bias)
        km_ref[h, pl.ds(b, 1), :] = jnp.mean(kh, axis=0, keepdims=True)
        qf = qb.astype(F32)
        kf = kb.astype(F32)
        qn = jnp.sqrt(_dot_nt(ones8, (qf * qf).astype(BF16)))
        kn2 = jnp.max(_dot_nt(ones8, (kf * kf).astype(BF16)), axis=1, keepdims=True)
        kmax = jnp.maximum(kmax_ref[h], jnp.sqrt(kn2))
        kmax_ref[h] = kmax
        bound = qn * kmax[:, 0:1] * 1.02 + 1e-3
        shift_ref[h, 0] = bound - SHIFT_HEADROOM
        own = _dot_nt(ones8, (qf * kf).astype(BF16))
        margin_ref[0, h:h + 1, :] = jnp.broadcast_to(jnp.max(bound - own, axis=1, keepdims=True)[0:1], (1, LANES))


def _moba_prep(proj_main, invf):
    T = proj_main.shape[0]
    nb = T // MOBA_BLOCK
    NR = -(-nb // 8) * 8
    BS = MOBA_BLOCK
    H = MOBA_HEADS
    return pl.pallas_call(
        _moba_prep_kernel,
        grid=(nb,),
        in_specs=[
            pl.BlockSpec((BS, COL_BLOCK), lambda b: (b, Q_M_COL)),
            pl.BlockSpec((BS, COL_BLOCK), lambda b: (b, K_M_COL)),
            pl.BlockSpec((BS, COL_BLOCK), lambda b: (b, V_M_COL)),
            pl.BlockSpec((1, LANES), lambda b: (0, 0)),
        ],
        out_specs=[
            pl.BlockSpec((H, BS, LANES), lambda b: (0, b, 0)),
            pl.BlockSpec((H, 1, BS, LANES), lambda b: (0, b, 0, 0)),
            pl.BlockSpec((H, 1, V_AUG_ROWS, BS), lambda b: (0, b, 0, 0)),
            pl.BlockSpec((H, 1, NR, BS), lambda b: (0, b, 0, 0)),
            pl.BlockSpec((H, 1, 8, BS), lambda b: (0, b, 0, 0)),
            pl.BlockSpec((1, H, LANES), lambda b: (b, 0, 0)),
        ],
        out_shape=[
            jax.ShapeDtypeStruct((H, T, LANES), BF16),
            jax.ShapeDtypeStruct((H, nb, BS, LANES), BF16),
            jax.ShapeDtypeStruct((H, nb, V_AUG_ROWS, BS), BF16),
            jax.ShapeDtypeStruct((H, nb, NR, BS), F32),
            jax.ShapeDtypeStruct((H, nb, 8, BS), F32),
            jax.ShapeDtypeStruct((nb, H, LANES), F32),
        ],
        scratch_shapes=[pltpu.VMEM((H, NR, LANES), F32), pltpu.VMEM((H, 8, LANES), F32),
                        pltpu.VMEM((2, BS, LANES), F32)],
        compiler_params=pltpu.CompilerParams(
            dimension_semantics=("arbitrary",), vmem_limit_bytes=VMEM_LIMIT),
        name="moba_prep",
    )(proj_main, proj_main, proj_main, invf)


def _moba_attn_kernel(fast_ref, q_ref, k_ref, vt_ref, bias_ref, shift_ref, g_ref, o_ref, m_ref, l_ref, acc_ref):
    hg = pl.program_id(0)
    i = pl.program_id(1)
    BS = MOBA_BLOCK
    G = range(ATTN_HEADS)
    kidx = lax.broadcasted_iota(jnp.int32, (BS, BS), 0)
    qidx = lax.broadcasted_iota(jnp.int32, (BS, BS), 1)
    causal = kidx <= qidx

    def scores(g, j, shifted):
        row = bias_ref[g, 0, pl.ds(j, 1), :]
        if shifted:
            row = row - shift_ref[g, 0, 0:1, :]
        return _dot_nt(k_ref[g, j], q_ref[g]) + row

    def write_out(g, acc, denom):
        sl = slice(LANES * g, LANES * (g + 1))
        o_ref[:, sl] = ((acc / denom).T * _silu(g_ref[:, sl])).astype(BF16)

    @pl.when(fast_ref[hg, i] != 0)
    def _():
        def tiles(js, causal_flags):
            s = [[scores(g, j, True) for g in G] for j in js]
            s = [[jnp.where(causal, sg, NEG_INF) if flag else sg for sg in sj]
                 for sj, flag in zip(s, causal_flags)]
            p = [[jnp.exp2(sg).astype(BF16) for sg in sj] for sj in s]
            pv = [[_dot(vt_ref[g, j], p[u][g]) for g in G] for u, j in enumerate(js)]
            return [functools.reduce(lambda a, b: a + b, [pv[u][g] for u in range(len(js))]) for g in G]

        n_main = i // ATTN_UNROLL
        left = i - n_main * ATTN_UNROLL
        for r in range(ATTN_UNROLL):
            @pl.when(left == r)
            def _(r=r):
                first = tiles([i] + [n_main * ATTN_UNROLL + u for u in range(r)], [True] + [False] * r)
                for g in G:
                    acc_ref[g] = first[g]

        def main_body(t, carry):
            pv = tiles([t * ATTN_UNROLL + u for u in range(ATTN_UNROLL)], [False] * ATTN_UNROLL)
            for g in G:
                acc_ref[g] = acc_ref[g] + pv[g]
            return carry

        lax.fori_loop(0, n_main, main_body, 0)
        for g in G:
            acc = acc_ref[g]
            write_out(g, acc[0:LANES], acc[LANES:LANES + 1])

    @pl.when(fast_ref[hg, i] == 0)
    def _():
        for g in G:
            s = jnp.where(causal, scores(g, i, False), NEG_INF)
            m = jnp.max(s, axis=0, keepdims=True)
            p = jnp.exp2(s - m)
            m_ref[g] = m
            l_ref[g] = jnp.sum(p, axis=0, keepdims=True)
            acc_ref[g] = _dot(vt_ref[g, i], p.astype(BF16))

        def body(j, carry):
            s = [scores(g, j, False) for g in G]
            pb, alpha = [], []
            for g in G:
                m = m_ref[g]
                m_new = jnp.maximum(m, jnp.max(s[g], axis=0, keepdims=True))
                a = jnp.exp2(m - m_new)
                p = jnp.exp2(s[g] - m_new)
                m_ref[g] = m_new
                l_ref[g] = a * l_ref[g] + jnp.sum(p, axis=0, keepdims=True)
                pb.append(p.astype(BF16))
                alpha.append(a)
            pv = [_dot(vt_ref[g, j], pb[g]) for g in G]
            for g in G:
                acc_ref[g] = alpha[g] * acc_ref[g] + pv[g]
            return carry

        lax.fori_loop(0, i, body, 0)
        for g in G:
            write_out(g, acc_ref[g][0:LANES], l_ref[g])


def _moba_attn(fast, qs, kr, vt, bias, shift, proj_main):
    H, T, _ = qs.shape
    nb = T // MOBA_BLOCK
    BS = MOBA_BLOCK
    G = ATTN_HEADS
    NR = bias.shape[2]
    resident = pl.Buffered(1)
    grid_spec = pltpu.PrefetchScalarGridSpec(
        num_scalar_prefetch=1,
        grid=(H // G, nb),
        in_specs=[
            pl.BlockSpec((G, BS, LANES), lambda h, i, f: (h, i, 0)),
            pl.BlockSpec((G, nb, BS, LANES), lambda h, i, f: (h, 0, 0, 0), pipeline_mode=resident),
            pl.BlockSpec((G, nb, V_AUG_ROWS, BS), lambda h, i, f: (h, 0, 0, 0), pipeline_mode=resident),
            pl.BlockSpec((G, 1, NR, BS), lambda h, i, f: (h, i, 0, 0)),
            pl.BlockSpec((G, 1, 8, BS), lambda h, i, f: (h, i, 0, 0)),
            pl.BlockSpec((BS, G * LANES), lambda h, i, f: (i, GATE_M_COL * (H // G) + h)),
        ],
        out_specs=pl.BlockSpec((BS, G * LANES), lambda h, i, f: (i, h)),
        scratch_shapes=[
            pltpu.VMEM((G, 1, BS), F32), pltpu.VMEM((G, 1, BS), F32), pltpu.VMEM((G, V_AUG_ROWS, BS), F32),
        ],
    )
    return pl.pallas_call(
        _moba_attn_kernel,
        grid_spec=grid_spec,
        out_shape=jax.ShapeDtypeStruct((T, MOBA_WIDTH), BF16),
        compiler_params=pltpu.CompilerParams(
            dimension_semantics=("arbitrary", "arbitrary"), vmem_limit_bytes=VMEM_LIMIT),
        name="moba_attn",
    )(fast, qs, kr, vt, bias, shift, proj_main)


def _out_kernel(mr_ref, mm_ref, x_ref, wt_ref, wb_ref, fw_ref, o_ref):
    h = x_ref[...] + _dot(mr_ref[...], wt_ref[...]) + _dot(mm_ref[...], wb_ref[...])
    ms = jnp.mean(h * h, axis=-1, keepdims=True)
    o_ref[...] = h * lax.rsqrt(ms + RMS_EPS) * fw_ref[...]


def _out_proj(mixed_r, mixed_m, x2, w_out, fw, tm):
    T = x2.shape[0]
    assert T % tm == 0
    return pl.pallas_call(
        _out_kernel,
        grid=(T // tm,),
        in_specs=[
            pl.BlockSpec((tm, RWKV_WIDTH), lambda i: (i, 0)),
            pl.BlockSpec((tm, MOBA_WIDTH), lambda i: (i, 0)),
            pl.BlockSpec((tm, D_MODEL), lambda i: (i, 0)),
            pl.BlockSpec((RWKV_WIDTH, D_MODEL), lambda i: (0, 0)),
            pl.BlockSpec((MOBA_WIDTH, D_MODEL), lambda i: (1, 0)),
            pl.BlockSpec((1, D_MODEL), lambda i: (0, 0)),
        ],
        out_specs=pl.BlockSpec((tm, D_MODEL), lambda i: (i, 0)),
        out_shape=jax.ShapeDtypeStruct((T, D_MODEL), F32),
        compiler_params=pltpu.CompilerParams(
            dimension_semantics=("arbitrary",), vmem_limit_bytes=VMEM_LIMIT),
        name="out_proj",
    )(mixed_r, mixed_m, x2, w_out, w_out, fw)


def kernel(x, norm_w, w_in, mu_rkv, mu_lora, w_decay_up, w_decay_bias, w_iclr_up, w_iclr_bias,
           k_k, k_a, r_k, lnx_w, lnx_b, w_out, final_norm_w):
    B, T, _ = x.shape
    assert B == 1 and T % MOBA_BLOCK == 0
    x2 = x.reshape(T, D_MODEL)

    w_main, w_lora = _weight_relayout(w_in)
    tm_proj = min(1024, T)
    proj_main, proj_lora = _project(x2, norm_w[0:1], w_main, w_lora, tm_proj)

    rows = [mu_rkv[0, 0], mu_rkv[0, 1], mu_rkv[0, 2], w_decay_bias[0], w_iclr_bias[0], k_k[0], k_a[0],
            r_k[0].reshape(-1), lnx_w[0], lnx_b[0]]
    vec_tab = jnp.stack(rows + [jnp.zeros_like(rows[0])] * (N_VEC - len(rows)), axis=0)
    mu_l = mu_lora[0].reshape(1, 2 * LORA_RANK)
    zeros_up = jnp.zeros_like(w_decay_up[0])
    wdec = jnp.concatenate([w_decay_up[0], zeros_up], axis=0)
    wicl = jnp.concatenate([zeros_up, w_iclr_up[0]], axis=0)
    w_lora_split = []
    for w_up in (wdec, wicl):
        w_hi = w_up.astype(BF16)
        w_lora_split += [w_hi, (w_up - w_hi.astype(F32)).astype(BF16)]
    mixed_r = _rwkv(proj_main, proj_lora, vec_tab, mu_l, w_lora_split)

    lane = jnp.arange(LANES)
    inv_freq = ROPE_THETA ** (-(lane % ROT_HALF).astype(F32) / ROT_HALF)
    qs, kr, vt, bias, shift, margin = _moba_prep(proj_main, inv_freq.reshape(1, LANES))
    ok = (margin[:, :, 0] <= FAST_WINDOW).reshape(-1, MOBA_HEADS // ATTN_HEADS, ATTN_HEADS)
    fast = jnp.all(ok, axis=-1).T.astype(jnp.int32)
    mixed_m = _moba_attn(fast, qs, kr, vt, bias, shift, proj_main)

    tm_out = min(512, T)
    out = _out_proj(mixed_r, mixed_m, x2, w_out[0].astype(BF16), final_norm_w.reshape(1, D_MODEL), tm_out)
    return out.reshape(B, T, D_MODEL)
```

```python
import functools

import jax
import jax.numpy as jnp
from jax import lax
from jax.experimental import pallas as pl
from jax.experimental.pallas import tpu as pltpu

F32 = jnp.float32
BF16 = jnp.bfloat16

D_MODEL = 2048
RWKV_WIDTH = 1024
RWKV_HEAD_DIM = 64
LORA_RANK = 64
MOBA_WIDTH = 1024
MOBA_HEAD_DIM = 128
MOBA_HEADS = 8
ROT_HALF = 16
ROT_PAIR_LANE = 64
ROPE_THETA = 500000.0
MOBA_BLOCK = 256
MOBA_TOPK = 3
RMS_EPS = 1e-6
LNX_EPS = 64e-5
NEG_INF = -1e30

LANES = 128
N_PAIRS = RWKV_WIDTH // LANES
CHUNK = 64
RWKV_CHUNKS_PER_STEP = 4
RECUR_SIDE_SCHEDULE = (0, 0, 0, 2, 2, 2, 2, 2, 2, 2, 1, 1, 1, 0, 0, 0, 0, 0)
N_MAIN_BLOCKS = 8
R_COL, K_COL, V_COL, GATE_R_COL, Q_M_COL, K_M_COL, V_M_COL, GATE_M_COL = range(N_MAIN_BLOCKS)
COL_BLOCK = 1024
PROJ_SUB_ROWS = 256
WPREP_ROWS = 256
VMEM_LIMIT = 56 * 1024 * 1024
ATTN_HEADS = 4
ATTN_UNROLL = 16
LOG2E = 1.4426950408889634
V_AUG_ROWS = 144
SHIFT_HEADROOM = 20.0
FAST_WINDOW = 120.0


def _dot(a, b, precision=None):
    return jnp.dot(a, b, preferred_element_type=F32, precision=precision)


def _dot_nt(a, b, precision=None):
    return lax.dot_general(a, b, (((1,), (1,)), ((), ())), preferred_element_type=F32,
                           precision=precision)


def _silu(g):
    return g / (1.0 + jnp.exp(-g))


def _wprep_kernel(w_ref, om_ref, ol_ref):
    lora0 = 3 * RWKV_WIDTH
    lora1 = lora0 + 2 * LORA_RANK
    skip = lora1 - lora0
    qk0, qk1 = Q_M_COL * COL_BLOCK, V_M_COL * COL_BLOCK
    om_ref[:, 0:lora0] = w_ref[0, :, 0:lora0].astype(BF16)
    om_ref[:, lora0:qk0] = w_ref[0, :, lora1:qk0 + skip].astype(BF16)
    om_ref[:, qk1:] = w_ref[0, :, qk1 + skip:].astype(BF16)
    ol_ref[...] = w_ref[0, :, lora0:lora1].astype(BF16)
    lane = lax.broadcasted_iota(jnp.int32, (w_ref.shape[1], LANES), 1)
    up = (lane >= ROT_HALF) & (lane < 2 * ROT_HALF)
    down = (lane >= ROT_PAIR_LANE) & (lane < ROT_PAIR_LANE + ROT_HALF)
    for c0 in range(qk0, qk1, LANES):
        x = w_ref[0, :, c0 + skip:c0 + skip + LANES]
        moved = jnp.where(up, pltpu.roll(x, LANES - (ROT_PAIR_LANE - ROT_HALF), axis=1),
                          jnp.where(down, pltpu.roll(x, ROT_PAIR_LANE - ROT_HALF, axis=1), x))
        om_ref[:, c0:c0 + LANES] = moved.astype(BF16)


def _weight_relayout(w_in):
    d_in = w_in.shape[2]
    n_main = N_MAIN_BLOCKS * COL_BLOCK
    assert d_in == n_main + 2 * LORA_RANK
    return pl.pallas_call(
        _wprep_kernel,
        grid=(D_MODEL // WPREP_ROWS,),
        in_specs=[pl.BlockSpec((1, WPREP_ROWS, d_in), lambda i: (0, i, 0))],
        out_specs=[pl.BlockSpec((WPREP_ROWS, n_main), lambda i: (i, 0)),
                   pl.BlockSpec((WPREP_ROWS, 2 * LORA_RANK), lambda i: (i, 0))],
        out_shape=[jax.ShapeDtypeStruct((D_MODEL, n_main), BF16),
                   jax.ShapeDtypeStruct((D_MODEL, 2 * LORA_RANK), BF16)],
        compiler_params=pltpu.CompilerParams(
            dimension_semantics=("arbitrary",), vmem_limit_bytes=VMEM_LIMIT),
        name="wprep",
    )(w_in)


def _proj_kernel(x_ref, nw_ref, wm_ref, wl_ref, om_ref, ol_ref, xn_ref):
    n = pl.program_id(1)

    @pl.when(n == 0)
    def _():
        for r0 in range(0, x_ref.shape[0], PROJ_SUB_ROWS):
            rows = slice(r0, r0 + PROJ_SUB_ROWS)
            x = x_ref[rows, :]
            ms = jnp.mean(x * x, axis=-1, keepdims=True)
            xn = (x * lax.rsqrt(ms + RMS_EPS) * nw_ref[...]).astype(BF16)
            xn_ref[rows, :] = xn
            ol_ref[rows, :] = _dot(xn, wl_ref[...])

    om_ref[...] = _dot(xn_ref[...], wm_ref[...])


def _project(x2, norm_w, w_main, w_lora, tm):
    T = x2.shape[0]
    assert T % tm == 0 and tm % PROJ_SUB_ROWS == 0
    return pl.pallas_call(
        _proj_kernel,
        grid=(T // tm, N_MAIN_BLOCKS),
        in_specs=[
            pl.BlockSpec((tm, D_MODEL), lambda i, n: (i, 0)),
            pl.BlockSpec((1, D_MODEL), lambda i, n: (0, 0)),
            pl.BlockSpec((D_MODEL, COL_BLOCK), lambda i, n: (0, n)),
            pl.BlockSpec((D_MODEL, LANES), lambda i, n: (0, 0)),
        ],
        out_specs=[
            pl.BlockSpec((tm, COL_BLOCK), lambda i, n: (i, n)),
            pl.BlockSpec((tm, LANES), lambda i, n: (i, 0)),
        ],
        out_shape=[
            jax.ShapeDtypeStruct((T, N_MAIN_BLOCKS * COL_BLOCK), F32),
            jax.ShapeDtypeStruct((T, LANES), F32),
        ],
        scratch_shapes=[pltpu.VMEM((tm, D_MODEL), BF16)],
        compiler_params=pltpu.CompilerParams(
            dimension_semantics=("arbitrary", "arbitrary"), vmem_limit_bytes=VMEM_LIMIT),
        name="proj",
    )(x2, norm_w, w_main, w_lora)


_V_MU_R, _V_MU_K, _V_MU_V, _V_DEC_B, _V_ICL_B, _V_KK, _V_KA, _V_RK, _V_LNW, _V_LNB = range(10)
N_VEC = 16


def _shift_rows(z, prev_row):
    rolled = pltpu.roll(z, 1, axis=0)
    row = lax.broadcasted_iota(jnp.int32, z.shape, 0)
    return jnp.where(row == 0, prev_row, rolled)


def _split3(x):
    hi = x.astype(BF16)
    rem = x - hi.astype(F32)
    mid = rem.astype(BF16)
    return hi, mid, (rem - mid.astype(F32)).astype(BF16)


def _interleave(main, side, side_per_main):
    for n_side in side_per_main:
        next(main, None)
        for _ in range(n_side):
            next(side, None)
    for _ in main:
        pass
    for _ in side:
        pass


def _rwkv_kernel(r_ref, k_ref, v_ref, g_ref, lo_ref, vec_ref, mul_ref, wdh_ref, wdl_ref, wih_ref, wil_ref,
                 o_ref, s_ref, prev_ref, prevl_ref, lw_ref, a_ref, tr_ref, tk_ref, t2_ref, ss_ref, sm_ref,
                 *slots):
    C = CHUNK
    NC = RWKV_CHUNKS_PER_STEP
    TB = NC * C
    n_slot = len(slots) // 2
    slots = (slots[:n_slot], slots[n_slot:])

    @pl.when(pl.program_id(0) == 0)
    def _():
        s_ref[...] = jnp.zeros_like(s_ref)
        prev_ref[...] = jnp.zeros_like(prev_ref)
        prevl_ref[...] = jnp.zeros_like(prevl_ref)

    def vec(i, sl=slice(None)):
        return vec_ref[i:i + 1, sl]

    lo = lo_ref[...]
    lop = _shift_rows(lo, prevl_ref[0:1, :])
    lo = lo + mul_ref[...] * (lop - lo)
    lane_t = lax.broadcasted_iota(jnp.int32, (TB, LANES), 1)
    lmix = jnp.where(lane_t < LORA_RANK, jnp.tanh(lo), lo)
    l_hi = lmix.astype(BF16)
    l_lo = (lmix - l_hi.astype(F32)).astype(BF16)

    def dot3(w_hi_ref, w_lo_ref):
        w_hi = w_hi_ref[...]
        return _dot(l_hi, w_hi) + (_dot(l_lo, w_hi) + _dot(l_hi, w_lo_ref[...]))

    z = -(vec(_V_DEC_B) + dot3(wdh_ref, wdl_ref))
    softplus = jnp.maximum(z, 0.0) + jnp.log(1.0 + jnp.exp(-jnp.abs(z)))
    lw_ref[...] = -jnp.exp(-softplus - 0.5)
    a_ref[...] = 1.0 / (1.0 + jnp.exp(-(vec(_V_ICL_B) + dot3(wih_ref, wil_ref))))

    lane = lax.broadcasted_iota(jnp.int32, (C, LANES), 1)
    ti = lax.broadcasted_iota(jnp.int32, (C, C), 0)
    si = lax.broadcasted_iota(jnp.int32, (C, C), 1)
    tril = jnp.where(si <= ti, 1.0, 0.0).astype(BF16)
    tril3 = jnp.concatenate([tril, tril, tril], axis=1)
    gi = lax.broadcasted_iota(jnp.int32, (LANES, LANES), 0)
    gj = lax.broadcasted_iota(jnp.int32, (LANES, LANES), 1)
    same_head = (gi < RWKV_HEAD_DIM) == (gj < RWKV_HEAD_DIM)
    ones_bd = jnp.where(same_head, 1.0, 0.0).astype(BF16)
    lo_half = lane < RWKV_HEAD_DIM
    t_idx = lax.broadcasted_iota(jnp.int32, (C, LANES), 0)
    s_idx = lane & (RWKV_HEAD_DIM - 1)
    strict = s_idx < t_idx
    incl = s_idx <= t_idx
    zero = jnp.zeros((C, LANES), F32)
    eye_pair = jnp.where(s_idx == t_idx, 1.0, 0.0)
    below = {}
    b = 1
    while b < C:
        below[b] = (((t_idx // b) & 1) == 1) & ((s_idx // (2 * b)) == (t_idx // (2 * b))) & (((s_idx // b) & 1) == 0)
        b *= 2

    def bd(m):
        zeros = jnp.zeros_like(m)
        return jnp.concatenate([jnp.where(lo_half, m, zeros), jnp.where(lo_half, zeros, m)], axis=0)

    def prepare(c):
        xa_ref, yb_ref, yk_ref, vb_ref, vf_ref, wb_ref, pc_ref, bon_ref = slots[c % 2]
        rows = slice(c * C, (c + 1) * C)
        cols = [slice(LANES * p, LANES * (p + 1)) for p in range(N_PAIRS)]
        for p in range(N_PAIRS):
            sl = cols[p]

            def shifted(ref, slot, mu):
                zc = ref[rows, sl]
                before = prev_ref[slot:slot + 1, sl] if c == 0 else ref[c * C - 1:c * C, sl]
                return zc + mu * (_shift_rows(zc, before) - zc)

            r = shifted(r_ref, 0, vec(_V_MU_R, sl))
            k = shifted(k_ref, 1, vec(_V_MU_K, sl))
            v = shifted(v_ref, 2, vec(_V_MU_V, sl))
            kk = k * vec(_V_KK, sl)
            k2 = k * (1.0 + (a_ref[rows, sl] - 1.0) * vec(_V_KA, sl))
            tr_ref[p] = r
            tk_ref[p] = kk
            t2_ref[p] = k2
            vf_ref[p] = v
            vb_ref[p] = v.astype(BF16)
            ss_ref[p] = (kk * kk).astype(BF16)
            ss_ref[N_PAIRS + p] = (r * k2 * vec(_V_RK, sl)).astype(BF16)
            yield
        sums = _dot(ss_ref[...].reshape(2 * N_PAIRS * C, LANES), ones_bd).reshape(2 * N_PAIRS, C, LANES)
        for p in range(2 * N_PAIRS):
            sm_ref[p] = sums[p]
        yield
        for p in range(N_PAIRS):
            sl = cols[p]
            lw = lw_ref[rows, sl]
            cl = _dot(tril3, jnp.concatenate(_split3(lw), axis=0))
            r, k2, v = tr_ref[p], t2_ref[p], vf_ref[p]
            kk = tk_ref[p] / jnp.maximum(jnp.sqrt(sm_ref[p]), 1e-12)
            b = kk * a_ref[rows, sl]
            cl_end = cl[C - 1:C, :]
            p_inv = jnp.exp(-cl)
            p_tail = jnp.exp(cl_end - cl)
            xa_ref[p, 0:C, :] = (-kk * jnp.exp(cl - lw)).astype(BF16)
            xa_ref[p, C:2 * C, :] = (r * jnp.exp(cl)).astype(BF16)
            yb_ref[p] = (b * p_inv).astype(BF16)
            yk_ref[p] = (k2 * p_inv).astype(BF16)
            wb_ref[p, 0:C, :] = (b * p_tail).astype(BF16)
            wb_ref[p, C:2 * C, :] = (k2 * p_tail).astype(BF16)
            pc_ref[p] = jnp.broadcast_to(jnp.exp(cl_end), (8, LANES))
            bon_ref[p] = sm_ref[N_PAIRS + p] * v
            yield

    def recur(c):
        xa_ref, yb_ref, yk_ref, vb_ref, vf_ref, wb_ref, pc_ref, bon_ref = slots[c % 2]
        rows = slice(c * C, (c + 1) * C)
        P = range(N_PAIRS)
        x = [xa_ref[p] for p in P]
        sc = [_dot_nt(x[p], jnp.concatenate([bd(yb_ref[p]), bd(yk_ref[p])], axis=0)) for p in P]
        yield
        st = [s_ref[p].T.astype(BF16) for p in P]
        a_ab = [jnp.where(strict, sc[p][0:C, 0:LANES], zero) for p in P]
        b_rb = [jnp.where(incl, sc[p][C:2 * C, 0:LANES], zero).astype(BF16) for p in P]
        akrk = [jnp.concatenate([jnp.where(strict, sc[p][0:C, LANES:2 * LANES], zero),
                                 jnp.where(incl, sc[p][C:2 * C, LANES:2 * LANES], zero)],
                                axis=0).astype(BF16) for p in P]
        yield
        xsav = [_dot(jnp.concatenate([x[p], akrk[p]], axis=1),
                     jnp.concatenate([st[p], bd(vb_ref[p])], axis=0)) for p in P]
        yield
        t_inv = [eye_pair + jnp.where(below[1], a_ab[p], zero) for p in P]
        b = 2
        while b < C:
            tb = [t_inv[p].astype(BF16) for p in P]
            w = [_dot(tb[p], bd(jnp.where(below[b], a_ab[p], zero).astype(BF16))) for p in P]
            yield
            t_inv = [t_inv[p] + _dot(w[p].astype(BF16), bd(tb[p])) for p in P]
            yield
            b *= 2
        u = [_dot(t_inv[p].astype(BF16), bd(xsav[p][0:C].astype(BF16))) for p in P]
        yield
        y = [xsav[p][C:2 * C] + _dot(b_rb[p], bd(u[p].astype(BF16))) for p in P]
        yield
        uvt = [jnp.concatenate([u[p], vf_ref[p]], axis=0).T.astype(BF16) for p in P]
        zmat = [_dot(uvt[p], wb_ref[p]) for p in P]
        for p in P:
            s_ref[p] = s_ref[p] * pc_ref[p][0:1, :] + jnp.where(same_head, zmat[p], jnp.zeros_like(zmat[p]))
        yield
        inv_n = 1.0 / RWKV_HEAD_DIM

        def head_mean(ts):
            flat = jnp.concatenate([t.astype(BF16) for t in ts], axis=0)
            m = _dot(flat, ones_bd) * inv_n
            return [m[C * p:C * (p + 1)] for p in P]

        mean = head_mean(y)
        d = [y[p] - mean[p] for p in P]
        yield
        var = head_mean([d[p] * d[p] for p in P])
        for p in P:
            sl = slice(LANES * p, LANES * (p + 1))
            yn = d[p] * lax.rsqrt(var[p] + LNX_EPS) * vec(_V_LNW, sl) + vec(_V_LNB, sl)
            o_ref[rows, sl] = ((yn + bon_ref[p]) * _silu(g_ref[rows, sl])).astype(BF16)
        yield

    for _ in prepare(0):
        pass
    for c in range(NC):
        side = prepare(c + 1) if c + 1 < NC else iter(())
        _interleave(recur(c), side, RECUR_SIDE_SCHEDULE)

    prev_ref[0:1, :] = r_ref[TB - 1:TB, :]
    prev_ref[1:2, :] = k_ref[TB - 1:TB, :]
    prev_ref[2:3, :] = v_ref[TB - 1:TB, :]
    prevl_ref[0:1, :] = lo_ref[TB - 1:TB, :]


def _rwkv(proj_main, proj_lora, vec_tab, mu_l, w_lora_split):
    T = proj_main.shape[0]
    C = CHUNK
    TB = RWKV_CHUNKS_PER_STEP * C
    assert T % TB == 0
    pair_bf = lambda rows: pltpu.VMEM((N_PAIRS, rows, LANES), BF16)
    pair_f32 = lambda rows: pltpu.VMEM((N_PAIRS, rows, LANES), F32)
    slot = [pair_bf(2 * C), pair_bf(C), pair_bf(C), pair_bf(C), pair_f32(C), pair_bf(2 * C),
            pair_f32(8), pair_f32(C)]
    col = lambda j: pl.BlockSpec((TB, COL_BLOCK), lambda c: (c, j))
    whole = lambda shape: pl.BlockSpec(shape, lambda c: (0,) * len(shape))
    return pl.pallas_call(
        _rwkv_kernel,
        grid=(T // TB,),
        in_specs=[col(R_COL), col(K_COL), col(V_COL), col(GATE_R_COL),
                  pl.BlockSpec((TB, LANES), lambda c: (c, 0)),
                  whole((N_VEC, RWKV_WIDTH)), whole((1, LANES))]
                 + [whole((LANES, RWKV_WIDTH))] * 4,
        out_specs=pl.BlockSpec((TB, RWKV_WIDTH), lambda c: (c, 0)),
        out_shape=jax.ShapeDtypeStruct((T, RWKV_WIDTH), BF16),
        scratch_shapes=[
            pltpu.VMEM((N_PAIRS, LANES, LANES), F32),
            pltpu.VMEM((8, RWKV_WIDTH), F32),
            pltpu.VMEM((8, LANES), F32),
            pltpu.VMEM((TB, RWKV_WIDTH), F32),
            pltpu.VMEM((TB, RWKV_WIDTH), F32),
            pair_f32(C), pair_f32(C), pair_f32(C),
            pltpu.VMEM((2 * N_PAIRS, C, LANES), BF16),
            pltpu.VMEM((2 * N_PAIRS, C, LANES), F32),
        ] + slot + slot,
        compiler_params=pltpu.CompilerParams(
            dimension_semantics=("arbitrary",), vmem_limit_bytes=VMEM_LIMIT),
        name="rwkv",
    )(proj_main, proj_main, proj_main, proj_main, proj_lora, vec_tab, mu_l, *w_lora_split)


def _moba_prep_kernel(q_ref, k_ref, v_ref, invf_ref, qs_ref, kr_ref, vt_ref, bias_ref, shift_ref, margin_ref,
                      km_ref, kmax_ref, cs_ref):
    b = pl.program_id(0)
    BS = MOBA_BLOCK
    NR = bias_ref.shape[2]
    lane = lax.broadcasted_iota(jnp.int32, (BS, LANES), 1)

    @pl.when(b == 0)
    def _():
        km_ref[...] = jnp.zeros_like(km_ref)
        kmax_ref[...] = jnp.zeros_like(kmax_ref)
        local = lax.broadcasted_iota(jnp.int32, (BS, LANES), 0).astype(F32) * invf_ref[...]
        cs_ref[0] = jnp.cos(local)
        cs_ref[1] = jnp.sin(local)

    base = (b * BS).astype(F32) * invf_ref[...]
    cb, sb = jnp.cos(base), jnp.sin(base)
    cos = cb * cs_ref[0] - sb * cs_ref[1]
    sin = sb * cs_ref[0] + cb * cs_ref[1]
    first = lane < ROT_HALF
    second = (lane >= ROT_PAIR_LANE) & (lane < ROT_PAIR_LANE + ROT_HALF)
    cos_f = jnp.where(first | second, cos, 1.0)
    sin_t = jnp.where(first, sin, jnp.where(second, -sin, 0.0))

    def rope(x):
        return x * cos_f + pltpu.roll(x * sin_t, ROT_PAIR_LANE, axis=1)

    jidx = lax.broadcasted_iota(jnp.int32, (NR, BS), 0)
    past = jidx < b
    past_bias = jnp.where(past, 0.0, NEG_INF)
    own_bias = jnp.where(jidx == b, 0.0, NEG_INF)
    ones8 = jnp.ones((8, LANES), BF16)
    scale = MOBA_HEAD_DIM ** -0.5
    ones_rows = jnp.where(lax.broadcasted_iota(jnp.int32, (V_AUG_ROWS - LANES, BS), 0) == 0, 1.0, 0.0)
    for h in range(MOBA_HEADS):
        sl = slice(LANES * h, LANES * (h + 1))
        qh = rope(q_ref[:, sl])
        kh = rope(k_ref[:, sl])
        qb = (qh * (scale * LOG2E)).astype(BF16)
        kb = kh.astype(BF16)
        qs_ref[h] = qb
        kr_ref[h, 0] = kb
        vt_ref[h, 0] = jnp.concatenate([v_ref[:, sl].T, ones_rows], axis=0).astype(BF16)
        gate = _dot_nt(km_ref[h], qh, lax.Precision.HIGHEST)
        gate = jnp.where(past, gate, NEG_INF)
        bias = jnp.full((NR, BS), NEG_INF, F32)
        for _ in range(MOBA_TOPK):
            m = jnp.max(gate, axis=0, keepdims=True)
            first = jnp.min(jnp.where(gate == m, jidx, NR), axis=0, keepdims=True)
            hit = jidx == first
            bias = jnp.where(hit, past_bias, bias)
            gate = jnp.where(hit, -3e38, gate)
        bias_ref[h, 0] = jnp.maximum(bias, own_bias)
        km_ref[h, pl.ds(b, 1), :] = jnp.mean(kh, axis=0, keepdims=True)
        qf = qb.astype(F32)
        kf = kb.astype(F32)
        qn = jnp.sqrt(_dot_nt(ones8, (qf * qf).astype(BF16)))
        kn2 = jnp.max(_dot_nt(ones8, (kf * kf).astype(BF16)), axis=1, keepdims=True)
        kmax = jnp.maximum(kmax_ref[h], jnp.sqrt(kn2))
        kmax_ref[h] = kmax
        bound = qn * kmax[:, 0:1] * 1.02 + 1e-3
        shift_ref[h, 0] = bound - SHIFT_HEADROOM
        own = _dot_nt(ones8, (qf * kf).astype(BF16))
        margin_ref[0, h:h + 1, :] = jnp.broadcast_to(jnp.max(bound - own, axis=1, keepdims=True)[0:1], (1, LANES))


def _moba_prep(proj_main, invf):
    T = proj_main.shape[0]
    nb = T // MOBA_BLOCK
    NR = -(-nb // 8) * 8
    BS = MOBA_BLOCK
    H = MOBA_HEADS
    return pl.pallas_call(
        _moba_prep_kernel,
        grid=(nb,),
        in_specs=[
            pl.BlockSpec((BS, COL_BLOCK), lambda b: (b, Q_M_COL)),
            pl.BlockSpec((BS, COL_BLOCK), lambda b: (b, K_M_COL)),
            pl.BlockSpec((BS, COL_BLOCK), lambda b: (b, V_M_COL)),
            pl.BlockSpec((1, LANES), lambda b: (0, 0)),
        ],
        out_specs=[
            pl.BlockSpec((H, BS, LANES), lambda b: (0, b, 0)),
            pl.BlockSpec((H, 1, BS, LANES), lambda b: (0, b, 0, 0)),
            pl.BlockSpec((H, 1, V_AUG_ROWS, BS), lambda b: (0, b, 0, 0)),
            pl.BlockSpec((H, 1, NR, BS), lambda b: (0, b, 0, 0)),
            pl.BlockSpec((H, 1, 8, BS), lambda b: (0, b, 0, 0)),
            pl.BlockSpec((1, H, LANES), lambda b: (b, 0, 0)),
        ],
        out_shape=[
            jax.ShapeDtypeStruct((H, T, LANES), BF16),
            jax.ShapeDtypeStruct((H, nb, BS, LANES), BF16),
            jax.ShapeDtypeStruct((H, nb, V_AUG_ROWS, BS), BF16),
            jax.ShapeDtypeStruct((H, nb, NR, BS), F32),
            jax.ShapeDtypeStruct((H, nb, 8, BS), F32),
            jax.ShapeDtypeStruct((nb, H, LANES), F32),
        ],
        scratch_shapes=[pltpu.VMEM((H, NR, LANES), F32), pltpu.VMEM((H, 8, LANES), F32),
                        pltpu.VMEM((2, BS, LANES), F32)],
        compiler_params=pltpu.CompilerParams(
            dimension_semantics=("arbitrary",), vmem_limit_bytes=VMEM_LIMIT),
        name="moba_prep",
    )(proj_main, proj_main, proj_main, invf)


def _moba_attn_kernel(fast_ref, q_ref, k_ref, vt_ref, bias_ref, shift_ref, g_ref, o_ref, m_ref, l_ref, acc_ref):
    hg = pl.program_id(0)
    i = pl.program_id(1)
    BS = MOBA_BLOCK
    G = range(ATTN_HEADS)
    kidx = lax.broadcasted_iota(jnp.int32, (BS, BS), 0)
    qidx = lax.broadcasted_iota(jnp.int32, (BS, BS), 1)
    causal = kidx <= qidx

    def scores(g, j, shifted):
        row = bias_ref[g, 0, pl.ds(j, 1), :]
        if shifted:
            row = row - shift_ref[g, 0, 0:1, :]
        return _dot_nt(k_ref[g, j], q_ref[g]) + row

    def write_out(g, acc, denom):
        sl = slice(LANES * g, LANES * (g + 1))
        o_ref[:, sl] = ((acc / denom).T * _silu(g_ref[:, sl])).astype(BF16)

    @pl.when(fast_ref[hg, i] != 0)
    def _():
        def tiles(js, causal_flags):
            s = [[scores(g, j, True) for g in G] for j in js]
            s = [[jnp.where(causal, sg, NEG_INF) if flag else sg for sg in sj]
                 for sj, flag in zip(s, causal_flags)]
            p = [[jnp.exp2(sg).astype(BF16) for sg in sj] for sj in s]
            pv = [[_dot(vt_ref[g, j], p[u][g]) for g in G] for u, j in enumerate(js)]
            return [functools.reduce(lambda a, b: a + b, [pv[u][g] for u in range(len(js))]) for g in G]

        n_main = i // ATTN_UNROLL
        left = i - n_main * ATTN_UNROLL
        for r in range(ATTN_UNROLL):
            @pl.when(left == r)
            def _(r=r):
                first = tiles([i] + [n_main * ATTN_UNROLL + u for u in range(r)], [True] + [False] * r)
                for g in G:
                    acc_ref[g] = first[g]

        def main_body(t, carry):
            pv = tiles([t * ATTN_UNROLL + u for u in range(ATTN_UNROLL)], [False] * ATTN_UNROLL)
            for g in G:
                acc_ref[g] = acc_ref[g] + pv[g]
            return carry

        lax.fori_loop(0, n_main, main_body, 0)
        for g in G:
            acc = acc_ref[g]
            write_out(g, acc[0:LANES], acc[LANES:LANES + 1])

    @pl.when(fast_ref[hg, i] == 0)
    def _():
        for g in G:
            s = jnp.where(causal, scores(g, i, False), NEG_INF)
            m = jnp.max(s, axis=0, keepdims=True)
            p = jnp.exp2(s - m)
            m_ref[g] = m
            l_ref[g] = jnp.sum(p, axis=0, keepdims=True)
            acc_ref[g] = _dot(vt_ref[g, i], p.astype(BF16))

        def body(j, carry):
            s = [scores(g, j, False) for g in G]
            pb, alpha = [], []
            for g in G:
                m = m_ref[g]
                m_new = jnp.maximum(m, jnp.max(s[g], axis=0, keepdims=True))
                a = jnp.exp2(m - m_new)
                p = jnp.exp2(s[g] - m_new)
                m_ref[g] = m_new
                l_ref[g] = a * l_ref[g] + jnp.sum(p, axis=0, keepdims=True)
                pb.append(p.astype(BF16))
                alpha.append(a)
            pv = [_dot(vt_ref[g, j], pb[g]) for g in G]
            for g in G:
                acc_ref[g] = alpha[g] * acc_ref[g] + pv[g]
            return carry

        lax.fori_loop(0, i, body, 0)
        for g in G:
            write_out(g, acc_ref[g][0:LANES], l_ref[g])


def _moba_attn(fast, qs, kr, vt, bias, shift, proj_main):
    H, T, _ = qs.shape
    nb = T // MOBA_BLOCK
    BS = MOBA_BLOCK
    G = ATTN_HEADS
    NR = bias.shape[2]
    resident = pl.Buffered(1)
    grid_spec = pltpu.PrefetchScalarGridSpec(
        num_scalar_prefetch=1,
        grid=(H // G, nb),
        in_specs=[
            pl.BlockSpec((G, BS, LANES), lambda h, i, f: (h, i, 0)),
            pl.BlockSpec((G, nb, BS, LANES), lambda h, i, f: (h, 0, 0, 0), pipeline_mode=resident),
            pl.BlockSpec((G, nb, V_AUG_ROWS, BS), lambda h, i, f: (h, 0, 0, 0), pipeline_mode=resident),
            pl.BlockSpec((G, 1, NR, BS), lambda h, i, f: (h, i, 0, 0)),
            pl.BlockSpec((G, 1, 8, BS), lambda h, i, f: (h, i, 0, 0)),
            pl.BlockSpec((BS, G * LANES), lambda h, i, f: (i, GATE_M_COL * (H // G) + h)),
        ],
        out_specs=pl.BlockSpec((BS, G * LANES), lambda h, i, f: (i, h)),
        scratch_shapes=[
            pltpu.VMEM((G, 1, BS), F32), pltpu.VMEM((G, 1, BS), F32), pltpu.VMEM((G, V_AUG_ROWS, BS), F32),
        ],
    )
    return pl.pallas_call(
        _moba_attn_kernel,
        grid_spec=grid_spec,
        out_shape=jax.ShapeDtypeStruct((T, MOBA_WIDTH), BF16),
        compiler_params=pltpu.CompilerParams(
            dimension_semantics=("arbitrary", "arbitrary"), vmem_limit_bytes=VMEM_LIMIT),
        name="moba_attn",
    )(fast, qs, kr, vt, bias, shift, proj_main)


def _out_kernel(mr_ref, mm_ref, x_ref, wt_ref, wb_ref, fw_ref, o_ref):
    h = x_ref[...] + _dot(mr_ref[...], wt_ref[...]) + _dot(mm_ref[...], wb_ref[...])
    ms = jnp.mean(h * h, axis=-1, keepdims=True)
    o_ref[...] = h * lax.rsqrt(ms + RMS_EPS) * fw_ref[...]


def _out_proj(mixed_r, mixed_m, x2, w_out, fw, tm):
    T = x2.shape[0]
    assert T % tm == 0
    return pl.pallas_call(
        _out_kernel,
        grid=(T // tm,),
        in_specs=[
            pl.BlockSpec((tm, RWKV_WIDTH), lambda i: (i, 0)),
            pl.BlockSpec((tm, MOBA_WIDTH), lambda i: (i, 0)),
            pl.BlockSpec((tm, D_MODEL), lambda i: (i, 0)),
            pl.BlockSpec((RWKV_WIDTH, D_MODEL), lambda i: (0, 0)),
            pl.BlockSpec((MOBA_WIDTH, D_MODEL), lambda i: (1, 0)),
            pl.BlockSpec((1, D_MODEL), lambda i: (0, 0)),
        ],
        out_specs=pl.BlockSpec((tm, D_MODEL), lambda i: (i, 0)),
        out_shape=jax.ShapeDtypeStruct((T, D_MODEL), F32),
        compiler_params=pltpu.CompilerParams(
            dimension_semantics=("arbitrary",), vmem_limit_bytes=VMEM_LIMIT),
        name="out_proj",
    )(mixed_r, mixed_m, x2, w_out, w_out, fw)


def kernel(x, norm_w, w_in, mu_rkv, mu_lora, w_decay_up, w_decay_bias, w_iclr_up, w_iclr_bias,
           k_k, k_a, r_k, lnx_w, lnx_b, w_out, final_norm_w):
    B, T, _ = x.shape
    assert B == 1 and T % MOBA_BLOCK == 0
    x2 = x.reshape(T, D_MODEL)

    w_main, w_lora = _weight_relayout(w_in)
    tm_proj = min(1024, T)
    proj_main, proj_lora = _project(x2, norm_w[0:1], w_main, w_lora, tm_proj)

    rows = [mu_rkv[0, 0], mu_rkv[0, 1], mu_rkv[0, 2], w_decay_bias[0], w_iclr_bias[0], k_k[0], k_a[0],
            r_k[0].reshape(-1), lnx_w[0], lnx_b[0]]
    vec_tab = jnp.stack(rows + [jnp.zeros_like(rows[0])] * (N_VEC - len(rows)), axis=0)
    mu_l = mu_lora[0].reshape(1, 2 * LORA_RANK)
    zeros_up = jnp.zeros_like(w_decay_up[0])
    wdec = jnp.concatenate([w_decay_up[0], zeros_up], axis=0)
    wicl = jnp.concatenate([zeros_up, w_iclr_up[0]], axis=0)
    w_lora_split = []
    for w_up in (wdec, wicl):
        w_hi = w_up.astype(BF16)
        w_lora_split += [w_hi, (w_up - w_hi.astype(F32)).astype(BF16)]
    mixed_r = _rwkv(proj_main, proj_lora, vec_tab, mu_l, w_lora_split)

    lane = jnp.arange(LANES)
    inv_freq = ROPE_THETA ** (-(lane % ROT_HALF).astype(F32) / ROT_HALF)
    qs, kr, vt, bias, shift, margin = _moba_prep(proj_main, inv_freq.reshape(1, LANES))
    ok = (margin[:, :, 0] <= FAST_WINDOW).reshape(-1, MOBA_HEADS // ATTN_HEADS, ATTN_HEADS)
    fast = jnp.all(ok, axis=-1).T.astype(jnp.int32)
    mixed_m = _moba_attn(fast, qs, kr, vt, bias, shift, proj_main)

    tm_out = min(512, T)
    out = _out_proj(mixed_r, mixed_m, x2, w_out[0].astype(BF16), final_norm_w.reshape(1, D_MODEL), tm_out)
    return out.reshape(B, T, D_MODEL)
```

```python
import functools

import jax
import jax.numpy as jnp
from jax import lax
from jax.experimental import pallas as pl
from jax.experimental.pallas import tpu as pltpu

F32 = jnp.float32
BF16 = jnp.bfloat16

D_MODEL = 2048
RWKV_WIDTH = 1024
RWKV_HEAD_DIM = 64
LORA_RANK = 64
MOBA_WIDTH = 1024
MOBA_HEAD_DIM = 128
MOBA_HEADS = 8
ROT_HALF = 16
ROT_PAIR_LANE = 64
ROPE_THETA = 500000.0
MOBA_BLOCK = 256
MOBA_TOPK = 3
RMS_EPS = 1e-6
LNX_EPS = 64e-5
NEG_INF = -1e30

LANES = 128
N_PAIRS = RWKV_WIDTH // LANES
CHUNK = 64
RWKV_CHUNKS_PER_STEP = 4
N_SLOTS = 3
N_MAIN_BLOCKS = 8
R_COL, K_COL, V_COL, GATE_R_COL, Q_M_COL, K_M_COL, V_M_COL, GATE_M_COL = range(N_MAIN_BLOCKS)
COL_BLOCK = 1024
PROJ_SUB_ROWS = 256
WPREP_ROWS = 256
VMEM_LIMIT = 56 * 1024 * 1024
ATTN_HEADS = 4
ATTN_UNROLL = 16
LOG2E = 1.4426950408889634
V_AUG_ROWS = 144
SHIFT_HEADROOM = 20.0
FAST_WINDOW = 120.0


def _dot(a, b, precision=None):
    return jnp.dot(a, b, preferred_element_type=F32, precision=precision)


def _dot_nt(a, b, precision=None):
    return lax.dot_general(a, b, (((1,), (1,)), ((), ())), preferred_element_type=F32,
                           precision=precision)


def _silu(g):
    return g / (1.0 + jnp.exp(-g))


def _wprep_kernel(w_ref, om_ref, ol_ref):
    lora0 = 3 * RWKV_WIDTH
    lora1 = lora0 + 2 * LORA_RANK
    skip = lora1 - lora0
    qk0, qk1 = Q_M_COL * COL_BLOCK, V_M_COL * COL_BLOCK
    om_ref[:, 0:lora0] = w_ref[0, :, 0:lora0].astype(BF16)
    om_ref[:, lora0:qk0] = w_ref[0, :, lora1:qk0 + skip].astype(BF16)
    om_ref[:, qk1:] = w_ref[0, :, qk1 + skip:].astype(BF16)
    ol_ref[...] = w_ref[0, :, lora0:lora1].astype(BF16)
    lane = lax.broadcasted_iota(jnp.int32, (w_ref.shape[1], LANES), 1)
    up = (lane >= ROT_HALF) & (lane < 2 * ROT_HALF)
    down = (lane >= ROT_PAIR_LANE) & (lane < ROT_PAIR_LANE + ROT_HALF)
    for c0 in range(qk0, qk1, LANES):
        x = w_ref[0, :, c0 + skip:c0 + skip + LANES]
        moved = jnp.where(up, pltpu.roll(x, LANES - (ROT_PAIR_LANE - ROT_HALF), axis=1),
                          jnp.where(down, pltpu.roll(x, ROT_PAIR_LANE - ROT_HALF, axis=1), x))
        om_ref[:, c0:c0 + LANES] = moved.astype(BF16)


def _weight_relayout(w_in):
    d_in = w_in.shape[2]
    n_main = N_MAIN_BLOCKS * COL_BLOCK
    assert d_in == n_main + 2 * LORA_RANK
    return pl.pallas_call(
        _wprep_kernel,
        grid=(D_MODEL // WPREP_ROWS,),
        in_specs=[pl.BlockSpec((1, WPREP_ROWS, d_in), lambda i: (0, i, 0))],
        out_specs=[pl.BlockSpec((WPREP_ROWS, n_main), lambda i: (i, 0)),
                   pl.BlockSpec((WPREP_ROWS, 2 * LORA_RANK), lambda i: (i, 0))],
        out_shape=[jax.ShapeDtypeStruct((D_MODEL, n_main), BF16),
                   jax.ShapeDtypeStruct((D_MODEL, 2 * LORA_RANK), BF16)],
        compiler_params=pltpu.CompilerParams(
            dimension_semantics=("arbitrary",), vmem_limit_bytes=VMEM_LIMIT),
        name="wprep",
    )(w_in)


def _proj_kernel(x_ref, nw_ref, wm_ref, wl_ref, om_ref, ol_ref, xn_ref):
    n = pl.program_id(1)

    @pl.when(n == 0)
    def _():
        for r0 in range(0, x_ref.shape[0], PROJ_SUB_ROWS):
            rows = slice(r0, r0 + PROJ_SUB_ROWS)
            x = x_ref[rows, :]
            ms = jnp.mean(x * x, axis=-1, keepdims=True)
            xn = (x * lax.rsqrt(ms + RMS_EPS) * nw_ref[...]).astype(BF16)
            xn_ref[rows, :] = xn
            ol_ref[rows, :] = _dot(xn, wl_ref[...])

    om_ref[...] = _dot(xn_ref[...], wm_ref[...])


def _project(x2, norm_w, w_main, w_lora, tm):
    T = x2.shape[0]
    assert T % tm == 0 and tm % PROJ_SUB_ROWS == 0
    return pl.pallas_call(
        _proj_kernel,
        grid=(T // tm, N_MAIN_BLOCKS),
        in_specs=[
            pl.BlockSpec((tm, D_MODEL), lambda i, n: (i, 0)),
            pl.BlockSpec((1, D_MODEL), lambda i, n: (0, 0)),
            pl.BlockSpec((D_MODEL, COL_BLOCK), lambda i, n: (0, n)),
            pl.BlockSpec((D_MODEL, LANES), lambda i, n: (0, 0)),
        ],
        out_specs=[
            pl.BlockSpec((tm, COL_BLOCK), lambda i, n: (i, n)),
            pl.BlockSpec((tm, LANES), lambda i, n: (i, 0)),
        ],
        out_shape=[
            jax.ShapeDtypeStruct((T, N_MAIN_BLOCKS * COL_BLOCK), F32),
            jax.ShapeDtypeStruct((T, LANES), F32),
        ],
        scratch_shapes=[pltpu.VMEM((tm, D_MODEL), BF16)],
        compiler_params=pltpu.CompilerParams(
            dimension_semantics=("arbitrary", "arbitrary"), vmem_limit_bytes=VMEM_LIMIT),
        name="proj",
    )(x2, norm_w, w_main, w_lora)


_V_MU_R, _V_MU_K, _V_MU_V, _V_DEC_B, _V_ICL_B, _V_KK, _V_KA, _V_RK, _V_LNW, _V_LNB = range(10)
N_VEC = 16


def _shift_rows(z, prev_row):
    rolled = pltpu.roll(z, 1, axis=0)
    row = lax.broadcasted_iota(jnp.int32, z.shape, 0)
    return jnp.where(row == 0, prev_row, rolled)


def _split3(x):
    hi = x.astype(BF16)
    rem = x - hi.astype(F32)
    mid = rem.astype(BF16)
    return hi, mid, (rem - mid.astype(F32)).astype(BF16)


def _rwkv_kernel(r_ref, k_ref, v_ref, g_ref, lo_ref, vec_ref, mul_ref, wdh_ref, wdl_ref, wih_ref, wil_ref,
                 o_ref, s_ref, prev_ref, prevl_ref, lw_ref, a_ref, tr_ref, tk_ref, t2_ref, ss_ref, sm_ref,
                 *slots):
    C = CHUNK
    NC = RWKV_CHUNKS_PER_STEP
    TB = NC * C
    n_slot = len(slots) // N_SLOTS
    slots = [slots[i * n_slot:(i + 1) * n_slot] for i in range(N_SLOTS)]

    @pl.when(pl.program_id(0) == 0)
    def _():
        s_ref[...] = jnp.zeros_like(s_ref)
        prev_ref[...] = jnp.zeros_like(prev_ref)
        prevl_ref[...] = jnp.zeros_like(prevl_ref)

    def vec(i, sl=slice(None)):
        return vec_ref[i:i + 1, sl]

    lo = lo_ref[...]
    lop = _shift_rows(lo, prevl_ref[0:1, :])
    lo = lo + mul_ref[...] * (lop - lo)
    lane_t = lax.broadcasted_iota(jnp.int32, (TB, LANES), 1)
    lmix = jnp.where(lane_t < LORA_RANK, jnp.tanh(lo), lo)
    l_hi = lmix.astype(BF16)
    l_lo = (lmix - l_hi.astype(F32)).astype(BF16)

    def dot3(w_hi_ref, w_lo_ref):
        w_hi = w_hi_ref[...]
        return _dot(l_hi, w_hi) + (_dot(l_lo, w_hi) + _dot(l_hi, w_lo_ref[...]))

    z = -(vec(_V_DEC_B) + dot3(wdh_ref, wdl_ref))
    softplus = jnp.maximum(z, 0.0) + jnp.log(1.0 + jnp.exp(-jnp.abs(z)))
    lw_ref[...] = -jnp.exp(-softplus - 0.5)
    a_ref[...] = 1.0 / (1.0 + jnp.exp(-(vec(_V_ICL_B) + dot3(wih_ref, wil_ref))))

    lane = lax.broadcasted_iota(jnp.int32, (C, LANES), 1)
    ti = lax.broadcasted_iota(jnp.int32, (C, C), 0)
    si = lax.broadcasted_iota(jnp.int32, (C, C), 1)
    tril = jnp.where(si <= ti, 1.0, 0.0).astype(BF16)
    tril3 = jnp.concatenate([tril, tril, tril], axis=1)
    gi = lax.broadcasted_iota(jnp.int32, (LANES, LANES), 0)
    gj = lax.broadcasted_iota(jnp.int32, (LANES, LANES), 1)
    same_head = (gi < RWKV_HEAD_DIM) == (gj < RWKV_HEAD_DIM)
    ones_bd = jnp.where(same_head, 1.0, 0.0).astype(BF16)
    lo_half = lane < RWKV_HEAD_DIM
    t_idx = lax.broadcasted_iota(jnp.int32, (C, LANES), 0)
    s_idx = lane & (RWKV_HEAD_DIM - 1)
    strict = s_idx < t_idx
    incl = s_idx <= t_idx
    zero = jnp.zeros((C, LANES), F32)
    eye_pair = jnp.where(s_idx == t_idx, 1.0, 0.0)
    below = {}
    b = 1
    while b < C:
        below[b] = (((t_idx // b) & 1) == 1) & ((s_idx // (2 * b)) == (t_idx // (2 * b))) & (((s_idx // b) & 1) == 0)
        b *= 2

    def bd(m):
        zeros = jnp.zeros_like(m)
        return jnp.concatenate([jnp.where(lo_half, m, zeros), jnp.where(lo_half, zeros, m)], axis=0)

    def prepare(c):
        xa_ref, yb_ref, yk_ref, vb_ref, vf_ref, wb_ref, pc_ref, bon_ref = slots[c % N_SLOTS]
        rows = slice(c * C, (c + 1) * C)
        cols = [slice(LANES * p, LANES * (p + 1)) for p in range(N_PAIRS)]
        for p in range(N_PAIRS):
            sl = cols[p]

            def shifted(ref, slot, mu):
                zc = ref[rows, sl]
                before = prev_ref[slot:slot + 1, sl] if c == 0 else ref[c * C - 1:c * C, sl]
                return zc + mu * (_shift_rows(zc, before) - zc)

            r = shifted(r_ref, 0, vec(_V_MU_R, sl))
            k = shifted(k_ref, 1, vec(_V_MU_K, sl))
            v = shifted(v_ref, 2, vec(_V_MU_V, sl))
            kk = k * vec(_V_KK, sl)
            k2 = k * (1.0 + (a_ref[rows, sl] - 1.0) * vec(_V_KA, sl))
            tr_ref[p] = r
            tk_ref[p] = kk
            t2_ref[p] = k2
            vf_ref[p] = v
            vb_ref[p] = v.astype(BF16)
            ss_ref[p] = (kk * kk).astype(BF16)
            ss_ref[N_PAIRS + p] = (r * k2 * vec(_V_RK, sl)).astype(BF16)
            yield
        sums = _dot(ss_ref[...].reshape(2 * N_PAIRS * C, LANES), ones_bd).reshape(2 * N_PAIRS, C, LANES)
        for p in range(2 * N_PAIRS):
            sm_ref[p] = sums[p]
        yield
        for p in range(N_PAIRS):
            sl = cols[p]
            lw = lw_ref[rows, sl]
            cl = _dot(tril3, jnp.concatenate(_split3(lw), axis=0))
            r, k2, v = tr_ref[p], t2_ref[p], vf_ref[p]
            kk = tk_ref[p] / jnp.maximum(jnp.sqrt(sm_ref[p]), 1e-12)
            b = kk * a_ref[rows, sl]
            cl_end = cl[C - 1:C, :]
            p_inv = jnp.exp(-cl)
            p_tail = jnp.exp(cl_end - cl)
            xa_ref[p, 0:C, :] = (-kk * jnp.exp(cl - lw)).astype(BF16)
            xa_ref[p, C:2 * C, :] = (r * jnp.exp(cl)).astype(BF16)
            yb_ref[p] = (b * p_inv).astype(BF16)
            yk_ref[p] = (k2 * p_inv).astype(BF16)
            wb_ref[p, 0:C, :] = (b * p_tail).astype(BF16)
            wb_ref[p, C:2 * C, :] = (k2 * p_tail).astype(BF16)
            pc_ref[p] = jnp.broadcast_to(jnp.exp(cl_end), (8, LANES))
            bon_ref[p] = sm_ref[N_PAIRS + p] * v
            yield

    staged = {}

    def recur_a(c):
        xa_ref, yb_ref, yk_ref = slots[c % N_SLOTS][0:3]
        P = range(N_PAIRS)
        x = [xa_ref[p] for p in P]
        sc = [_dot_nt(x[p], jnp.concatenate([bd(yb_ref[p]), bd(yk_ref[p])], axis=0)) for p in P]
        yield
        a_ab = [jnp.where(strict, sc[p][0:C, 0:LANES], zero) for p in P]
        b_rb = [jnp.where(incl, sc[p][C:2 * C, 0:LANES], zero).astype(BF16) for p in P]
        akrk = [jnp.concatenate([jnp.where(strict, sc[p][0:C, LANES:2 * LANES], zero),
                                 jnp.where(incl, sc[p][C:2 * C, LANES:2 * LANES], zero)],
                                axis=0).astype(BF16) for p in P]
        yield
        t_inv = [eye_pair + jnp.where(below[1], a_ab[p], zero) for p in P]
        b = 2
        while b < C:
            tb = [t_inv[p].astype(BF16) for p in P]
            w = [_dot(tb[p], bd(jnp.where(below[b], a_ab[p], zero).astype(BF16))) for p in P]
            yield
            t_inv = [t_inv[p] + _dot(w[p].astype(BF16), bd(tb[p])) for p in P]
            yield
            b *= 2
        staged[c] = (x, akrk, b_rb, [t_inv[p].astype(BF16) for p in P])

    def recur_b(c):
        vb_ref, vf_ref, wb_ref, pc_ref, bon_ref = slots[c % N_SLOTS][3:8]
        x, akrk, b_rb, t_inv = staged.pop(c)
        rows = slice(c * C, (c + 1) * C)
        P = range(N_PAIRS)
        xsav = [_dot(jnp.concatenate([x[p], akrk[p]], axis=1),
                     jnp.concatenate([s_ref[p].T.astype(BF16), bd(vb_ref[p])], axis=0)) for p in P]
        yield
        u = [_dot(t_inv[p], bd(xsav[p][0:C].astype(BF16))) for p in P]
        yield
        y = [xsav[p][C:2 * C] + _dot(b_rb[p], bd(u[p].astype(BF16))) for p in P]
        uvt = [jnp.concatenate([u[p], vf_ref[p]], axis=0).T.astype(BF16) for p in P]
        zmat = [_dot(uvt[p], wb_ref[p]) for p in P]
        for p in P:
            s_ref[p] = s_ref[p] * pc_ref[p][0:1, :] + jnp.where(same_head, zmat[p], jnp.zeros_like(zmat[p]))
        yield
        inv_n = 1.0 / RWKV_HEAD_DIM

        def head_mean(ts):
            flat = jnp.concatenate([t.astype(BF16) for t in ts], axis=0)
            m = _dot(flat, ones_bd) * inv_n
            return [m[C * p:C * (p + 1)] for p in P]

        mean = head_mean(y)
        d = [y[p] - mean[p] for p in P]
        yield
        var = head_mean([d[p] * d[p] for p in P])
        for p in P:
            sl = slice(LANES * p, LANES * (p + 1))
            yn = d[p] * lax.rsqrt(var[p] + LNX_EPS) * vec(_V_LNW, sl) + vec(_V_LNB, sl)
            o_ref[rows, sl] = ((yn + bon_ref[p]) * _silu(g_ref[rows, sl])).astype(BF16)
        yield

    def together(*gens_and_rates):
        live = [[g, r] for g, r in gens_and_rates]
        while live:
            for item in list(live):
                for _ in range(item[1]):
                    if next(item[0], StopIteration) is StopIteration:
                        live.remove(item)
                        break
            yield

    for _ in prepare(0):
        pass
    for _ in together((recur_a(0), 1), (prepare(1), 2)):
        pass
    for c in range(NC):
        sides = []
        if c + 1 < NC:
            sides.append((recur_a(c + 1), 3))
        if c + 2 < NC:
            sides.append((prepare(c + 2), 4))
        for _ in together((recur_b(c), 1), *sides):
            pass

    prev_ref[0:1, :] = r_ref[TB - 1:TB, :]
    prev_ref[1:2, :] = k_ref[TB - 1:TB, :]
    prev_ref[2:3, :] = v_ref[TB - 1:TB, :]
    prevl_ref[0:1, :] = lo_ref[TB - 1:TB, :]


def _rwkv(proj_main, proj_lora, vec_tab, mu_l, w_lora_split):
    T = proj_main.shape[0]
    C = CHUNK
    TB = RWKV_CHUNKS_PER_STEP * C
    assert T % TB == 0
    pair_bf = lambda rows: pltpu.VMEM((N_PAIRS, rows, LANES), BF16)
    pair_f32 = lambda rows: pltpu.VMEM((N_PAIRS, rows, LANES), F32)
    slot = [pair_bf(2 * C), pair_bf(C), pair_bf(C), pair_bf(C), pair_f32(C), pair_bf(2 * C),
            pair_f32(8), pair_f32(C)]
    col = lambda j: pl.BlockSpec((TB, COL_BLOCK), lambda c: (c, j))
    whole = lambda shape: pl.BlockSpec(shape, lambda c: (0,) * len(shape))
    return pl.pallas_call(
        _rwkv_kernel,
        grid=(T // TB,),
        in_specs=[col(R_COL), col(K_COL), col(V_COL), col(GATE_R_COL),
                  pl.BlockSpec((TB, LANES), lambda c: (c, 0)),
                  whole((N_VEC, RWKV_WIDTH)), whole((1, LANES))]
                 + [whole((LANES, RWKV_WIDTH))] * 4,
        out_specs=pl.BlockSpec((TB, RWKV_WIDTH), lambda c: (c, 0)),
        out_shape=jax.ShapeDtypeStruct((T, RWKV_WIDTH), BF16),
        scratch_shapes=[
            pltpu.VMEM((N_PAIRS, LANES, LANES), F32),
            pltpu.VMEM((8, RWKV_WIDTH), F32),
            pltpu.VMEM((8, LANES), F32),
            pltpu.VMEM((TB, RWKV_WIDTH), F32),
            pltpu.VMEM((TB, RWKV_WIDTH), F32),
            pair_f32(C), pair_f32(C), pair_f32(C),
            pltpu.VMEM((2 * N_PAIRS, C, LANES), BF16),
            pltpu.VMEM((2 * N_PAIRS, C, LANES), F32),
        ] + slot * N_SLOTS,
        compiler_params=pltpu.CompilerParams(
            dimension_semantics=("arbitrary",), vmem_limit_bytes=VMEM_LIMIT),
        name="rwkv",
    )(proj_main, proj_main, proj_main, proj_main, proj_lora, vec_tab, mu_l, *w_lora_split)


def _moba_prep_kernel(q_ref, k_ref, v_ref, invf_ref, qs_ref, kr_ref, vt_ref, bias_ref, shift_ref, margin_ref,
                      km_ref, kmax_ref, cs_ref):
    b = pl.program_id(0)
    BS = MOBA_BLOCK
    NR = bias_ref.shape[2]
    lane = lax.broadcasted_iota(jnp.int32, (BS, LANES), 1)

    @pl.when(b == 0)
    def _():
        km_ref[...] = jnp.zeros_like(km_ref)
        kmax_ref[...] = jnp.zeros_like(kmax_ref)
        local = lax.broadcasted_iota(jnp.int32, (BS, LANES), 0).astype(F32) * invf_ref[...]
        cs_ref[0] = jnp.cos(local)
        cs_ref[1] = jnp.sin(local)

    base = (b * BS).astype(F32) * invf_ref[...]
    cb, sb = jnp.cos(base), jnp.sin(base)
    cos = cb * cs_ref[0] - sb * cs_ref[1]
    sin = sb * cs_ref[0] + cb * cs_ref[1]
    first = lane < ROT_HALF
    second = (lane >= ROT_PAIR_LANE) & (lane < ROT_PAIR_LANE + ROT_HALF)
    cos_f = jnp.where(first | second, cos, 1.0)
    sin_t = jnp.where(first, sin, jnp.where(second, -sin, 0.0))

    def rope(x):
        return x * cos_f + pltpu.roll(x * sin_t, ROT_PAIR_LANE, axis=1)

    jidx = lax.broadcasted_iota(jnp.int32, (NR, BS), 0)
    past = jidx < b
    past_bias = jnp.where(past, 0.0, NEG_INF)
    own_bias = jnp.where(jidx == b, 0.0, NEG_INF)
    ones8 = jnp.ones((8, LANES), BF16)
    scale = MOBA_HEAD_DIM ** -0.5
    ones_rows = jnp.where(lax.broadcasted_iota(jnp.int32, (V_AUG_ROWS - LANES, BS), 0) == 0, 1.0, 0.0)
    for h in range(MOBA_HEADS):
        sl = slice(LANES * h, LANES * (h + 1))
        qh = rope(q_ref[:, sl])
        kh = rope(k_ref[:, sl])
        qb = (qh * (scale * LOG2E)).astype(BF16)
        kb = kh.astype(BF16)
        qs_ref[h] = qb
        kr_ref[h, 0] = kb
        vt_ref[h, 0] = jnp.concatenate([v_ref[:, sl].T, ones_rows], axis=0).astype(BF16)
        gate = _dot_nt(km_ref[h], qh, lax.Precision.HIGHEST)
        gate = jnp.where(past, gate, NEG_INF)
        bias = jnp.full((NR, BS), NEG_INF, F32)
        for _ in range(MOBA_TOPK):
            m = jnp.max(gate, axis=0, keepdims=True)
            first = jnp.min(jnp.where(gate == m, jidx, NR), axis=0, keepdims=True)
            hit = jidx == first
            bias = jnp.where(hit, past_bias, bias)
            gate = jnp.where(hit, -3e38, gate)
        bias_ref[h, 0] = jnp.maximum(bias, own_bias)
        km_ref[h, pl.ds(b, 1), :] = jnp.mean(kh, axis=0, keepdims=True)
        qf = qb.astype(F32)
        kf = kb.astype(F32)
        qn = jnp.sqrt(_dot_nt(ones8, (qf * qf).astype(BF16)))
        kn2 = jnp.max(_dot_nt(ones8, (kf * kf).astype(BF16)), axis=1, keepdims=True)
        kmax = jnp.maximum(kmax_ref[h], jnp.sqrt(kn2))
        kmax_ref[h] = kmax
        bound = qn * kmax[:, 0:1] * 1.02 + 1e-3
        shift_ref[h, 0] = bound - SHIFT_HEADROOM
        own = _dot_nt(ones8, (qf * kf).astype(BF16))
        margin_ref[0, h:h + 1, :] = jnp.broadcast_to(jnp.max(bound - own, axis=1, keepdims=True)[0:1], (1, LANES))


def _moba_prep(proj_main, invf):
    T = proj_main.shape[0]
    nb = T // MOBA_BLOCK
    NR = -(-nb // 8) * 8
    BS = MOBA_BLOCK
    H = MOBA_HEADS
    return pl.pallas_call(
        _moba_prep_kernel,
        grid=(nb,),
        in_specs=[
            pl.BlockSpec((BS, COL_BLOCK), lambda b: (b, Q_M_COL)),
            pl.BlockSpec((BS, COL_BLOCK), lambda b: (b, K_M_COL)),
            pl.BlockSpec((BS, COL_BLOCK), lambda b: (b, V_M_COL)),
            pl.BlockSpec((1, LANES), lambda b: (0, 0)),
        ],
        out_specs=[
            pl.BlockSpec((H, BS, LANES), lambda b: (0, b, 0)),
            pl.BlockSpec((H, 1, BS, LANES), lambda b: (0, b, 0, 0)),
            pl.BlockSpec((H, 1, V_AUG_ROWS, BS), lambda b: (0, b, 0, 0)),
            pl.BlockSpec((H, 1, NR, BS), lambda b: (0, b, 0, 0)),
            pl.BlockSpec((H, 1, 8, BS), lambda b: (0, b, 0, 0)),
            pl.BlockSpec((1, H, LANES), lambda b: (b, 0, 0)),
        ],
        out_shape=[
            jax.ShapeDtypeStruct((H, T, LANES), BF16),
            jax.ShapeDtypeStruct((H, nb, BS, LANES), BF16),
            jax.ShapeDtypeStruct((H, nb, V_AUG_ROWS, BS), BF16),
            jax.ShapeDtypeStruct((H, nb, NR, BS), F32),
            jax.ShapeDtypeStruct((H, nb, 8, BS), F32),
            jax.ShapeDtypeStruct((nb, H, LANES), F32),
        ],
        scratch_shapes=[pltpu.VMEM((H, NR, LANES), F32), pltpu.VMEM((H, 8, LANES), F32),
                        pltpu.VMEM((2, BS, LANES), F32)],
        compiler_params=pltpu.CompilerParams(
            dimension_semantics=("arbitrary",), vmem_limit_bytes=VMEM_LIMIT),
        name="moba_prep",
    )(proj_main, proj_main, proj_main, invf)


def _moba_attn_kernel(fast_ref, q_ref, k_ref, vt_ref, bias_ref, shift_ref, g_ref, o_ref, m_ref, l_ref, acc_ref):
    hg = pl.program_id(0)
    i = pl.program_id(1)
    BS = MOBA_BLOCK
    G = range(ATTN_HEADS)
    kidx = lax.broadcasted_iota(jnp.int32, (BS, BS), 0)
    qidx = lax.broadcasted_iota(jnp.int32, (BS, BS), 1)
    causal = kidx <= qidx

    def scores(g, j, shifted):
        row = bias_ref[g, 0, pl.ds(j, 1), :]
        if shifted:
            row = row - shift_ref[g, 0, 0:1, :]
        return _dot_nt(k_ref[g, j], q_ref[g]) + row

    def write_out(g, acc, denom):
        sl = slice(LANES * g, LANES * (g + 1))
        o_ref[:, sl] = ((acc / denom).T * _silu(g_ref[:, sl])).astype(BF16)

    @pl.when(fast_ref[hg, i] != 0)
    def _():
        def tiles(js, causal_flags):
            s = [[scores(g, j, True) for g in G] for j in js]
            s = [[jnp.where(causal, sg, NEG_INF) if flag else sg for sg in sj]
                 for sj, flag in zip(s, causal_flags)]
            p = [[jnp.exp2(sg).astype(BF16) for sg in sj] for sj in s]
            pv = [[_dot(vt_ref[g, j], p[u][g]) for g in G] for u, j in enumerate(js)]
            return [functools.reduce(lambda a, b: a + b, [pv[u][g] for u in range(len(js))]) for g in G]

        n_main = i // ATTN_UNROLL
        left = i - n_main * ATTN_UNROLL
        for r in range(ATTN_UNROLL):
            @pl.when(left == r)
            def _(r=r):
                first = tiles([i] + [n_main * ATTN_UNROLL + u for u in range(r)], [True] + [False] * r)
                for g in G:
                    acc_ref[g] = first[g]

        def main_body(t, carry):
            pv = tiles([t * ATTN_UNROLL + u for u in range(ATTN_UNROLL)], [False] * ATTN_UNROLL)
            for g in G:
                acc_ref[g] = acc_ref[g] + pv[g]
            return carry

        lax.fori_loop(0, n_main, main_body, 0)
        for g in G:
            acc = acc_ref[g]
            write_out(g, acc[0:LANES], acc[LANES:LANES + 1])

    @pl.when(fast_ref[hg, i] == 0)
    def _():
        for g in G:
            s = jnp.where(causal, scores(g, i, False), NEG_INF)
            m = jnp.max(s, axis=0, keepdims=True)
            p = jnp.exp2(s - m)
            m_ref[g] = m
            l_ref[g] = jnp.sum(p, axis=0, keepdims=True)
            acc_ref[g] = _dot(vt_ref[g, i], p.astype(BF16))

        def body(j, carry):
            s = [scores(g, j, False) for g in G]
            pb, alpha = [], []
            for g in G:
                m = m_ref[g]
                m_new = jnp.maximum(m, jnp.max(s[g], axis=0, keepdims=True))
                a = jnp.exp2(m - m_new)
                p = jnp.exp2(s[g] - m_new)
                m_ref[g] = m_new
                l_ref[g] = a * l_ref[g] + jnp.sum(p, axis=0, keepdims=True)
                pb.append(p.astype(BF16))
                alpha.append(a)
            pv = [_dot(vt_ref[g, j], pb[g]) for g in G]
            for g in G:
                acc_ref[g] = alpha[g] * acc_ref[g] + pv[g]
            return carry

        lax.fori_loop(0, i, body, 0)
        for g in G:
            write_out(g, acc_ref[g][0:LANES], l_ref[g])


def _moba_attn(fast, qs, kr, vt, bias, shift, proj_main):
    H, T, _ = qs.shape
    nb = T // MOBA_BLOCK
    BS = MOBA_BLOCK
    G = ATTN_HEADS
    NR = bias.shape[2]
    resident = pl.Buffered(1)
    grid_spec = pltpu.PrefetchScalarGridSpec(
        num_scalar_prefetch=1,
        grid=(H // G, nb),
        in_specs=[
            pl.BlockSpec((G, BS, LANES), lambda h, i, f: (h, i, 0)),
            pl.BlockSpec((G, nb, BS, LANES), lambda h, i, f: (h, 0, 0, 0), pipeline_mode=resident),
            pl.BlockSpec((G, nb, V_AUG_ROWS, BS), lambda h, i, f: (h, 0, 0, 0), pipeline_mode=resident),
            pl.BlockSpec((G, 1, NR, BS), lambda h, i, f: (h, i, 0, 0)),
            pl.BlockSpec((G, 1, 8, BS), lambda h, i, f: (h, i, 0, 0)),
            pl.BlockSpec((BS, G * LANES), lambda h, i, f: (i, GATE_M_COL * (H // G) + h)),
        ],
        out_specs=pl.BlockSpec((BS, G * LANES), lambda h, i, f: (i, h)),
        scratch_shapes=[
            pltpu.VMEM((G, 1, BS), F32), pltpu.VMEM((G, 1, BS), F32), pltpu.VMEM((G, V_AUG_ROWS, BS), F32),
        ],
    )
    return pl.pallas_call(
        _moba_attn_kernel,
        grid_spec=grid_spec,
        out_shape=jax.ShapeDtypeStruct((T, MOBA_WIDTH), BF16),
        compiler_params=pltpu.CompilerParams(
            dimension_semantics=("arbitrary", "arbitrary"), vmem_limit_bytes=VMEM_LIMIT),
        name="moba_attn",
    )(fast, qs, kr, vt, bias, shift, proj_main)


def _out_kernel(mr_ref, mm_ref, x_ref, wt_ref, wb_ref, fw_ref, o_ref):
    h = x_ref[...] + _dot(mr_ref[...], wt_ref[...]) + _dot(mm_ref[...], wb_ref[...])
    ms = jnp.mean(h * h, axis=-1, keepdims=True)
    o_ref[...] = h * lax.rsqrt(ms + RMS_EPS) * fw_ref[...]


def _out_proj(mixed_r, mixed_m, x2, w_out, fw, tm):
    T = x2.shape[0]
    assert T % tm == 0
    return pl.pallas_call(
        _out_kernel,
        grid=(T // tm,),
        in_specs=[
            pl.BlockSpec((tm, RWKV_WIDTH), lambda i: (i, 0)),
            pl.BlockSpec((tm, MOBA_WIDTH), lambda i: (i, 0)),
            pl.BlockSpec((tm, D_MODEL), lambda i: (i, 0)),
            pl.BlockSpec((RWKV_WIDTH, D_MODEL), lambda i: (0, 0)),
            pl.BlockSpec((MOBA_WIDTH, D_MODEL), lambda i: (1, 0)),
            pl.BlockSpec((1, D_MODEL), lambda i: (0, 0)),
        ],
        out_specs=pl.BlockSpec((tm, D_MODEL), lambda i: (i, 0)),
        out_shape=jax.ShapeDtypeStruct((T, D_MODEL), F32),
        compiler_params=pltpu.CompilerParams(
            dimension_semantics=("arbitrary",), vmem_limit_bytes=VMEM_LIMIT),
        name="out_proj",
    )(mixed_r, mixed_m, x2, w_out, w_out, fw)


def kernel(x, norm_w, w_in, mu_rkv, mu_lora, w_decay_up, w_decay_bias, w_iclr_up, w_iclr_bias,
           k_k, k_a, r_k, lnx_w, lnx_b, w_out, final_norm_w):
    B, T, _ = x.shape
    assert B == 1 and T % MOBA_BLOCK == 0
    x2 = x.reshape(T, D_MODEL)

    w_main, w_lora = _weight_relayout(w_in)
    tm_proj = min(1024, T)
    proj_main, proj_lora = _project(x2, norm_w[0:1], w_main, w_lora, tm_proj)

    rows = [mu_rkv[0, 0], mu_rkv[0, 1], mu_rkv[0, 2], w_decay_bias[0], w_iclr_bias[0], k_k[0], k_a[0],
            r_k[0].reshape(-1), lnx_w[0], lnx_b[0]]
    vec_tab = jnp.stack(rows + [jnp.zeros_like(rows[0])] * (N_VEC - len(rows)), axis=0)
    mu_l = mu_lora[0].reshape(1, 2 * LORA_RANK)
    zeros_up = jnp.zeros_like(w_decay_up[0])
    wdec = jnp.concatenate([w_decay_up[0], zeros_up], axis=0)
    wicl = jnp.concatenate([zeros_up, w_iclr_up[0]], axis=0)
    w_lora_split = []
    for w_up in (wdec, wicl):
        w_hi = w_up.astype(BF16)
        w_lora_split += [w_hi, (w_up - w_hi.astype(F32)).astype(BF16)]
    mixed_r = _rwkv(proj_main, proj_lora, vec_tab, mu_l, w_lora_split)

    lane = jnp.arange(LANES)
    inv_freq = ROPE_THETA ** (-(lane % ROT_HALF).astype(F32) / ROT_HALF)
    qs, kr, vt, bias, shift, margin = _moba_prep(proj_main, inv_freq.reshape(1, LANES))
    ok = (margin[:, :, 0] <= FAST_WINDOW).reshape(-1, MOBA_HEADS // ATTN_HEADS, ATTN_HEADS)
    fast = jnp.all(ok, axis=-1).T.astype(jnp.int32)
    mixed_m = _moba_attn(fast, qs, kr, vt, bias, shift, proj_main)

    tm_out = min(512, T)
    out = _out_proj(mixed_r, mixed_m, x2, w_out[0].astype(BF16), final_norm_w.reshape(1, D_MODEL), tm_out)
    return out.reshape(B, T, D_MODEL)
```

```python
import functools

import jax
import jax.numpy as jnp
from jax import lax
from jax.experimental import pallas as pl
from jax.experimental.pallas import tpu as pltpu

F32 = jnp.float32
BF16 = jnp.bfloat16

D_MODEL = 2048
RWKV_WIDTH = 1024
RWKV_HEAD_DIM = 64
LORA_RANK = 64
MOBA_WIDTH = 1024
MOBA_HEAD_DIM = 128
MOBA_HEADS = 8
ROT_HALF = 16
ROT_PAIR_LANE = 64
ROPE_THETA = 500000.0
MOBA_BLOCK = 256
MOBA_TOPK = 3
RMS_EPS = 1e-6
LNX_EPS = 64e-5
NEG_INF = -1e30

LANES = 128
N_PAIRS = RWKV_WIDTH // LANES
CHUNK = 64
RWKV_CHUNKS_PER_STEP = 4
RECUR_SIDE_SCHEDULE = (0, 0, 0, 2, 2, 2, 2, 2, 2, 2, 1, 1, 1, 0, 0, 0, 0, 0)
N_MAIN_BLOCKS = 8
R_COL, K_COL, V_COL, GATE_R_COL, Q_M_COL, K_M_COL, V_M_COL, GATE_M_COL = range(N_MAIN_BLOCKS)
COL_BLOCK = 1024
PROJ_SUB_ROWS = 256
WPREP_ROWS = 256
VMEM_LIMIT = 56 * 1024 * 1024
ATTN_HEADS = 4
ATTN_UNROLL = 16
LOG2E = 1.4426950408889634
V_AUG_ROWS = 144
SHIFT_HEADROOM = 20.0
FAST_WINDOW = 120.0


def _dot(a, b, precision=None):
    return jnp.dot(a, b, preferred_element_type=F32, precision=precision)


def _dot_nt(a, b, precision=None):
    return lax.dot_general(a, b, (((1,), (1,)), ((), ())), preferred_element_type=F32,
                           precision=precision)


def _silu(g):
    return g / (1.0 + jnp.exp(-g))


def _wprep_kernel(w_ref, om_ref, ol_ref):
    lora0 = 3 * RWKV_WIDTH
    lora1 = lora0 + 2 * LORA_RANK
    skip = lora1 - lora0
    qk0, qk1 = Q_M_COL * COL_BLOCK, V_M_COL * COL_BLOCK
    om_ref[:, 0:lora0] = w_ref[0, :, 0:lora0].astype(BF16)
    om_ref[:, lora0:qk0] = w_ref[0, :, lora1:qk0 + skip].astype(BF16)
    om_ref[:, qk1:] = w_ref[0, :, qk1 + skip:].astype(BF16)
    ol_ref[...] = w_ref[0, :, lora0:lora1].astype(BF16)
    lane = lax.broadcasted_iota(jnp.int32, (w_ref.shape[1], LANES), 1)
    up = (lane >= ROT_HALF) & (lane < 2 * ROT_HALF)
    down = (lane >= ROT_PAIR_LANE) & (lane < ROT_PAIR_LANE + ROT_HALF)
    for c0 in range(qk0, qk1, LANES):
        x = w_ref[0, :, c0 + skip:c0 + skip + LANES]
        moved = jnp.where(up, pltpu.roll(x, LANES - (ROT_PAIR_LANE - ROT_HALF), axis=1),
                          jnp.where(down, pltpu.roll(x, ROT_PAIR_LANE - ROT_HALF, axis=1), x))
        om_ref[:, c0:c0 + LANES] = moved.astype(BF16)


def _weight_relayout(w_in):
    d_in = w_in.shape[2]
    n_main = N_MAIN_BLOCKS * COL_BLOCK
    assert d_in == n_main + 2 * LORA_RANK
    return pl.pallas_call(
        _wprep_kernel,
        grid=(D_MODEL // WPREP_ROWS,),
        in_specs=[pl.BlockSpec((1, WPREP_ROWS, d_in), lambda i: (0, i, 0))],
        out_specs=[pl.BlockSpec((WPREP_ROWS, n_main), lambda i: (i, 0)),
                   pl.BlockSpec((WPREP_ROWS, 2 * LORA_RANK), lambda i: (i, 0))],
        out_shape=[jax.ShapeDtypeStruct((D_MODEL, n_main), BF16),
                   jax.ShapeDtypeStruct((D_MODEL, 2 * LORA_RANK), BF16)],
        compiler_params=pltpu.CompilerParams(
            dimension_semantics=("arbitrary",), vmem_limit_bytes=VMEM_LIMIT),
        name="wprep",
    )(w_in)


def _proj_kernel(x_ref, nw_ref, wm_ref, wl_ref, om_ref, ol_ref, xn_ref):
    n = pl.program_id(1)

    @pl.when(n == 0)
    def _():
        for r0 in range(0, x_ref.shape[0], PROJ_SUB_ROWS):
            rows = slice(r0, r0 + PROJ_SUB_ROWS)
            x = x_ref[rows, :]
            ms = jnp.mean(x * x, axis=-1, keepdims=True)
            xn = (x * lax.rsqrt(ms + RMS_EPS) * nw_ref[...]).astype(BF16)
            xn_ref[rows, :] = xn
            ol_ref[rows, :] = _dot(xn, wl_ref[...])

    om_ref[...] = _dot(xn_ref[...], wm_ref[...])


def _project(x2, norm_w, w_main, w_lora, tm):
    T = x2.shape[0]
    assert T % tm == 0 and tm % PROJ_SUB_ROWS == 0
    return pl.pallas_call(
        _proj_kernel,
        grid=(T // tm, N_MAIN_BLOCKS),
        in_specs=[
            pl.BlockSpec((tm, D_MODEL), lambda i, n: (i, 0)),
            pl.BlockSpec((1, D_MODEL), lambda i, n: (0, 0)),
            pl.BlockSpec((D_MODEL, COL_BLOCK), lambda i, n: (0, n)),
            pl.BlockSpec((D_MODEL, LANES), lambda i, n: (0, 0)),
        ],
        out_specs=[
            pl.BlockSpec((tm, COL_BLOCK), lambda i, n: (i, n)),
            pl.BlockSpec((tm, LANES), lambda i, n: (i, 0)),
        ],
        out_shape=[
            jax.ShapeDtypeStruct((T, N_MAIN_BLOCKS * COL_BLOCK), F32),
            jax.ShapeDtypeStruct((T, LANES), F32),
        ],
        scratch_shapes=[pltpu.VMEM((tm, D_MODEL), BF16)],
        compiler_params=pltpu.CompilerParams(
            dimension_semantics=("arbitrary", "arbitrary"), vmem_limit_bytes=VMEM_LIMIT),
        name="proj",
    )(x2, norm_w, w_main, w_lora)


_V_MU_R, _V_MU_K, _V_MU_V, _V_DEC_B, _V_ICL_B, _V_KK, _V_KA, _V_RK, _V_LNW, _V_LNB = range(10)
N_VEC = 16


def _shift_rows(z, prev_row):
    rolled = pltpu.roll(z, 1, axis=0)
    row = lax.broadcasted_iota(jnp.int32, z.shape, 0)
    return jnp.where(row == 0, prev_row, rolled)


def _split3(x):
    hi = x.astype(BF16)
    rem = x - hi.astype(F32)
    mid = rem.astype(BF16)
    return hi, mid, (rem - mid.astype(F32)).astype(BF16)


def _interleave(main, side, side_per_main):
    for n_side in side_per_main:
        next(main, None)
        for _ in range(n_side):
            next(side, None)
    for _ in main:
        pass
    for _ in side:
        pass


def _rwkv_kernel(r_ref, k_ref, v_ref, g_ref, lo_ref, vec_ref, mul_ref, wdh_ref, wdl_ref, wih_ref, wil_ref,
                 o_ref, s_ref, prev_ref, prevl_ref, lw_ref, a_ref, tr_ref, tk_ref, t2_ref, ss_ref, sm_ref,
                 *slots):
    C = CHUNK
    NC = RWKV_CHUNKS_PER_STEP
    TB = NC * C
    n_slot = len(slots) // 2
    slots = (slots[:n_slot], slots[n_slot:])

    @pl.when(pl.program_id(0) == 0)
    def _():
        s_ref[...] = jnp.zeros_like(s_ref)
        prev_ref[...] = jnp.zeros_like(prev_ref)
        prevl_ref[...] = jnp.zeros_like(prevl_ref)

    def vec(i, sl=slice(None)):
        return vec_ref[i:i + 1, sl]

    lo = lo_ref[...]
    lop = _shift_rows(lo, prevl_ref[0:1, :])
    lo = lo + mul_ref[...] * (lop - lo)
    lane_t = lax.broadcasted_iota(jnp.int32, (TB, LANES), 1)
    lmix = jnp.where(lane_t < LORA_RANK, jnp.tanh(lo), lo)
    l_hi = lmix.astype(BF16)
    l_lo = (lmix - l_hi.astype(F32)).astype(BF16)

    def dot3(w_hi_ref, w_lo_ref):
        w_hi = w_hi_ref[...]
        return _dot(l_hi, w_hi) + (_dot(l_lo, w_hi) + _dot(l_hi, w_lo_ref[...]))

    z = -(vec(_V_DEC_B) + dot3(wdh_ref, wdl_ref))
    softplus = jnp.maximum(z, 0.0) + jnp.log(1.0 + jnp.exp(-jnp.abs(z)))
    lw_ref[...] = -jnp.exp(-softplus - 0.5)
    a_ref[...] = 1.0 / (1.0 + jnp.exp(-(vec(_V_ICL_B) + dot3(wih_ref, wil_ref))))

    lane = lax.broadcasted_iota(jnp.int32, (C, LANES), 1)
    ti = lax.broadcasted_iota(jnp.int32, (C, C), 0)
    si = lax.broadcasted_iota(jnp.int32, (C, C), 1)
    tril = jnp.where(si <= ti, 1.0, 0.0).astype(BF16)
    tril3 = jnp.concatenate([tril, tril, tril], axis=1)
    gi = lax.broadcasted_iota(jnp.int32, (LANES, LANES), 0)
    gj = lax.broadcasted_iota(jnp.int32, (LANES, LANES), 1)
    same_head = (gi < RWKV_HEAD_DIM) == (gj < RWKV_HEAD_DIM)
    ones_bd = jnp.where(same_head, 1.0, 0.0).astype(BF16)
    lo_half = lane < RWKV_HEAD_DIM
    t_idx = lax.broadcasted_iota(jnp.int32, (C, LANES), 0)
    s_idx = lane & (RWKV_HEAD_DIM - 1)
    strict = s_idx < t_idx
    incl = s_idx <= t_idx
    zero = jnp.zeros((C, LANES), F32)
    eye_pair = jnp.where(s_idx == t_idx, 1.0, 0.0)
    below = {}
    b = 1
    while b < C:
        below[b] = (((t_idx // b) & 1) == 1) & ((s_idx // (2 * b)) == (t_idx // (2 * b))) & (((s_idx // b) & 1) == 0)
        b *= 2

    def bd(m):
        zeros = jnp.zeros_like(m)
        return jnp.concatenate([jnp.where(lo_half, m, zeros), jnp.where(lo_half, zeros, m)], axis=0)

    def prepare(c):
        xa_ref, yb_ref, yk_ref, vb_ref, vf_ref, wb_ref, pc_ref, bon_ref = slots[c % 2]
        rows = slice(c * C, (c + 1) * C)
        cols = [slice(LANES * p, LANES * (p + 1)) for p in range(N_PAIRS)]
        for p in range(N_PAIRS):
            sl = cols[p]

            def shifted(ref, slot, mu):
                zc = ref[rows, sl]
                before = prev_ref[slot:slot + 1, sl] if c == 0 else ref[c * C - 1:c * C, sl]
                return zc + mu * (_shift_rows(zc, before) - zc)

            r = shifted(r_ref, 0, vec(_V_MU_R, sl))
            k = shifted(k_ref, 1, vec(_V_MU_K, sl))
            v = shifted(v_ref, 2, vec(_V_MU_V, sl))
            kk = k * vec(_V_KK, sl)
            k2 = k * (1.0 + (a_ref[rows, sl] - 1.0) * vec(_V_KA, sl))
            tr_ref[p] = r
            tk_ref[p] = kk
            t2_ref[p] = k2
            vf_ref[p] = v
            vb_ref[p] = v.astype(BF16)
            ss_ref[p] = (kk * kk).astype(BF16)
            ss_ref[N_PAIRS + p] = (r * k2 * vec(_V_RK, sl)).astype(BF16)
            yield
        sums = _dot(ss_ref[...].reshape(2 * N_PAIRS * C, LANES), ones_bd).reshape(2 * N_PAIRS, C, LANES)
        for p in range(2 * N_PAIRS):
            sm_ref[p] = sums[p]
        yield
        for p in range(N_PAIRS):
            sl = cols[p]
            lw = lw_ref[rows, sl]
            cl = _dot(tril3, jnp.concatenate(_split3(lw), axis=0))
            r, k2, v = tr_ref[p], t2_ref[p], vf_ref[p]
            kk = tk_ref[p] / jnp.maximum(jnp.sqrt(sm_ref[p]), 1e-12)
            b = kk * a_ref[rows, sl]
            cl_end = cl[C - 1:C, :]
            p_inv = jnp.exp(-cl)
            p_tail = jnp.exp(cl_end - cl)
            xa_ref[p, 0:C, :] = (-kk * jnp.exp(cl - lw)).astype(BF16)
            xa_ref[p, C:2 * C, :] = (r * jnp.exp(cl)).astype(BF16)
            yb_ref[p] = (b * p_inv).astype(BF16)
            yk_ref[p] = (k2 * p_inv).astype(BF16)
            wb_ref[p, 0:C, :] = (b * p_tail).astype(BF16)
            wb_ref[p, C:2 * C, :] = (k2 * p_tail).astype(BF16)
            pc_ref[p] = jnp.broadcast_to(jnp.exp(cl_end), (8, LANES))
            bon_ref[p] = sm_ref[N_PAIRS + p] * v
            yield

    def recur(c):
        xa_ref, yb_ref, yk_ref, vb_ref, vf_ref, wb_ref, pc_ref, bon_ref = slots[c % 2]
        rows = slice(c * C, (c + 1) * C)
        P = range(N_PAIRS)
        x = [xa_ref[p] for p in P]
        sc = [_dot_nt(x[p], jnp.concatenate([bd(yb_ref[p]), bd(yk_ref[p])], axis=0)) for p in P]
        yield
        st = [s_ref[p].T.astype(BF16) for p in P]
        a_ab = [jnp.where(strict, sc[p][0:C, 0:LANES], zero) for p in P]
        b_rb = [jnp.where(incl, sc[p][C:2 * C, 0:LANES], zero).astype(BF16) for p in P]
        akrk = [jnp.concatenate([jnp.where(strict, sc[p][0:C, LANES:2 * LANES], zero),
                                 jnp.where(incl, sc[p][C:2 * C, LANES:2 * LANES], zero)],
                                axis=0).astype(BF16) for p in P]
        yield
        xsav = []

        def emit_xsav():
            p = len(xsav)
            if p < N_PAIRS:
                xsav.append(_dot(jnp.concatenate([x[p], akrk[p]], axis=1),
                                 jnp.concatenate([st[p], bd(vb_ref[p])], axis=0)))

        t_inv = [eye_pair + jnp.where(below[1], a_ab[p], zero) for p in P]
        b = 2
        while b < C:
            tb = [t_inv[p].astype(BF16) for p in P]
            w = [_dot(tb[p], bd(jnp.where(below[b], a_ab[p], zero).astype(BF16))) for p in P]
            emit_xsav()
            yield
            t_inv = [t_inv[p] + _dot(w[p].astype(BF16), bd(tb[p])) for p in P]
            emit_xsav()
            yield
            b *= 2
        while len(xsav) < N_PAIRS:
            emit_xsav()
        u = [_dot(t_inv[p].astype(BF16), bd(xsav[p][0:C].astype(BF16))) for p in P]
        yield
        y = [xsav[p][C:2 * C] + _dot(b_rb[p], bd(u[p].astype(BF16))) for p in P]
        yield
        uvt = [jnp.concatenate([u[p], vf_ref[p]], axis=0).T.astype(BF16) for p in P]
        zmat = [_dot(uvt[p], wb_ref[p]) for p in P]
        for p in P:
            s_ref[p] = s_ref[p] * pc_ref[p][0:1, :] + jnp.where(same_head, zmat[p], jnp.zeros_like(zmat[p]))
        yield
        inv_n = 1.0 / RWKV_HEAD_DIM

        def head_mean(ts):
            flat = jnp.concatenate([t.astype(BF16) for t in ts], axis=0)
            m = _dot(flat, ones_bd) * inv_n
            return [m[C * p:C * (p + 1)] for p in P]

        mean = head_mean(y)
        d = [y[p] - mean[p] for p in P]
        yield
        var = head_mean([d[p] * d[p] for p in P])
        for p in P:
            sl = slice(LANES * p, LANES * (p + 1))
            yn = d[p] * lax.rsqrt(var[p] + LNX_EPS) * vec(_V_LNW, sl) + vec(_V_LNB, sl)
            o_ref[rows, sl] = ((yn + bon_ref[p]) * _silu(g_ref[rows, sl])).astype(BF16)
        yield

    for _ in prepare(0):
        pass
    for c in range(NC):
        side = prepare(c + 1) if c + 1 < NC else iter(())
        _interleave(recur(c), side, RECUR_SIDE_SCHEDULE)

    prev_ref[0:1, :] = r_ref[TB - 1:TB, :]
    prev_ref[1:2, :] = k_ref[TB - 1:TB, :]
    prev_ref[2:3, :] = v_ref[TB - 1:TB, :]
    prevl_ref[0:1, :] = lo_ref[TB - 1:TB, :]


def _rwkv(proj_main, proj_lora, vec_tab, mu_l, w_lora_split):
    T = proj_main.shape[0]
    C = CHUNK
    TB = RWKV_CHUNKS_PER_STEP * C
    assert T % TB == 0
    pair_bf = lambda rows: pltpu.VMEM((N_PAIRS, rows, LANES), BF16)
    pair_f32 = lambda rows: pltpu.VMEM((N_PAIRS, rows, LANES), F32)
    slot = [pair_bf(2 * C), pair_bf(C), pair_bf(C), pair_bf(C), pair_f32(C), pair_bf(2 * C),
            pair_f32(8), pair_f32(C)]
    col = lambda j: pl.BlockSpec((TB, COL_BLOCK), lambda c: (c, j))
    whole = lambda shape: pl.BlockSpec(shape, lambda c: (0,) * len(shape))
    return pl.pallas_call(
        _rwkv_kernel,
        grid=(T // TB,),
        in_specs=[col(R_COL), col(K_COL), col(V_COL), col(GATE_R_COL),
                  pl.BlockSpec((TB, LANES), lambda c: (c, 0)),
                  whole((N_VEC, RWKV_WIDTH)), whole((1, LANES))]
                 + [whole((LANES, RWKV_WIDTH))] * 4,
        out_specs=pl.BlockSpec((TB, RWKV_WIDTH), lambda c: (c, 0)),
        out_shape=jax.ShapeDtypeStruct((T, RWKV_WIDTH), BF16),
        scratch_shapes=[
            pltpu.VMEM((N_PAIRS, LANES, LANES), F32),
            pltpu.VMEM((8, RWKV_WIDTH), F32),
            pltpu.VMEM((8, LANES), F32),
            pltpu.VMEM((TB, RWKV_WIDTH), F32),
            pltpu.VMEM((TB, RWKV_WIDTH), F32),
            pair_f32(C), pair_f32(C), pair_f32(C),
            pltpu.VMEM((2 * N_PAIRS, C, LANES), BF16),
            pltpu.VMEM((2 * N_PAIRS, C, LANES), F32),
        ] + slot + slot,
        compiler_params=pltpu.CompilerParams(
            dimension_semantics=("arbitrary",), vmem_limit_bytes=VMEM_LIMIT),
        name="rwkv",
    )(proj_main, proj_main, proj_main, proj_main, proj_lora, vec_tab, mu_l, *w_lora_split)


def _moba_prep_kernel(q_ref, k_ref, v_ref, invf_ref, qs_ref, kr_ref, vt_ref, bias_ref, shift_ref, margin_ref,
                      km_ref, kmax_ref, cs_ref):
    b = pl.program_id(0)
    BS = MOBA_BLOCK
    NR = bias_ref.shape[2]
    lane = lax.broadcasted_iota(jnp.int32, (BS, LANES), 1)

    @pl.when(b == 0)
    def _():
        km_ref[...] = jnp.zeros_like(km_ref)
        kmax_ref[...] = jnp.zeros_like(kmax_ref)
        local = lax.broadcasted_iota(jnp.int32, (BS, LANES), 0).astype(F32) * invf_ref[...]
        cs_ref[0] = jnp.cos(local)
        cs_ref[1] = jnp.sin(local)

    base = (b * BS).astype(F32) * invf_ref[...]
    cb, sb = jnp.cos(base), jnp.sin(base)
    cos = cb * cs_ref[0] - sb * cs_ref[1]
    sin = sb * cs_ref[0] + cb * cs_ref[1]
    first = lane < ROT_HALF
    second = (lane >= ROT_PAIR_LANE) & (lane < ROT_PAIR_LANE + ROT_HALF)
    cos_f = jnp.where(first | second, cos, 1.0)
    sin_t = jnp.where(first, sin, jnp.where(second, -sin, 0.0))

    def rope(x):
        return x * cos_f + pltpu.roll(x * sin_t, ROT_PAIR_LANE, axis=1)

    jidx = lax.broadcasted_iota(jnp.int32, (NR, BS), 0)
    past = jidx < b
    past_bias = jnp.where(past, 0.0, NEG_INF)
    own_bias = jnp.where(jidx == b, 0.0, NEG_INF)
    ones8 = jnp.ones((8, LANES), BF16)
    scale = MOBA_HEAD_DIM ** -0.5
    ones_rows = jnp.where(lax.broadcasted_iota(jnp.int32, (V_AUG_ROWS - LANES, BS), 0) == 0, 1.0, 0.0)
    for h in range(MOBA_HEADS):
        sl = slice(LANES * h, LANES * (h + 1))
        qh = rope(q_ref[:, sl])
        kh = rope(k_ref[:, sl])
        qb = (qh * (scale * LOG2E)).astype(BF16)
        kb = kh.astype(BF16)
        qs_ref[h] = qb
        kr_ref[h, 0] = kb
        vt_ref[h, 0] = jnp.concatenate([v_ref[:, sl].T, ones_rows], axis=0).astype(BF16)
        gate = _dot_nt(km_ref[h], qh, lax.Precision.HIGHEST)
        gate = jnp.where(past, gate, NEG_INF)
        bias = jnp.full((NR, BS), NEG_INF, F32)
        for _ in range(MOBA_TOPK):
            m = jnp.max(gate, axis=0, keepdims=True)
            first = jnp.min(jnp.where(gate == m, jidx, NR), axis=0, keepdims=True)
            hit = jidx == first
            bias = jnp.where(hit, past_bias, bias)
            gate = jnp.where(hit, -3e38, gate)
        bias_ref[h, 0] = jnp.maximum(bias, own_bias)
        km_ref[h, pl.ds(b, 1), :] = jnp.mean(kh, axis=0, keepdims=True)
        qf = qb.astype(F32)
        kf = kb.astype(F32)
        qn = jnp.sqrt(_dot_nt(ones8, (qf * qf).astype(BF16)))
        kn2 = jnp.max(_dot_nt(ones8, (kf * kf).astype(BF16)), axis=1, keepdims=True)
        kmax = jnp.maximum(kmax_ref[h], jnp.sqrt(kn2))
        kmax_ref[h] = kmax
        bound = qn * kmax[:, 0:1] * 1.02 + 1e-3
        shift_ref[h, 0] = bound - SHIFT_HEADROOM
        own = _dot_nt(ones8, (qf * kf).astype(BF16))
        margin_ref[0, h:h + 1, :] = jnp.broadcast_to(jnp.max(bound - own, axis=1, keepdims=True)[0:1], (1, LANES))


def _moba_prep(proj_main, invf):
    T = proj_main.shape[0]
    nb = T // MOBA_BLOCK
    NR = -(-nb // 8) * 8
    BS = MOBA_BLOCK
    H = MOBA_HEADS
    return pl.pallas_call(
        _moba_prep_kernel,
        grid=(nb,),
        in_specs=[
            pl.BlockSpec((BS, COL_BLOCK), lambda b: (b, Q_M_COL)),
            pl.BlockSpec((BS, COL_BLOCK), lambda b: (b, K_M_COL)),
            pl.BlockSpec((BS, COL_BLOCK), lambda b: (b, V_M_COL)),
            pl.BlockSpec((1, LANES), lambda b: (0, 0)),
        ],
        out_specs=[
            pl.BlockSpec((H, BS, LANES), lambda b: (0, b, 0)),
            pl.BlockSpec((H, 1, BS, LANES), lambda b: (0, b, 0, 0)),
            pl.BlockSpec((H, 1, V_AUG_ROWS, BS), lambda b: (0, b, 0, 0)),
            pl.BlockSpec((H, 1, NR, BS), lambda b: (0, b, 0, 0)),
            pl.BlockSpec((H, 1, 8, BS), lambda b: (0, b, 0, 0)),
            pl.BlockSpec((1, H, LANES), lambda b: (b, 0, 0)),
        ],
        out_shape=[
            jax.ShapeDtypeStruct((H, T, LANES), BF16),
            jax.ShapeDtypeStruct((H, nb, BS, LANES), BF16),
            jax.ShapeDtypeStruct((H, nb, V_AUG_ROWS, BS), BF16),
            jax.ShapeDtypeStruct((H, nb, NR, BS), F32),
            jax.ShapeDtypeStruct((H, nb, 8, BS), F32),
            jax.ShapeDtypeStruct((nb, H, LANES), F32),
        ],
        scratch_shapes=[pltpu.VMEM((H, NR, LANES), F32), pltpu.VMEM((H, 8, LANES), F32),
                        pltpu.VMEM((2, BS, LANES), F32)],
        compiler_params=pltpu.CompilerParams(
            dimension_semantics=("arbitrary",), vmem_limit_bytes=VMEM_LIMIT),
        name="moba_prep",
    )(proj_main, proj_main, proj_main, invf)


def _moba_attn_kernel(fast_ref, q_ref, k_ref, vt_ref, bias_ref, shift_ref, g_ref, o_ref, m_ref, l_ref, acc_ref):
    hg = pl.program_id(0)
    i = pl.program_id(1)
    BS = MOBA_BLOCK
    G = range(ATTN_HEADS)
    kidx = lax.broadcasted_iota(jnp.int32, (BS, BS), 0)
    qidx = lax.broadcasted_iota(jnp.int32, (BS, BS), 1)
    causal = kidx <= qidx

    def scores(g, j, shifted):
        row = bias_ref[g, 0, pl.ds(j, 1), :]
        if shifted:
            row = row - shift_ref[g, 0, 0:1, :]
        return _dot_nt(k_ref[g, j], q_ref[g]) + row

    def write_out(g, acc, denom):
        sl = slice(LANES * g, LANES * (g + 1))
        o_ref[:, sl] = ((acc / denom).T * _silu(g_ref[:, sl])).astype(BF16)

    @pl.when(fast_ref[hg, i] != 0)
    def _():
        def tiles(js, causal_flags):
            s = [[scores(g, j, True) for g in G] for j in js]
            s = [[jnp.where(causal, sg, NEG_INF) if flag else sg for sg in sj]
                 for sj, flag in zip(s, causal_flags)]
            p = [[jnp.exp2(sg).astype(BF16) for sg in sj] for sj in s]
            pv = [[_dot(vt_ref[g, j], p[u][g]) for g in G] for u, j in enumerate(js)]
            return [functools.reduce(lambda a, b: a + b, [pv[u][g] for u in range(len(js))]) for g in G]

        n_main = i // ATTN_UNROLL
        left = i - n_main * ATTN_UNROLL
        for r in range(ATTN_UNROLL):
            @pl.when(left == r)
            def _(r=r):
                first = tiles([i] + [n_main * ATTN_UNROLL + u for u in range(r)], [True] + [False] * r)
                for g in G:
                    acc_ref[g] = first[g]

        def main_body(t, carry):
            pv = tiles([t * ATTN_UNROLL + u for u in range(ATTN_UNROLL)], [False] * ATTN_UNROLL)
            for g in G:
                acc_ref[g] = acc_ref[g] + pv[g]
            return carry

        lax.fori_loop(0, n_main, main_body, 0)
        for g in G:
            acc = acc_ref[g]
            write_out(g, acc[0:LANES], acc[LANES:LANES + 1])

    @pl.when(fast_ref[hg, i] == 0)
    def _():
        for g in G:
            s = jnp.where(causal, scores(g, i, False), NEG_INF)
            m = jnp.max(s, axis=0, keepdims=True)
            p = jnp.exp2(s - m)
            m_ref[g] = m
            l_ref[g] = jnp.sum(p, axis=0, keepdims=True)
            acc_ref[g] = _dot(vt_ref[g, i], p.astype(BF16))

        def body(j, carry):
            s = [scores(g, j, False) for g in G]
            pb, alpha = [], []
            for g in G:
                m = m_ref[g]
                m_new = jnp.maximum(m, jnp.max(s[g], axis=0, keepdims=True))
                a = jnp.exp2(m - m_new)
                p = jnp.exp2(s[g] - m_new)
                m_ref[g] = m_new
                l_ref[g] = a * l_ref[g] + jnp.sum(p, axis=0, keepdims=True)
                pb.append(p.astype(BF16))
                alpha.append(a)
            pv = [_dot(vt_ref[g, j], pb[g]) for g in G]
            for g in G:
                acc_ref[g] = alpha[g] * acc_ref[g] + pv[g]
            return carry

        lax.fori_loop(0, i, body, 0)
        for g in G:
            write_out(g, acc_ref[g][0:LANES], l_ref[g])


def _moba_attn(fast, qs, kr, vt, bias, shift, proj_main):
    H, T, _ = qs.shape
    nb = T // MOBA_BLOCK
    BS = MOBA_BLOCK
    G = ATTN_HEADS
    NR = bias.shape[2]
    resident = pl.Buffered(1)
    grid_spec = pltpu.PrefetchScalarGridSpec(
        num_scalar_prefetch=1,
        grid=(H // G, nb),
        in_specs=[
            pl.BlockSpec((G, BS, LANES), lambda h, i, f: (h, i, 0)),
            pl.BlockSpec((G, nb, BS, LANES), lambda h, i, f: (h, 0, 0, 0), pipeline_mode=resident),
            pl.BlockSpec((G, nb, V_AUG_ROWS, BS), lambda h, i, f: (h, 0, 0, 0), pipeline_mode=resident),
            pl.BlockSpec((G, 1, NR, BS), lambda h, i, f: (h, i, 0, 0)),
            pl.BlockSpec((G, 1, 8, BS), lambda h, i, f: (h, i, 0, 0)),
            pl.BlockSpec((BS, G * LANES), lambda h, i, f: (i, GATE_M_COL * (H // G) + h)),
        ],
        out_specs=pl.BlockSpec((BS, G * LANES), lambda h, i, f: (i, h)),
        scratch_shapes=[
            pltpu.VMEM((G, 1, BS), F32), pltpu.VMEM((G, 1, BS), F32), pltpu.VMEM((G, V_AUG_ROWS, BS), F32),
        ],
    )
    return pl.pallas_call(
        _moba_attn_kernel,
        grid_spec=grid_spec,
        out_shape=jax.ShapeDtypeStruct((T, MOBA_WIDTH), BF16),
        compiler_params=pltpu.CompilerParams(
            dimension_semantics=("arbitrary", "arbitrary"), vmem_limit_bytes=VMEM_LIMIT),
        name="moba_attn",
    )(fast, qs, kr, vt, bias, shift, proj_main)


def _out_kernel(mr_ref, mm_ref, x_ref, wt_ref, wb_ref, fw_ref, o_ref):
    h = x_ref[...] + _dot(mr_ref[...], wt_ref[...]) + _dot(mm_ref[...], wb_ref[...])
    ms = jnp.mean(h * h, axis=-1, keepdims=True)
    o_ref[...] = h * lax.rsqrt(ms + RMS_EPS) * fw_ref[...]


def _out_proj(mixed_r, mixed_m, x2, w_out, fw, tm):
    T = x2.shape[0]
    assert T % tm == 0
    return pl.pallas_call(
        _out_kernel,
        grid=(T // tm,),
        in_specs=[
            pl.BlockSpec((tm, RWKV_WIDTH), lambda i: (i, 0)),
            pl.BlockSpec((tm, MOBA_WIDTH), lambda i: (i, 0)),
            pl.BlockSpec((tm, D_MODEL), lambda i: (i, 0)),
            pl.BlockSpec((RWKV_WIDTH, D_MODEL), lambda i: (0, 0)),
            pl.BlockSpec((MOBA_WIDTH, D_MODEL), lambda i: (1, 0)),
            pl.BlockSpec((1, D_MODEL), lambda i: (0, 0)),
        ],
        out_specs=pl.BlockSpec((tm, D_MODEL), lambda i: (i, 0)),
        out_shape=jax.ShapeDtypeStruct((T, D_MODEL), F32),
        compiler_params=pltpu.CompilerParams(
            dimension_semantics=("arbitrary",), vmem_limit_bytes=VMEM_LIMIT),
        name="out_proj",
    )(mixed_r, mixed_m, x2, w_out, w_out, fw)


def kernel(x, norm_w, w_in, mu_rkv, mu_lora, w_decay_up, w_decay_bias, w_iclr_up, w_iclr_bias,
           k_k, k_a, r_k, lnx_w, lnx_b, w_out, final_norm_w):
    B, T, _ = x.shape
    assert B == 1 and T % MOBA_BLOCK == 0
    x2 = x.reshape(T, D_MODEL)

    w_main, w_lora = _weight_relayout(w_in)
    tm_proj = min(1024, T)
    proj_main, proj_lora = _project(x2, norm_w[0:1], w_main, w_lora, tm_proj)

    rows = [mu_rkv[0, 0], mu_rkv[0, 1], mu_rkv[0, 2], w_decay_bias[0], w_iclr_bias[0], k_k[0], k_a[0],
            r_k[0].reshape(-1), lnx_w[0], lnx_b[0]]
    vec_tab = jnp.stack(rows + [jnp.zeros_like(rows[0])] * (N_VEC - len(rows)), axis=0)
    mu_l = mu_lora[0].reshape(1, 2 * LORA_RANK)
    zeros_up = jnp.zeros_like(w_decay_up[0])
    wdec = jnp.concatenate([w_decay_up[0], zeros_up], axis=0)
    wicl = jnp.concatenate([zeros_up, w_iclr_up[0]], axis=0)
    w_lora_split = []
    for w_up in (wdec, wicl):
        w_hi = w_up.astype(BF16)
        w_lora_split += [w_hi, (w_up - w_hi.astype(F32)).astype(BF16)]
    mixed_r = _rwkv(proj_main, proj_lora, vec_tab, mu_l, w_lora_split)

    lane = jnp.arange(LANES)
    inv_freq = ROPE_THETA ** (-(lane % ROT_HALF).astype(F32) / ROT_HALF)
    qs, kr, vt, bias, shift, margin = _moba_prep(proj_main, inv_freq.reshape(1, LANES))
    ok = (margin[:, :, 0] <= FAST_WINDOW).reshape(-1, MOBA_HEADS // ATTN_HEADS, ATTN_HEADS)
    fast = jnp.all(ok, axis=-1).T.astype(jnp.int32)
    mixed_m = _moba_attn(fast, qs, kr, vt, bias, shift, proj_main)

    tm_out = min(512, T)
    out = _out_proj(mixed_r, mixed_m, x2, w_out[0].astype(BF16), final_norm_w.reshape(1, D_MODEL), tm_out)
    return out.reshape(B, T, D_MODEL)
```

```python
import functools

import jax
import jax.numpy as jnp
from jax import lax
from jax.experimental import pallas as pl
from jax.experimental.pallas import tpu as pltpu

F32 = jnp.float32
BF16 = jnp.bfloat16

D_MODEL = 2048
RWKV_WIDTH = 1024
RWKV_HEAD_DIM = 64
LORA_RANK = 64
MOBA_WIDTH = 1024
MOBA_HEAD_DIM = 128
MOBA_HEADS = 8
ROT_HALF = 16
ROT_PAIR_LANE = 64
ROPE_THETA = 500000.0
MOBA_BLOCK = 256
MOBA_TOPK = 3
RMS_EPS = 1e-6
LNX_EPS = 64e-5
NEG_INF = -1e30

LANES = 128
N_PAIRS = RWKV_WIDTH // LANES
CHUNK = 64
RWKV_CHUNKS_PER_STEP = 8
RECUR_SIDE_SCHEDULE = (0, 0, 0, 2, 2, 2, 2, 2, 2, 2, 1, 1, 1, 0, 0, 0, 0, 0)
N_MAIN_BLOCKS = 8
R_COL, K_COL, V_COL, GATE_R_COL, Q_M_COL, K_M_COL, V_M_COL, GATE_M_COL = range(N_MAIN_BLOCKS)
COL_BLOCK = 1024
PROJ_SUB_ROWS = 256
WPREP_ROWS = 256
VMEM_LIMIT = 56 * 1024 * 1024
ATTN_HEADS = 4
ATTN_UNROLL = 16
LOG2E = 1.4426950408889634
V_AUG_ROWS = 144
SHIFT_HEADROOM = 20.0
FAST_WINDOW = 120.0


def _dot(a, b, precision=None):
    return jnp.dot(a, b, preferred_element_type=F32, precision=precision)


def _dot_nt(a, b, precision=None):
    return lax.dot_general(a, b, (((1,), (1,)), ((), ())), preferred_element_type=F32,
                           precision=precision)


def _silu(g):
    return g / (1.0 + jnp.exp(-g))


def _wprep_kernel(w_ref, om_ref, ol_ref):
    lora0 = 3 * RWKV_WIDTH
    lora1 = lora0 + 2 * LORA_RANK
    skip = lora1 - lora0
    qk0, qk1 = Q_M_COL * COL_BLOCK, V_M_COL * COL_BLOCK
    om_ref[:, 0:lora0] = w_ref[0, :, 0:lora0].astype(BF16)
    om_ref[:, lora0:qk0] = w_ref[0, :, lora1:qk0 + skip].astype(BF16)
    om_ref[:, qk1:] = w_ref[0, :, qk1 + skip:].astype(BF16)
    ol_ref[...] = w_ref[0, :, lora0:lora1].astype(BF16)
    lane = lax.broadcasted_iota(jnp.int32, (w_ref.shape[1], LANES), 1)
    up = (lane >= ROT_HALF) & (lane < 2 * ROT_HALF)
    down = (lane >= ROT_PAIR_LANE) & (lane < ROT_PAIR_LANE + ROT_HALF)
    for c0 in range(qk0, qk1, LANES):
        x = w_ref[0, :, c0 + skip:c0 + skip + LANES]
        moved = jnp.where(up, pltpu.roll(x, LANES - (ROT_PAIR_LANE - ROT_HALF), axis=1),
                          jnp.where(down, pltpu.roll(x, ROT_PAIR_LANE - ROT_HALF, axis=1), x))
        om_ref[:, c0:c0 + LANES] = moved.astype(BF16)


def _weight_relayout(w_in):
    d_in = w_in.shape[2]
    n_main = N_MAIN_BLOCKS * COL_BLOCK
    assert d_in == n_main + 2 * LORA_RANK
    return pl.pallas_call(
        _wprep_kernel,
        grid=(D_MODEL // WPREP_ROWS,),
        in_specs=[pl.BlockSpec((1, WPREP_ROWS, d_in), lambda i: (0, i, 0))],
        out_specs=[pl.BlockSpec((WPREP_ROWS, n_main), lambda i: (i, 0)),
                   pl.BlockSpec((WPREP_ROWS, 2 * LORA_RANK), lambda i: (i, 0))],
        out_shape=[jax.ShapeDtypeStruct((D_MODEL, n_main), BF16),
                   jax.ShapeDtypeStruct((D_MODEL, 2 * LORA_RANK), BF16)],
        compiler_params=pltpu.CompilerParams(
            dimension_semantics=("arbitrary",), vmem_limit_bytes=VMEM_LIMIT),
        name="wprep",
    )(w_in)


def _proj_kernel(x_ref, nw_ref, wm_ref, wl_ref, om_ref, ol_ref, xn_ref):
    n = pl.program_id(1)

    @pl.when(n == 0)
    def _():
        for r0 in range(0, x_ref.shape[0], PROJ_SUB_ROWS):
            rows = slice(r0, r0 + PROJ_SUB_ROWS)
            x = x_ref[rows, :]
            ms = jnp.mean(x * x, axis=-1, keepdims=True)
            xn = (x * lax.rsqrt(ms + RMS_EPS) * nw_ref[...]).astype(BF16)
            xn_ref[rows, :] = xn
            ol_ref[rows, :] = _dot(xn, wl_ref[...])

    om_ref[...] = _dot(xn_ref[...], wm_ref[...])


def _project(x2, norm_w, w_main, w_lora, tm):
    T = x2.shape[0]
    assert T % tm == 0 and tm % PROJ_SUB_ROWS == 0
    return pl.pallas_call(
        _proj_kernel,
        grid=(T // tm, N_MAIN_BLOCKS),
        in_specs=[
            pl.BlockSpec((tm, D_MODEL), lambda i, n: (i, 0)),
            pl.BlockSpec((1, D_MODEL), lambda i, n: (0, 0)),
            pl.BlockSpec((D_MODEL, COL_BLOCK), lambda i, n: (0, n)),
            pl.BlockSpec((D_MODEL, LANES), lambda i, n: (0, 0)),
        ],
        out_specs=[
            pl.BlockSpec((tm, COL_BLOCK), lambda i, n: (i, n)),
            pl.BlockSpec((tm, LANES), lambda i, n: (i, 0)),
        ],
        out_shape=[
            jax.ShapeDtypeStruct((T, N_MAIN_BLOCKS * COL_BLOCK), F32),
            jax.ShapeDtypeStruct((T, LANES), F32),
        ],
        scratch_shapes=[pltpu.VMEM((tm, D_MODEL), BF16)],
        compiler_params=pltpu.CompilerParams(
            dimension_semantics=("arbitrary", "arbitrary"), vmem_limit_bytes=VMEM_LIMIT),
        name="proj",
    )(x2, norm_w, w_main, w_lora)


_V_MU_R, _V_MU_K, _V_MU_V, _V_DEC_B, _V_ICL_B, _V_KK, _V_KA, _V_RK, _V_LNW, _V_LNB = range(10)
N_VEC = 16


def _shift_rows(z, prev_row):
    rolled = pltpu.roll(z, 1, axis=0)
    row = lax.broadcasted_iota(jnp.int32, z.shape, 0)
    return jnp.where(row == 0, prev_row, rolled)


def _split3(x):
    hi = x.astype(BF16)
    rem = x - hi.astype(F32)
    mid = rem.astype(BF16)
    return hi, mid, (rem - mid.astype(F32)).astype(BF16)


def _interleave(main, side, side_per_main):
    for n_side in side_per_main:
        next(main, None)
        for _ in range(n_side):
            next(side, None)
    for _ in main:
        pass
    for _ in side:
        pass


def _rwkv_kernel(r_ref, k_ref, v_ref, g_ref, lo_ref, vec_ref, mul_ref, wdh_ref, wdl_ref, wih_ref, wil_ref,
                 o_ref, s_ref, prev_ref, prevl_ref, lw_ref, a_ref, tr_ref, tk_ref, t2_ref, ss_ref, sm_ref,
                 *slots):
    C = CHUNK
    NC = RWKV_CHUNKS_PER_STEP
    TB = NC * C
    n_slot = len(slots) // 2
    slots = (slots[:n_slot], slots[n_slot:])

    @pl.when(pl.program_id(0) == 0)
    def _():
        s_ref[...] = jnp.zeros_like(s_ref)
        prev_ref[...] = jnp.zeros_like(prev_ref)
        prevl_ref[...] = jnp.zeros_like(prevl_ref)

    def vec(i, sl=slice(None)):
        return vec_ref[i:i + 1, sl]

    lo = lo_ref[...]
    lop = _shift_rows(lo, prevl_ref[0:1, :])
    lo = lo + mul_ref[...] * (lop - lo)
    lane_t = lax.broadcasted_iota(jnp.int32, (TB, LANES), 1)
    lmix = jnp.where(lane_t < LORA_RANK, jnp.tanh(lo), lo)
    l_hi = lmix.astype(BF16)
    l_lo = (lmix - l_hi.astype(F32)).astype(BF16)

    def dot3(w_hi_ref, w_lo_ref):
        w_hi = w_hi_ref[...]
        return _dot(l_hi, w_hi) + (_dot(l_lo, w_hi) + _dot(l_hi, w_lo_ref[...]))

    z = -(vec(_V_DEC_B) + dot3(wdh_ref, wdl_ref))
    softplus = jnp.maximum(z, 0.0) + jnp.log(1.0 + jnp.exp(-jnp.abs(z)))
    lw_ref[...] = -jnp.exp(-softplus - 0.5)
    a_ref[...] = 1.0 / (1.0 + jnp.exp(-(vec(_V_ICL_B) + dot3(wih_ref, wil_ref))))

    lane = lax.broadcasted_iota(jnp.int32, (C, LANES), 1)
    ti = lax.broadcasted_iota(jnp.int32, (C, C), 0)
    si = lax.broadcasted_iota(jnp.int32, (C, C), 1)
    tril = jnp.where(si <= ti, 1.0, 0.0).astype(BF16)
    tril3 = jnp.concatenate([tril, tril, tril], axis=1)
    gi = lax.broadcasted_iota(jnp.int32, (LANES, LANES), 0)
    gj = lax.broadcasted_iota(jnp.int32, (LANES, LANES), 1)
    same_head = (gi < RWKV_HEAD_DIM) == (gj < RWKV_HEAD_DIM)
    ones_bd = jnp.where(same_head, 1.0, 0.0).astype(BF16)
    lo_half = lane < RWKV_HEAD_DIM
    t_idx = lax.broadcasted_iota(jnp.int32, (C, LANES), 0)
    s_idx = lane & (RWKV_HEAD_DIM - 1)
    strict = s_idx < t_idx
    incl = s_idx <= t_idx
    zero = jnp.zeros((C, LANES), F32)
    eye_pair = jnp.where(s_idx == t_idx, 1.0, 0.0)
    below = {}
    b = 1
    while b < C:
        below[b] = (((t_idx // b) & 1) == 1) & ((s_idx // (2 * b)) == (t_idx // (2 * b))) & (((s_idx // b) & 1) == 0)
        b *= 2

    def bd(m):
        zeros = jnp.zeros_like(m)
        return jnp.concatenate([jnp.where(lo_half, m, zeros), jnp.where(lo_half, zeros, m)], axis=0)

    def prepare(c):
        xa_ref, yb_ref, yk_ref, vb_ref, vf_ref, wb_ref, pc_ref, bon_ref = slots[c % 2]
        rows = slice(c * C, (c + 1) * C)
        cols = [slice(LANES * p, LANES * (p + 1)) for p in range(N_PAIRS)]
        for p in range(N_PAIRS):
            sl = cols[p]

            def shifted(ref, slot, mu):
                zc = ref[rows, sl]
                before = prev_ref[slot:slot + 1, sl] if c == 0 else ref[c * C - 1:c * C, sl]
                return zc + mu * (_shift_rows(zc, before) - zc)

            r = shifted(r_ref, 0, vec(_V_MU_R, sl))
            k = shifted(k_ref, 1, vec(_V_MU_K, sl))
            v = shifted(v_ref, 2, vec(_V_MU_V, sl))
            kk = k * vec(_V_KK, sl)
            k2 = k * (1.0 + (a_ref[rows, sl] - 1.0) * vec(_V_KA, sl))
            tr_ref[p] = r
            tk_ref[p] = kk
            t2_ref[p] = k2
            vf_ref[p] = v
            vb_ref[p] = v.astype(BF16)
            ss_ref[p] = (kk * kk).astype(BF16)
            ss_ref[N_PAIRS + p] = (r * k2 * vec(_V_RK, sl)).astype(BF16)
            yield
        sums = _dot(ss_ref[...].reshape(2 * N_PAIRS * C, LANES), ones_bd).reshape(2 * N_PAIRS, C, LANES)
        for p in range(2 * N_PAIRS):
            sm_ref[p] = sums[p]
        yield
        for p in range(N_PAIRS):
            sl = cols[p]
            lw = lw_ref[rows, sl]
            cl = _dot(tril3, jnp.concatenate(_split3(lw), axis=0))
            r, k2, v = tr_ref[p], t2_ref[p], vf_ref[p]
            kk = tk_ref[p] / jnp.maximum(jnp.sqrt(sm_ref[p]), 1e-12)
            b = kk * a_ref[rows, sl]
            cl_end = cl[C - 1:C, :]
            p_inv = jnp.exp(-cl)
            p_tail = jnp.exp(cl_end - cl)
            xa_ref[p, 0:C, :] = (-kk * jnp.exp(cl - lw)).astype(BF16)
            xa_ref[p, C:2 * C, :] = (r * jnp.exp(cl)).astype(BF16)
            yb_ref[p] = (b * p_inv).astype(BF16)
            yk_ref[p] = (k2 * p_inv).astype(BF16)
            wb_ref[p, 0:C, :] = (b * p_tail).astype(BF16)
            wb_ref[p, C:2 * C, :] = (k2 * p_tail).astype(BF16)
            pc_ref[p] = jnp.broadcast_to(jnp.exp(cl_end), (8, LANES))
            bon_ref[p] = sm_ref[N_PAIRS + p] * v
            yield

    def recur(c):
        xa_ref, yb_ref, yk_ref, vb_ref, vf_ref, wb_ref, pc_ref, bon_ref = slots[c % 2]
        rows = slice(c * C, (c + 1) * C)
        P = range(N_PAIRS)
        x = [xa_ref[p] for p in P]
        sc = [_dot_nt(x[p], jnp.concatenate([bd(yb_ref[p]), bd(yk_ref[p])], axis=0)) for p in P]
        yield
        st = [s_ref[p].T.astype(BF16) for p in P]
        a_ab = [jnp.where(strict, sc[p][0:C, 0:LANES], zero) for p in P]
        b_rb = [jnp.where(incl, sc[p][C:2 * C, 0:LANES], zero).astype(BF16) for p in P]
        akrk = [jnp.concatenate([jnp.where(strict, sc[p][0:C, LANES:2 * LANES], zero),
                                 jnp.where(incl, sc[p][C:2 * C, LANES:2 * LANES], zero)],
                                axis=0).astype(BF16) for p in P]
        yield
        xsav = []

        def emit_xsav():
            p = len(xsav)
            if p < N_PAIRS:
                xsav.append(_dot(jnp.concatenate([x[p], akrk[p]], axis=1),
                                 jnp.concatenate([st[p], bd(vb_ref[p])], axis=0)))

        t_inv = [eye_pair + jnp.where(below[1], a_ab[p], zero) for p in P]
        b = 2
        while b < C:
            tb = [t_inv[p].astype(BF16) for p in P]
            w = [_dot(tb[p], bd(jnp.where(below[b], a_ab[p], zero).astype(BF16))) for p in P]
            emit_xsav()
            yield
            t_inv = [t_inv[p] + _dot(w[p].astype(BF16), bd(tb[p])) for p in P]
            emit_xsav()
            yield
            b *= 2
        while len(xsav) < N_PAIRS:
            emit_xsav()
        u = [_dot(t_inv[p].astype(BF16), bd(xsav[p][0:C].astype(BF16))) for p in P]
        yield
        y = [xsav[p][C:2 * C] + _dot(b_rb[p], bd(u[p].astype(BF16))) for p in P]
        yield
        uvt = [jnp.concatenate([u[p], vf_ref[p]], axis=0).T.astype(BF16) for p in P]
        zmat = [_dot(uvt[p], wb_ref[p]) for p in P]
        for p in P:
            s_ref[p] = s_ref[p] * pc_ref[p][0:1, :] + jnp.where(same_head, zmat[p], jnp.zeros_like(zmat[p]))
        yield
        inv_n = 1.0 / RWKV_HEAD_DIM

        def head_mean(ts):
            flat = jnp.concatenate([t.astype(BF16) for t in ts], axis=0)
            m = _dot(flat, ones_bd) * inv_n
            return [m[C * p:C * (p + 1)] for p in P]

        mean = head_mean(y)
        d = [y[p] - mean[p] for p in P]
        yield
        var = head_mean([d[p] * d[p] for p in P])
        for p in P:
            sl = slice(LANES * p, LANES * (p + 1))
            yn = d[p] * lax.rsqrt(var[p] + LNX_EPS) * vec(_V_LNW, sl) + vec(_V_LNB, sl)
            o_ref[rows, sl] = ((yn + bon_ref[p]) * _silu(g_ref[rows, sl])).astype(BF16)
        yield

    for _ in prepare(0):
        pass
    for c in range(NC):
        side = prepare(c + 1) if c + 1 < NC else iter(())
        _interleave(recur(c), side, RECUR_SIDE_SCHEDULE)

    prev_ref[0:1, :] = r_ref[TB - 1:TB, :]
    prev_ref[1:2, :] = k_ref[TB - 1:TB, :]
    prev_ref[2:3, :] = v_ref[TB - 1:TB, :]
    prevl_ref[0:1, :] = lo_ref[TB - 1:TB, :]


def _rwkv(proj_main, proj_lora, vec_tab, mu_l, w_lora_split):
    T = proj_main.shape[0]
    C = CHUNK
    TB = RWKV_CHUNKS_PER_STEP * C
    assert T % TB == 0
    pair_bf = lambda rows: pltpu.VMEM((N_PAIRS, rows, LANES), BF16)
    pair_f32 = lambda rows: pltpu.VMEM((N_PAIRS, rows, LANES), F32)
    slot = [pair_bf(2 * C), pair_bf(C), pair_bf(C), pair_bf(C), pair_f32(C), pair_bf(2 * C),
            pair_f32(8), pair_f32(C)]
    col = lambda j: pl.BlockSpec((TB, COL_BLOCK), lambda c: (c, j))
    whole = lambda shape: pl.BlockSpec(shape, lambda c: (0,) * len(shape))
    return pl.pallas_call(
        _rwkv_kernel,
        grid=(T // TB,),
        in_specs=[col(R_COL), col(K_COL), col(V_COL), col(GATE_R_COL),
                  pl.BlockSpec((TB, LANES), lambda c: (c, 0)),
                  whole((N_VEC, RWKV_WIDTH)), whole((1, LANES))]
                 + [whole((LANES, RWKV_WIDTH))] * 4,
        out_specs=pl.BlockSpec((TB, RWKV_WIDTH), lambda c: (c, 0)),
        out_shape=jax.ShapeDtypeStruct((T, RWKV_WIDTH), BF16),
        scratch_shapes=[
            pltpu.VMEM((N_PAIRS, LANES, LANES), F32),
            pltpu.VMEM((8, RWKV_WIDTH), F32),
            pltpu.VMEM((8, LANES), F32),
            pltpu.VMEM((TB, RWKV_WIDTH), F32),
            pltpu.VMEM((TB, RWKV_WIDTH), F32),
            pair_f32(C), pair_f32(C), pair_f32(C),
            pltpu.VMEM((2 * N_PAIRS, C, LANES), BF16),
            pltpu.VMEM((2 * N_PAIRS, C, LANES), F32),
        ] + slot + slot,
        compiler_params=pltpu.CompilerParams(
            dimension_semantics=("arbitrary",), vmem_limit_bytes=VMEM_LIMIT),
        name="rwkv",
    )(proj_main, proj_main, proj_main, proj_main, proj_lora, vec_tab, mu_l, *w_lora_split)


def _moba_prep_kernel(q_ref, k_ref, v_ref, invf_ref, qs_ref, kr_ref, vt_ref, bias_ref, shift_ref, margin_ref,
                      km_ref, kmax_ref, cs_ref):
    b = pl.program_id(0)
    BS = MOBA_BLOCK
    NR = bias_ref.shape[2]
    lane = lax.broadcasted_iota(jnp.int32, (BS, LANES), 1)

    @pl.when(b == 0)
    def _():
        km_ref[...] = jnp.zeros_like(km_ref)
        kmax_ref[...] = jnp.zeros_like(kmax_ref)
        local = lax.broadcasted_iota(jnp.int32, (BS, LANES), 0).astype(F32) * invf_ref[...]
        cs_ref[0] = jnp.cos(local)
        cs_ref[1] = jnp.sin(local)

    base = (b * BS).astype(F32) * invf_ref[...]
    cb, sb = jnp.cos(base), jnp.sin(base)
    cos = cb * cs_ref[0] - sb * cs_ref[1]
    sin = sb * cs_ref[0] + cb * cs_ref[1]
    first = lane < ROT_HALF
    second = (lane >= ROT_PAIR_LANE) & (lane < ROT_PAIR_LANE + ROT_HALF)
    cos_f = jnp.where(first | second, cos, 1.0)
    sin_t = jnp.where(first, sin, jnp.where(second, -sin, 0.0))

    def rope(x):
        return x * cos_f + pltpu.roll(x * sin_t, ROT_PAIR_LANE, axis=1)

    jidx = lax.broadcasted_iota(jnp.int32, (NR, BS), 0)
    past = jidx < b
    past_bias = jnp.where(past, 0.0, NEG_INF)
    own_bias = jnp.where(jidx == b, 0.0, NEG_INF)
    ones8 = jnp.ones((8, LANES), BF16)
    scale = MOBA_HEAD_DIM ** -0.5
    ones_rows = jnp.where(lax.broadcasted_iota(jnp.int32, (V_AUG_ROWS - LANES, BS), 0) == 0, 1.0, 0.0)
    for h in range(MOBA_HEADS):
        sl = slice(LANES * h, LANES * (h + 1))
        qh = rope(q_ref[:, sl])
        kh = rope(k_ref[:, sl])
        qb = (qh * (scale * LOG2E)).astype(BF16)
        kb = kh.astype(BF16)
        qs_ref[h] = qb
        kr_ref[h, 0] = kb
        vt_ref[h, 0] = jnp.concatenate([v_ref[:, sl].T, ones_rows], axis=0).astype(BF16)
        gate = _dot_nt(km_ref[h], qh, lax.Precision.HIGHEST)
        gate = jnp.where(past, gate, NEG_INF)
        bias = jnp.full((NR, BS), NEG_INF, F32)
        for _ in range(MOBA_TOPK):
            m = jnp.max(gate, axis=0, keepdims=True)
            first = jnp.min(jnp.where(gate == m, jidx, NR), axis=0, keepdims=True)
            hit = jidx == first
            bias = jnp.where(hit, past_bias, bias)
            gate = jnp.where(hit, -3e38, gate)
        bias_ref[h, 0] = jnp.maximum(bias, own_bias)
        km_ref[h, pl.ds(b, 1), :] = jnp.mean(kh, axis=0, keepdims=True)
        qf = qb.astype(F32)
        kf = kb.astype(F32)
        qn = jnp.sqrt(_dot_nt(ones8, (qf * qf).astype(BF16)))
        kn2 = jnp.max(_dot_nt(ones8, (kf * kf).astype(BF16)), axis=1, keepdims=True)
        kmax = jnp.maximum(kmax_ref[h], jnp.sqrt(kn2))
        kmax_ref[h] = kmax
        bound = qn * kmax[:, 0:1] * 1.02 + 1e-3
        shift_ref[h, 0] = bound - SHIFT_HEADROOM
        own = _dot_nt(ones8, (qf * kf).astype(BF16))
        margin_ref[0, h:h + 1, :] = jnp.broadcast_to(jnp.max(bound - own, axis=1, keepdims=True)[0:1], (1, LANES))


def _moba_prep(proj_main, invf):
    T = proj_main.shape[0]
    nb = T // MOBA_BLOCK
    NR = -(-nb // 8) * 8
    BS = MOBA_BLOCK
    H = MOBA_HEADS
    return pl.pallas_call(
        _moba_prep_kernel,
        grid=(nb,),
        in_specs=[
            pl.BlockSpec((BS, COL_BLOCK), lambda b: (b, Q_M_COL)),
            pl.BlockSpec((BS, COL_BLOCK), lambda b: (b, K_M_COL)),
            pl.BlockSpec((BS, COL_BLOCK), lambda b: (b, V_M_COL)),
            pl.BlockSpec((1, LANES), lambda b: (0, 0)),
        ],
        out_specs=[
            pl.BlockSpec((H, BS, LANES), lambda b: (0, b, 0)),
            pl.BlockSpec((H, 1, BS, LANES), lambda b: (0, b, 0, 0)),
            pl.BlockSpec((H, 1, V_AUG_ROWS, BS), lambda b: (0, b, 0, 0)),
            pl.BlockSpec((H, 1, NR, BS), lambda b: (0, b, 0, 0)),
            pl.BlockSpec((H, 1, 8, BS), lambda b: (0, b, 0, 0)),
            pl.BlockSpec((1, H, LANES), lambda b: (b, 0, 0)),
        ],
        out_shape=[
            jax.ShapeDtypeStruct((H, T, LANES), BF16),
            jax.ShapeDtypeStruct((H, nb, BS, LANES), BF16),
            jax.ShapeDtypeStruct((H, nb, V_AUG_ROWS, BS), BF16),
            jax.ShapeDtypeStruct((H, nb, NR, BS), F32),
            jax.ShapeDtypeStruct((H, nb, 8, BS), F32),
            jax.ShapeDtypeStruct((nb, H, LANES), F32),
        ],
        scratch_shapes=[pltpu.VMEM((H, NR, LANES), F32), pltpu.VMEM((H, 8, LANES), F32),
                        pltpu.VMEM((2, BS, LANES), F32)],
        compiler_params=pltpu.CompilerParams(
            dimension_semantics=("arbitrary",), vmem_limit_bytes=VMEM_LIMIT),
        name="moba_prep",
    )(proj_main, proj_main, proj_main, invf)


def _moba_attn_kernel(fast_ref, q_ref, k_ref, vt_ref, bias_ref, shift_ref, g_ref, o_ref, m_ref, l_ref, acc_ref):
    hg = pl.program_id(0)
    i = pl.program_id(1)
    BS = MOBA_BLOCK
    G = range(ATTN_HEADS)
    kidx = lax.broadcasted_iota(jnp.int32, (BS, BS), 0)
    qidx = lax.broadcasted_iota(jnp.int32, (BS, BS), 1)
    causal = kidx <= qidx

    def scores(g, j, shifted):
        row = bias_ref[g, 0, pl.ds(j, 1), :]
        if shifted:
            row = row - shift_ref[g, 0, 0:1, :]
        return _dot_nt(k_ref[g, j], q_ref[g]) + row

    def write_out(g, acc, denom):
        sl = slice(LANES * g, LANES * (g + 1))
        o_ref[:, sl] = ((acc / denom).T * _silu(g_ref[:, sl])).astype(BF16)

    @pl.when(fast_ref[hg, i] != 0)
    def _():
        def tiles(js, causal_flags):
            s = [[scores(g, j, True) for g in G] for j in js]
            s = [[jnp.where(causal, sg, NEG_INF) if flag else sg for sg in sj]
                 for sj, flag in zip(s, causal_flags)]
            p = [[jnp.exp2(sg).astype(BF16) for sg in sj] for sj in s]
            pv = [[_dot(vt_ref[g, j], p[u][g]) for g in G] for u, j in enumerate(js)]
            return [functools.reduce(lambda a, b: a + b, [pv[u][g] for u in range(len(js))]) for g in G]

        n_main = i // ATTN_UNROLL
        left = i - n_main * ATTN_UNROLL
        for r in range(ATTN_UNROLL):
            @pl.when(left == r)
            def _(r=r):
                first = tiles([i] + [n_main * ATTN_UNROLL + u for u in range(r)], [True] + [False] * r)
                for g in G:
                    acc_ref[g] = first[g]

        def main_body(t, carry):
            pv = tiles([t * ATTN_UNROLL + u for u in range(ATTN_UNROLL)], [False] * ATTN_UNROLL)
            for g in G:
                acc_ref[g] = acc_ref[g] + pv[g]
            return carry

        lax.fori_loop(0, n_main, main_body, 0)
        for g in G:
            acc = acc_ref[g]
            write_out(g, acc[0:LANES], acc[LANES:LANES + 1])

    @pl.when(fast_ref[hg, i] == 0)
    def _():
        for g in G:
            s = jnp.where(causal, scores(g, i, False), NEG_INF)
            m = jnp.max(s, axis=0, keepdims=True)
            p = jnp.exp2(s - m)
            m_ref[g] = m
            l_ref[g] = jnp.sum(p, axis=0, keepdims=True)
            acc_ref[g] = _dot(vt_ref[g, i], p.astype(BF16))

        def body(j, carry):
            s = [scores(g, j, False) for g in G]
            pb, alpha = [], []
            for g in G:
                m = m_ref[g]
                m_new = jnp.maximum(m, jnp.max(s[g], axis=0, keepdims=True))
                a = jnp.exp2(m - m_new)
                p = jnp.exp2(s[g] - m_new)
                m_ref[g] = m_new
                l_ref[g] = a * l_ref[g] + jnp.sum(p, axis=0, keepdims=True)
                pb.append(p.astype(BF16))
                alpha.append(a)
            pv = [_dot(vt_ref[g, j], pb[g]) for g in G]
            for g in G:
                acc_ref[g] = alpha[g] * acc_ref[g] + pv[g]
            return carry

        lax.fori_loop(0, i, body, 0)
        for g in G:
            write_out(g, acc_ref[g][0:LANES], l_ref[g])


def _moba_attn(fast, qs, kr, vt, bias, shift, proj_main):
    H, T, _ = qs.shape
    nb = T // MOBA_BLOCK
    BS = MOBA_BLOCK
    G = ATTN_HEADS
    NR = bias.shape[2]
    resident = pl.Buffered(1)
    grid_spec = pltpu.PrefetchScalarGridSpec(
        num_scalar_prefetch=1,
        grid=(H // G, nb),
        in_specs=[
            pl.BlockSpec((G, BS, LANES), lambda h, i, f: (h, i, 0)),
            pl.BlockSpec((G, nb, BS, LANES), lambda h, i, f: (h, 0, 0, 0), pipeline_mode=resident),
            pl.BlockSpec((G, nb, V_AUG_ROWS, BS), lambda h, i, f: (h, 0, 0, 0), pipeline_mode=resident),
            pl.BlockSpec((G, 1, NR, BS), lambda h, i, f: (h, i, 0, 0)),
            pl.BlockSpec((G, 1, 8, BS), lambda h, i, f: (h, i, 0, 0)),
            pl.BlockSpec((BS, G * LANES), lambda h, i, f: (i, GATE_M_COL * (H // G) + h)),
        ],
        out_specs=pl.BlockSpec((BS, G * LANES), lambda h, i, f: (i, h)),
        scratch_shapes=[
            pltpu.VMEM((G, 1, BS), F32), pltpu.VMEM((G, 1, BS), F32), pltpu.VMEM((G, V_AUG_ROWS, BS), F32),
        ],
    )
    return pl.pallas_call(
        _moba_attn_kernel,
        grid_spec=grid_spec,
        out_shape=jax.ShapeDtypeStruct((T, MOBA_WIDTH), BF16),
        compiler_params=pltpu.CompilerParams(
            dimension_semantics=("arbitrary", "arbitrary"), vmem_limit_bytes=VMEM_LIMIT),
        name="moba_attn",
    )(fast, qs, kr, vt, bias, shift, proj_main)


def _out_kernel(mr_ref, mm_ref, x_ref, wt_ref, wb_ref, fw_ref, o_ref):
    h = x_ref[...] + _dot(mr_ref[...], wt_ref[...]) + _dot(mm_ref[...], wb_ref[...])
    ms = jnp.mean(h * h, axis=-1, keepdims=True)
    o_ref[...] = h * lax.rsqrt(ms + RMS_EPS) * fw_ref[...]


def _out_proj(mixed_r, mixed_m, x2, w_out, fw, tm):
    T = x2.shape[0]
    assert T % tm == 0
    return pl.pallas_call(
        _out_kernel,
        grid=(T // tm,),
        in_specs=[
            pl.BlockSpec((tm, RWKV_WIDTH), lambda i: (i, 0)),
            pl.BlockSpec((tm, MOBA_WIDTH), lambda i: (i, 0)),
            pl.BlockSpec((tm, D_MODEL), lambda i: (i, 0)),
            pl.BlockSpec((RWKV_WIDTH, D_MODEL), lambda i: (0, 0)),
            pl.BlockSpec((MOBA_WIDTH, D_MODEL), lambda i: (1, 0)),
            pl.BlockSpec((1, D_MODEL), lambda i: (0, 0)),
        ],
        out_specs=pl.BlockSpec((tm, D_MODEL), lambda i: (i, 0)),
        out_shape=jax.ShapeDtypeStruct((T, D_MODEL), F32),
        compiler_params=pltpu.CompilerParams(
            dimension_semantics=("arbitrary",), vmem_limit_bytes=VMEM_LIMIT),
        name="out_proj",
    )(mixed_r, mixed_m, x2, w_out, w_out, fw)


def kernel(x, norm_w, w_in, mu_rkv, mu_lora, w_decay_up, w_decay_bias, w_iclr_up, w_iclr_bias,
           k_k, k_a, r_k, lnx_w, lnx_b, w_out, final_norm_w):
    B, T, _ = x.shape
    assert B == 1 and T % MOBA_BLOCK == 0
    x2 = x.reshape(T, D_MODEL)

    w_main, w_lora = _weight_relayout(w_in)
    tm_proj = min(1024, T)
    proj_main, proj_lora = _project(x2, norm_w[0:1], w_main, w_lora, tm_proj)

    rows = [mu_rkv[0, 0], mu_rkv[0, 1], mu_rkv[0, 2], w_decay_bias[0], w_iclr_bias[0], k_k[0], k_a[0],
            r_k[0].reshape(-1), lnx_w[0], lnx_b[0]]
    vec_tab = jnp.stack(rows + [jnp.zeros_like(rows[0])] * (N_VEC - len(rows)), axis=0)
    mu_l = mu_lora[0].reshape(1, 2 * LORA_RANK)
    zeros_up = jnp.zeros_like(w_decay_up[0])
    wdec = jnp.concatenate([w_decay_up[0], zeros_up], axis=0)
    wicl = jnp.concatenate([zeros_up, w_iclr_up[0]], axis=0)
    w_lora_split = []
    for w_up in (wdec, wicl):
        w_hi = w_up.astype(BF16)
        w_lora_split += [w_hi, (w_up - w_hi.astype(F32)).astype(BF16)]
    mixed_r = _rwkv(proj_main, proj_lora, vec_tab, mu_l, w_lora_split)

    lane = jnp.arange(LANES)
    inv_freq = ROPE_THETA ** (-(lane % ROT_HALF).astype(F32) / ROT_HALF)
    qs, kr, vt, bias, shift, margin = _moba_prep(proj_main, inv_freq.reshape(1, LANES))
    ok = (margin[:, :, 0] <= FAST_WINDOW).reshape(-1, MOBA_HEADS // ATTN_HEADS, ATTN_HEADS)
    fast = jnp.all(ok, axis=-1).T.astype(jnp.int32)
    mixed_m = _moba_attn(fast, qs, kr, vt, bias, shift, proj_main)

    tm_out = min(512, T)
    out = _out_proj(mixed_r, mixed_m, x2, w_out[0].astype(BF16), final_norm_w.reshape(1, D_MODEL), tm_out)
    return out.reshape(B, T, D_MODEL)
```
